```python
import jax, jax.numpy as jnp
from jax import lax
import numpy as np

D_MODEL = 1024
BATCH = 32
SEQ = 2048
DEPTH = 1

MEM_LEN = 256
BLOCK = 128
WINDOW = 128
EPS = 1e-6
DA_HEADS = 8
DA_HEAD_DIM = 64
DA_V_DIM = 2 * DA_HEAD_DIM
WA_HEADS = 16
WA_KV_HEADS = 4
WA_HEAD_DIM = 64
XA_HEADS = 4
XA_HEAD_DIM = 256
N_BRANCH = 3
DA_QK_W = DA_HEADS * 2 * DA_HEAD_DIM
DA_V_W = DA_HEADS * DA_V_DIM
WA_Q_W = WA_HEADS * WA_HEAD_DIM
WA_KV_W = WA_KV_HEADS * WA_HEAD_DIM
XA_W = XA_HEADS * XA_HEAD_DIM
GATE_W = N_BRANCH * D_MODEL
IN_WIDTHS = (DA_QK_W, DA_QK_W, DA_V_W, WA_Q_W, WA_KV_W, WA_KV_W, XA_W, GATE_W)
IN_W = sum(IN_WIDTHS)
N_EXPERTS = 16
EC_FACTOR = 2
D_EXPERT = 2048
LAMBDA_STD = 0.1

kernel_name = 'hybrid_diffattn_swa_memxattn_ecmoe'


def rmsnorm(x, g):
    xf = x.astype(jnp.float32)
    y = xf * lax.rsqrt(jnp.mean(xf * xf, axis=-1, keepdims=True) + EPS)
    return (y * g.astype(jnp.float32)).astype(x.dtype)


def alibi_slopes(n_heads):
    return jnp.exp2(-8.0 * jnp.arange(1, n_heads + 1, dtype=jnp.float32) / n_heads)


def lambda_init(layer):
    return 0.8 - 0.6 * float(np.exp(-0.3 * layer))


def diff_attention(q, k, v, lam, subln_g, lam_init):
    b, s = q.shape[0], q.shape[1]
    nb = s // BLOCK
    scale = DA_HEAD_DIM ** -0.5
    slopes = alibi_slopes(DA_HEADS)[None, :, None, None, None]
    kpos = jnp.arange(s)
    qb = jnp.moveaxis(q.reshape(b, nb, BLOCK, DA_HEADS, 2, DA_HEAD_DIM), 1, 0)

    def one_block(args):
        qi, i = args
        sc = jnp.einsum('bqhmd,bkhmd->bhmqk', qi, k).astype(jnp.float32) * scale
        qpos = i * BLOCK + jnp.arange(BLOCK)
        dist = jnp.abs(qpos[:, None] - kpos[None, :]).astype(jnp.float32)
        p = jax.nn.softmax(sc - slopes * dist, axis=-1)
        a = p[:, :, 0] - lam * p[:, :, 1]
        return jnp.einsum('bhqk,bkhe->bqhe', a.astype(v.dtype), v)

    o = lax.map(one_block, (qb, jnp.arange(nb)))
    o = jnp.moveaxis(o, 0, 1).reshape(b, s, DA_HEADS, DA_V_DIM)
    o = rmsnorm(o, subln_g) * (1.0 - lam_init)
    return o.reshape(b, s, DA_V_W)


def window_attention(q, k, v, sink):
    b, s = q.shape[0], q.shape[1]
    nb = s // BLOCK
    span = BLOCK + 2 * WINDOW
    groups = WA_HEADS // WA_KV_HEADS
    scale = WA_HEAD_DIM ** -0.5
    slopes = alibi_slopes(WA_HEADS).reshape(WA_KV_HEADS, groups)[None, :, :, None, None]
    sink_logit = sink.astype(jnp.float32).reshape(WA_KV_HEADS, groups)[None, :, :, None, None]
    pad = ((0, 0), (WINDOW, WINDOW), (0, 0), (0, 0))
    kp = jnp.pad(k, pad)
    vp = jnp.pad(v, pad)
    qb = jnp.moveaxis(q.reshape(b, nb, BLOCK, WA_KV_HEADS, groups, WA_HEAD_DIM), 1, 0)

    def one_block(args):
        qi, i = args
        start = i * BLOCK
        kb = lax.dynamic_slice_in_dim(kp, start, span, axis=1)
        vb = lax.dynamic_slice_in_dim(vp, start, span, axis=1)
        sc = jnp.einsum('bqhgd,bkhd->bhgqk', qi, kb).astype(jnp.float32) * scale
        qpos = start + jnp.arange(BLOCK)
        kpos = start - WINDOW + jnp.arange(span)
        dist = jnp.abs(qpos[:, None] - kpos[None, :])
        valid = (dist <= WINDOW) & ((kpos >= 0) & (kpos < s))[None, :]
        sc = jnp.where(valid, sc - slopes * dist.astype(jnp.float32), -jnp.inf)
        sinks = jnp.broadcast_to(sink_logit, sc.shape[:-1] + (1,))
        p = jax.nn.softmax(jnp.concatenate([sc, sinks], axis=-1), axis=-1)[..., :span]
        return jnp.einsum('bhgqk,bkhd->bqhgd', p.astype(vb.dtype), vb)

    o = lax.map(one_block, (qb, jnp.arange(nb)))
    return jnp.moveaxis(o, 0, 1).reshape(b, s, WA_Q_W)


def memory_attention(q, mk, mv):
    sc = jnp.einsum('bqhd,bkhd->bhqk', q, mk).astype(jnp.float32) * (XA_HEAD_DIM ** -0.5)
    p = jax.nn.softmax(sc, axis=-1)
    o = jnp.einsum('bhqk,bkhd->bqhd', p.astype(mv.dtype), mv)
    return o.reshape(o.shape[0], o.shape[1], XA_W)


def expert_choice_ffn(h, w_router, w_gate, w_up, w_down):
    b, s, _ = h.shape
    cap = max(1, EC_FACTOR * s // N_EXPERTS)
    logits = jnp.einsum('bsd,de->bse', h, w_router).astype(jnp.float32)
    aff = jax.nn.softmax(logits, axis=-1)
    g, idx = lax.top_k(jnp.swapaxes(aff, 1, 2), cap)
    bidx = jnp.arange(b)[:, None, None]
    xe = h[bidx, idx]
    a = jnp.einsum('becd,edf->becf', xe, w_gate)
    u = jnp.einsum('becd,edf->becf', xe, w_up)
    ye = jnp.einsum('becf,efd->becd', jax.nn.silu(a) * u, w_down)
    ye = ye * g[..., None].astype(ye.dtype)
    return jnp.zeros_like(h).at[bidx, idx].add(ye)


def hybrid_layer(x, mem, layer, attn_norm_g, mem_norm_g, w_in, w_mem_kv,
                 da_lambda_q1, da_lambda_k1, da_lambda_q2, da_lambda_k2, da_subln_g,
                 wa_sink, w_da_o, w_wa_o, w_xa_o, w_out, ffn_norm_g,
                 w_router, w_exp_gate, w_exp_up, w_exp_down):
    b, s, d = x.shape
    h = rmsnorm(x, attn_norm_g)
    m = rmsnorm(mem, mem_norm_g)
    proj = h @ w_in
    offsets = np.cumsum(IN_WIDTHS)[:-1].tolist()
    da_q, da_k, da_v, wa_q, wa_k, wa_v, xa_q, gates = jnp.split(proj, offsets, axis=-1)

    lam_init = lambda_init(layer)
    lam = (jnp.exp(jnp.sum(da_lambda_q1.astype(jnp.float32) * da_lambda_k1.astype(jnp.float32)))
           - jnp.exp(jnp.sum(da_lambda_q2.astype(jnp.float32) * da_lambda_k2.astype(jnp.float32)))
           + lam_init)
    o_da = diff_attention(da_q.reshape(b, s, DA_HEADS, 2, DA_HEAD_DIM),
                          da_k.reshape(b, s, DA_HEADS, 2, DA_HEAD_DIM),
                          da_v.reshape(b, s, DA_HEADS, DA_V_DIM),
                          lam, da_subln_g, lam_init)

    o_wa = window_attention(wa_q.reshape(b, s, WA_HEADS, WA_HEAD_DIM),
                            wa_k.reshape(b, s, WA_KV_HEADS, WA_HEAD_DIM),
                            wa_v.reshape(b, s, WA_KV_HEADS, WA_HEAD_DIM),
                            wa_sink)

    mk, mv = jnp.split(m @ w_mem_kv, 2, axis=-1)
    n_mem = mem.shape[1]
    o_xa = memory_attention(xa_q.reshape(b, s, XA_HEADS, XA_HEAD_DIM),
                            mk.reshape(b, n_mem, XA_HEADS, XA_HEAD_DIM),
                            mv.reshape(b, n_mem, XA_HEADS, XA_HEAD_DIM))

    gate = jax.nn.sigmoid(gates.reshape(b, s, N_BRANCH, d))
    merged = (gate[:, :, 0] * (o_da @ w_da_o)
              + gate[:, :, 1] * (o_wa @ w_wa_o)
              + gate[:, :, 2] * (o_xa @ w_xa_o))
    x = x + merged @ w_out

    h2 = rmsnorm(x, ffn_norm_g)
    return x + expert_choice_ffn(h2, w_router, w_exp_gate, w_exp_up, w_exp_down)


def setup_inputs(seed: int = 0) -> dict:
    key = jax.random.key(seed)
    ks = jax.random.split(key, 24)
    L, D = DEPTH, D_MODEL

    def w(k, shape, fan_in):
        return jax.random.normal(k, shape, jnp.float32) * (fan_in ** -0.5)

    def gain(k, shape):
        return 1.0 + 0.05 * jax.random.normal(k, shape, jnp.float32)

    def small(k, shape, scale):
        return scale * jax.random.normal(k, shape, jnp.float32)

    return {
        'x': jax.random.normal(ks[0], (BATCH, SEQ, D), jnp.float32),
        'mem': jax.random.normal(ks[1], (BATCH, MEM_LEN, D), jnp.float32),
        'attn_norm_g': gain(ks[2], (L, D)),
        'mem_norm_g': gain(ks[3], (L, D)),
        'w_in': w(ks[4], (L, D, IN_W), D),
        'w_mem_kv': w(ks[5], (L, D, 2 * XA_W), D),
        'da_lambda_q1': small(ks[6], (L, DA_HEAD_DIM), LAMBDA_STD),
        'da_lambda_k1': small(ks[7], (L, DA_HEAD_DIM), LAMBDA_STD),
        'da_lambda_q2': small(ks[8], (L, DA_HEAD_DIM), LAMBDA_STD),
        'da_lambda_k2': small(ks[9], (L, DA_HEAD_DIM), LAMBDA_STD),
        'da_subln_g': gain(ks[10], (L, DA_V_DIM)),
        'wa_sink': small(ks[11], (L, WA_HEADS), 0.5),
        'w_da_o': w(ks[12], (L, DA_V_W, D), DA_V_W),
        'w_wa_o': w(ks[13], (L, WA_Q_W, D), WA_Q_W),
        'w_xa_o': w(ks[14], (L, XA_W, D), XA_W),
        'w_out': w(ks[15], (L, D, D), D),
        'ffn_norm_g': gain(ks[16], (L, D)),
        'w_router': w(ks[17], (L, D, N_EXPERTS), D),
        'w_exp_gate': w(ks[18], (L, N_EXPERTS, D, D_EXPERT), D),
        'w_exp_up': w(ks[19], (L, N_EXPERTS, D, D_EXPERT), D),
        'w_exp_down': w(ks[20], (L, N_EXPERTS, D_EXPERT, D), D_EXPERT),
        'final_norm_g': gain(ks[21], (D,)),
    }


def reference(x, mem, attn_norm_g, mem_norm_g, w_in, w_mem_kv,
              da_lambda_q1, da_lambda_k1, da_lambda_q2, da_lambda_k2, da_subln_g,
              wa_sink, w_da_o, w_wa_o, w_xa_o, w_out, ffn_norm_g,
              w_router, w_exp_gate, w_exp_up, w_exp_down, final_norm_g):
    for l in range(DEPTH):
        x = hybrid_layer(x, mem, l, attn_norm_g[l], mem_norm_g[l], w_in[l], w_mem_kv[l],
                         da_lambda_q1[l], da_lambda_k1[l], da_lambda_q2[l], da_lambda_k2[l],
                         da_subln_g[l], wa_sink[l], w_da_o[l], w_wa_o[l], w_xa_o[l], w_out[l],
                         ffn_norm_g[l], w_router[l], w_exp_gate[l], w_exp_up[l], w_exp_down[l])
    return rmsnorm(x, final_norm_g)
```

```python
import functools

import numpy as np
import jax
import jax.numpy as jnp
from jax import lax
from jax.experimental import pallas as pl
from jax.experimental.pallas import tpu as pltpu

EPS = 1e-6
BLOCK = 128
WINDOW = 128
DA_HEADS = 8
DA_HEAD_DIM = 64
DA_V_DIM = 2 * DA_HEAD_DIM
WA_HEADS = 16
WA_KV_HEADS = 4
WA_GROUPS = WA_HEADS // WA_KV_HEADS
WA_HEAD_DIM = 64
XA_HEADS = 4
XA_HEAD_DIM = 256
N_BRANCH = 3
N_EXPERTS = 16
EC_FACTOR = 2

LANES = 128
VMEM_LIMIT = 56 * 1024 * 1024

F32 = jnp.float32
BF16 = jnp.bfloat16
NEG_INF = float("-inf")


def _cparams(sem):
    return pltpu.CompilerParams(dimension_semantics=sem, vmem_limit_bytes=VMEM_LIMIT)


def _rms(xf, g_row):
    ms = jnp.mean(xf * xf, axis=-1, keepdims=True)
    return xf * lax.rsqrt(ms + EPS) * g_row


def _dot(a, b):
    return jnp.dot(a, b, preferred_element_type=F32)


def _dot_nt(a, b):
    return lax.dot_general(a, b, (((1,), (1,)), ((), ())), preferred_element_type=F32)


def _dot_tn(a, b):
    return lax.dot_general(a, b, (((0,), (0,)), ((), ())), preferred_element_type=F32)


def _resident(shape, index_map):
    return pl.BlockSpec(shape, index_map, pipeline_mode=pl.Buffered(1))


def _in_proj_kernel(x_ref, g_ref, w_ref, daq_ref, dak_ref, dav_ref, waq_ref, wak_ref, wav_ref, xaq_ref):
    d = x_ref.shape[-1]
    h = _rms(x_ref[0], g_ref[...]).astype(BF16)

    def sec(i):
        return _dot(h, w_ref[:, i * d:(i + 1) * d])

    for ref, i in ((daq_ref, 0), (dak_ref, 1), (dav_ref, 2)):
        r = sec(i).astype(BF16)
        for hh in range(DA_HEADS):
            ref[0, hh] = r[:, hh * LANES:(hh + 1) * LANES]
    waq_ref[0] = sec(3).astype(BF16)
    r = sec(4).astype(BF16)
    half = d // 2
    wak_ref[0] = r[:, :half]
    wav_ref[0] = r[:, half:]
    xaq_ref[0] = sec(5).astype(BF16)


def _in_proj(x, g, w1, tm):
    B, S, D = x.shape
    n_t = S // tm
    head_shape = jax.ShapeDtypeStruct((B, DA_HEADS, S, LANES), BF16)
    head_spec = pl.BlockSpec((1, DA_HEADS, tm, LANES), lambda b, i: (b, 0, i, 0))
    tok = lambda w: pl.BlockSpec((1, tm, w), lambda b, i: (b, i, 0))
    return pl.pallas_call(
        _in_proj_kernel,
        grid=(B, n_t),
        in_specs=[tok(D), _resident((1, D), lambda b, i: (0, 0)), _resident(w1.shape, lambda b, i: (0, 0))],
        out_specs=[head_spec, head_spec, head_spec, tok(D), tok(D // 2), tok(D // 2), tok(D)],
        out_shape=[head_shape, head_shape, head_shape,
                   jax.ShapeDtypeStruct((B, S, D), BF16),
                   jax.ShapeDtypeStruct((B, S, D // 2), BF16),
                   jax.ShapeDtypeStruct((B, S, D // 2), BF16),
                   jax.ShapeDtypeStruct((B, S, D), BF16)],
        compiler_params=_cparams(("parallel", "parallel")),
        name="in_proj",
    )(x, g, w1)


def _mem_proj_kernel(m_ref, g_ref, w_ref, o_ref):
    h = _rms(m_ref[0], g_ref[...]).astype(BF16)
    o_ref[0] = _dot(h, w_ref[...]).astype(BF16)


def _mem_proj(mem, g, w):
    B, M, D = mem.shape
    N = w.shape[1]
    return pl.pallas_call(
        _mem_proj_kernel,
        grid=(B,),
        in_specs=[pl.BlockSpec((1, M, D), lambda b: (b, 0, 0)),
                  _resident((1, D), lambda b: (0, 0)),
                  _resident((D, N), lambda b: (0, 0))],
        out_specs=pl.BlockSpec((1, M, N), lambda b: (b, 0, 0)),
        out_shape=jax.ShapeDtypeStruct((B, M, N), BF16),
        compiler_params=_cparams(("parallel",)),
        name="mem_proj",
    )(mem, g, w)


def _diff_attn_kernel(lq1_ref, lk1_ref, lq2_ref, lk2_ref, q_ref, k_ref, v_ref, bias_ref, g_ref, o_ref,
                      k1_scr, k2_scr, vt_scr, *, lam_init):
    S = k_ref.shape[2]
    tq = q_ref.shape[2]
    t = pl.program_id(2)

    @pl.when(t == 0)
    def _():
        kk = k_ref[0, 0]
        lane = lax.broadcasted_iota(jnp.int32, kk.shape, 1)
        zero = jnp.zeros_like(kk)
        k1_scr[...] = jnp.where(lane < DA_HEAD_DIM, kk, zero)
        k2_scr[...] = jnp.where(lane >= DA_HEAD_DIM, kk, zero)
        vt_scr[...] = v_ref[0, 0].astype(F32).T.astype(BF16)

    lam = (jnp.exp(jnp.sum(lq1_ref[...] * lk1_ref[...], keepdims=True))
           - jnp.exp(jnp.sum(lq2_ref[...] * lk2_ref[...], keepdims=True)) + lam_init)

    q = (q_ref[0, 0].astype(F32) * (DA_HEAD_DIM ** -0.5)).astype(BF16)
    r0 = pl.multiple_of((S - tq) - t * tq, tq)
    bias = bias_ref[0, pl.ds(r0, S), :]

    def softmax_t(k_scr):
        s = _dot_nt(k_scr[...], q) + bias
        m = jnp.max(s, axis=0, keepdims=True)
        p = jnp.exp(s - m)
        return p, jnp.sum(p, axis=0, keepdims=True)

    p1, l1 = softmax_t(k1_scr)
    p2, l2 = softmax_t(k2_scr)
    a = (p1 * (1.0 / l1) - p2 * (lam / l2)).astype(BF16)
    ot = _dot(vt_scr[...], a)
    ms = jnp.mean(ot * ot, axis=0, keepdims=True)
    y = ot * lax.rsqrt(ms + EPS) * g_ref[...] * (1.0 - lam_init)
    o_ref[0] = y.T.astype(BF16)


def _diff_attn(daq, dak, dav, lams, subln_g, lam_init, tq):
    B, H, S, _ = daq.shape
    n_t = S // tq
    rows = 2 * S - tq
    slopes = jnp.exp2(-8.0 * jnp.arange(1, H + 1, dtype=F32) / H)
    r = jnp.arange(rows, dtype=jnp.int32)[:, None]
    c = jnp.arange(tq, dtype=jnp.int32)[None, :]
    dist = jnp.abs(r - (S - tq) - c).astype(F32)
    bias = -(slopes[:, None, None] * dist[None])
    lam_spec = _resident((1, DA_HEAD_DIM), lambda b, h, t: (0, 0))
    kv_spec = pl.BlockSpec((1, 1, S, LANES), lambda b, h, t: (b, h, 0, 0))
    return pl.pallas_call(
        functools.partial(_diff_attn_kernel, lam_init=lam_init),
        grid=(B, H, n_t),
        in_specs=[lam_spec, lam_spec, lam_spec, lam_spec,
                  pl.BlockSpec((1, 1, tq, LANES), lambda b, h, t: (b, h, t, 0)),
                  kv_spec, kv_spec,
                  pl.BlockSpec((1, rows, tq), lambda b, h, t: (h, 0, 0)),
                  _resident((DA_V_DIM, 1), lambda b, h, t: (0, 0))],
        out_specs=pl.BlockSpec((1, tq, LANES), lambda b, h, t: (b, t, h)),
        out_shape=jax.ShapeDtypeStruct((B, S, H * LANES), BF16),
        scratch_shapes=[pltpu.VMEM((S, LANES), BF16), pltpu.VMEM((S, LANES), BF16),
                        pltpu.VMEM((DA_V_DIM, S), BF16)],
        compiler_params=_cparams(("parallel", "parallel", "arbitrary")),
        name="diff_attn",
    )(*lams, daq, dak, dav, bias, subln_g.reshape(DA_V_DIM, 1))


def _win_attn_kernel(sink_ref, q_ref, k_ref, v_ref, tab_ref, o_ref, klo, khi, vlo, vhi):
    S = k_ref.shape[1]
    tq = q_ref.shape[1]
    span = tq + 2 * WINDOW
    g = pl.program_id(1)
    t = pl.program_id(2)

    @pl.when(t == 0)
    def _():
        kk = k_ref[0]
        vv = v_ref[0]
        lane = lax.broadcasted_iota(jnp.int32, kk.shape, 1)
        lo = lane < WA_HEAD_DIM
        zero = jnp.zeros_like(kk)
        pad = jnp.zeros((WINDOW, LANES), BF16)
        for scr, src, keep in ((klo, kk, lo), (khi, kk, ~lo), (vlo, vv, lo), (vhi, vv, ~lo)):
            scr[0:WINDOW, :] = pad
            scr[WINDOW + S:WINDOW + S + WINDOW, :] = pad
            scr[WINDOW:WINDOW + S, :] = jnp.where(keep, src, zero)

    start = pl.multiple_of(t * tq, tq)
    kpos = start - WINDOW + lax.broadcasted_iota(jnp.int32, (1, span), 1)
    edge = jnp.where((kpos >= 0) & (kpos < S), 0.0, NEG_INF).astype(F32)
    kz = (klo[pl.ds(start, span), :], khi[pl.ds(start, span), :])
    vz = (vlo[pl.ds(start, span), :], vhi[pl.ds(start, span), :])
    for pair in range(WA_GROUPS // 2):
        qp = (q_ref[0, :, pair * LANES:(pair + 1) * LANES].astype(F32) * (WA_HEAD_DIM ** -0.5)).astype(BF16)
        acc = None
        for half in range(2):
            j = pair * 2 + half
            s = _dot_nt(qp, kz[half]) + tab_ref[j] + edge
            sink = sink_ref[g * WA_GROUPS + j]
            m = jnp.maximum(jnp.max(s, axis=1, keepdims=True), sink)
            p = jnp.exp(s - m)
            l = jnp.sum(p, axis=1, keepdims=True) + jnp.exp(sink - m)
            pn = (p * (1.0 / l)).astype(BF16)
            o = _dot(pn, vz[half])
            acc = o if acc is None else acc + o
        o_ref[0, :, pair * LANES:(pair + 1) * LANES] = acc.astype(BF16)


def _win_attn(waq, wak, wav, sink, tq):
    B, S, D = waq.shape
    n_t = S // tq
    span = tq + 2 * WINDOW
    gw = WA_GROUPS * WA_HEAD_DIM
    slopes = jnp.exp2(-8.0 * jnp.arange(1, WA_HEADS + 1, dtype=F32) / WA_HEADS)
    ir = jnp.arange(tq, dtype=jnp.int32)[:, None]
    jr = jnp.arange(span, dtype=jnp.int32)[None, :]
    dist = jnp.abs(ir + WINDOW - jr)
    tab = jnp.where((dist <= WINDOW)[None], -(slopes[:, None, None] * dist.astype(F32)[None]), NEG_INF)
    pad_shape = pltpu.VMEM((S + 2 * WINDOW, LANES), BF16)
    return pl.pallas_call(
        _win_attn_kernel,
        grid=(B, WA_KV_HEADS, n_t),
        in_specs=[pl.BlockSpec(memory_space=pltpu.SMEM),
                  pl.BlockSpec((1, tq, gw), lambda b, g, t: (b, t, g)),
                  pl.BlockSpec((1, S, LANES), lambda b, g, t: (b, 0, g)),
                  pl.BlockSpec((1, S, LANES), lambda b, g, t: (b, 0, g)),
                  pl.BlockSpec((WA_GROUPS, tq, span), lambda b, g, t: (g, 0, 0))],
        out_specs=pl.BlockSpec((1, tq, gw), lambda b, g, t: (b, t, g)),
        out_shape=jax.ShapeDtypeStruct((B, S, D), BF16),
        scratch_shapes=[pad_shape, pad_shape, pad_shape, pad_shape],
        compiler_params=_cparams(("parallel", "parallel", "arbitrary")),
        name="win_attn",
    )(sink, waq, wak, wav, tab)


def _mem_attn_kernel(q_ref, k_ref, v_ref, o_ref):
    q = (q_ref[0].astype(F32) * (XA_HEAD_DIM ** -0.5)).astype(BF16)
    s = _dot_nt(q, k_ref[0])
    m = jnp.max(s, axis=1, keepdims=True)
    p = jnp.exp(s - m)
    l = jnp.sum(p, axis=1, keepdims=True)
    o_ref[0] = _dot((p * (1.0 / l)).astype(BF16), v_ref[0]).astype(BF16)


def _mem_attn(xaq, memkv, tq):
    B, S, D = xaq.shape
    M = memkv.shape[1]
    hd = XA_HEAD_DIM
    return pl.pallas_call(
        _mem_attn_kernel,
        grid=(B, XA_HEADS, S // tq),
        in_specs=[pl.BlockSpec((1, tq, hd), lambda b, h, t: (b, t, h)),
                  pl.BlockSpec((1, M, hd), lambda b, h, t: (b, 0, h)),
                  pl.BlockSpec((1, M, hd), lambda b, h, t: (b, 0, XA_HEADS + h))],
        out_specs=pl.BlockSpec((1, tq, hd), lambda b, h, t: (b, t, h)),
        out_shape=jax.ShapeDtypeStruct((B, S, D), BF16),
        compiler_params=_cparams(("parallel", "parallel", "parallel")),
        name="mem_attn",
    )(xaq, memkv, memkv)


def _merge_kernel(x_ref, oda_ref, owa_ref, oxa_ref, g1_ref, wg_ref, wda_ref, wwa_ref, wxa_ref, wout_ref,
                  g2_ref, wr_ref, x1_ref, h2_ref, aff_ref):
    d = x_ref.shape[-1]
    x = x_ref[0]
    h = _rms(x, g1_ref[...]).astype(BF16)
    merged = None
    for i, (o_ref, w_ref) in enumerate(((oda_ref, wda_ref), (owa_ref, wwa_ref), (oxa_ref, wxa_ref))):
        gate = jax.nn.sigmoid(_dot(h, wg_ref[:, i * d:(i + 1) * d]))
        term = gate * _dot(o_ref[0], w_ref[...])
        merged = term if merged is None else merged + term
    x1 = x + _dot(merged.astype(BF16), wout_ref[...])
    x1_ref[0] = x1
    h2 = _rms(x1, g2_ref[...]).astype(BF16)
    h2_ref[0] = h2
    logits = _dot(h2, wr_ref[...])
    lt = logits.T[:N_EXPERTS, :]
    m = jnp.max(lt, axis=0, keepdims=True)
    e = jnp.exp(lt - m)
    aff_ref[0] = e / jnp.sum(e, axis=0, keepdims=True)


def _merge(x, oda, owa, oxa, g1, wg, wda, wwa, wxa, wout, g2, wr, tm):
    B, S, D = x.shape
    tok = pl.BlockSpec((1, tm, D), lambda b, i: (b, i, 0))
    full = lambda a: _resident(a.shape, lambda b, i: (0,) * a.ndim)
    return pl.pallas_call(
        _merge_kernel,
        grid=(B, S // tm),
        in_specs=[tok, tok, tok, tok, full(g1), full(wg), full(wda), full(wwa), full(wxa), full(wout),
                  full(g2), full(wr)],
        out_specs=[tok, tok, pl.BlockSpec((1, N_EXPERTS, tm), lambda b, i: (b, 0, i))],
        out_shape=[jax.ShapeDtypeStruct((B, S, D), F32),
                   jax.ShapeDtypeStruct((B, S, D), BF16),
                   jax.ShapeDtypeStruct((B, N_EXPERTS, S), F32)],
        compiler_params=_cparams(("parallel", "parallel")),
        name="merge",
    )(x, oda, owa, oxa, g1, wg, wda, wwa, wxa, wout, g2, wr)


def _route_kernel(aff_ref, pos_ref, *, cap):
    aff = aff_ref[0]
    E, S = aff.shape
    bits = pltpu.bitcast(aff, jnp.int32)
    prefix = jnp.zeros((E, 1), jnp.int32)
    for bit in range(30, -1, -1):
        cand = prefix | (1 << bit)
        cnt = jnp.sum(jnp.where(bits >= cand, 1.0, 0.0), axis=1, keepdims=True)
        prefix = jnp.where(cnt >= cap, cand, prefix)
    gt = bits > prefix
    eq = bits == prefix
    blk = 2 * LANES
    tri = (lax.broadcasted_iota(jnp.int32, (blk, blk), 0) <= lax.broadcasted_iota(jnp.int32, (blk, blk), 1))
    tri = jnp.where(tri, 1.0, 0.0).astype(BF16)

    def cumsum(mask):
        mb = jnp.where(mask, 1.0, 0.0).astype(BF16)
        carry = jnp.zeros((E, 1), F32)
        outs = []
        for i in range(S // blk):
            cs = _dot(mb[:, i * blk:(i + 1) * blk], tri) + carry
            outs.append(cs)
            carry = cs[:, blk - 1:blk]
        return jnp.concatenate(outs, axis=1), carry

    cs_gt, n_gt = cumsum(gt)
    cs_eq, _ = cumsum(eq)
    need = cap - n_gt
    sel = gt | (eq & (cs_eq <= need))
    slot = cs_gt + jnp.minimum(cs_eq, need) - 1.0
    pos_ref[0] = jnp.where(sel, slot, -1.0).astype(jnp.int32)


def _route(aff, cap):
    B, E, S = aff.shape
    return pl.pallas_call(
        functools.partial(_route_kernel, cap=cap),
        grid=(B,),
        in_specs=[pl.BlockSpec((1, E, S), lambda b: (b, 0, 0))],
        out_specs=pl.BlockSpec((1, E, S), lambda b: (b, 0, 0)),
        out_shape=jax.ShapeDtypeStruct((B, E, S), jnp.int32),
        compiler_params=_cparams(("parallel",)),
        name="route",
    )(aff)


def _moe_kernel(pos_ref, aff_ref, h2_ref, wg_ref, wu_ref, wd_ref, o_ref, p_scr, g_scr, xe_scr, y_scr, *, cap):
    e = pl.program_id(1)
    f = pl.program_id(2)
    nf = pl.num_programs(2)
    S = h2_ref.shape[1]

    @pl.when((e == 0) & (f == 0))
    def _():
        o_ref[...] = jnp.zeros_like(o_ref)

    @pl.when(f == 0)
    def _():
        slot = lax.broadcasted_iota(jnp.int32, (cap, S), 0)
        hit = pos_ref[0, 0] == slot
        onehot = jnp.where(hit, 1.0, 0.0).astype(BF16)
        p_scr[...] = onehot
        g_scr[...] = jnp.sum(jnp.where(hit, aff_ref[0, 0], 0.0), axis=1, keepdims=True)
        xe_scr[...] = _dot(onehot, h2_ref[0]).astype(BF16)
        y_scr[...] = jnp.zeros_like(y_scr)

    xe = xe_scr[...]
    a = _dot(xe, wg_ref[0])
    u = _dot(xe, wu_ref[0])
    act = (a * jax.nn.sigmoid(a) * u).astype(BF16)
    y_scr[...] += _dot(act, wd_ref[0])

    @pl.when(f == nf - 1)
    def _():
        ye = (y_scr[...] * g_scr[...]).astype(BF16)
        o_ref[0] += _dot_tn(p_scr[...], ye)


def _moe(pos, aff, h2, wg, wu, wd, cap, tf):
    B, S, D = h2.shape
    E, _, F = wg.shape
    pos4 = pos.reshape(B, E, 1, S)
    aff4 = aff.reshape(B, E, 1, S)
    row_spec = pl.BlockSpec((1, 1, 1, S), lambda b, e, f: (b, e, 0, 0))
    return pl.pallas_call(
        functools.partial(_moe_kernel, cap=cap),
        grid=(B, E, F // tf),
        in_specs=[row_spec, row_spec,
                  pl.BlockSpec((1, S, D), lambda b, e, f: (b, 0, 0)),
                  pl.BlockSpec((1, D, tf), lambda b, e, f: (e, 0, f)),
                  pl.BlockSpec((1, D, tf), lambda b, e, f: (e, 0, f)),
                  pl.BlockSpec((1, tf, D), lambda b, e, f: (e, f, 0))],
        out_specs=pl.BlockSpec((1, S, D), lambda b, e, f: (b, 0, 0)),
        out_shape=jax.ShapeDtypeStruct((B, S, D), F32),
        scratch_shapes=[pltpu.VMEM((cap, S), BF16), pltpu.VMEM((cap, 1), F32),
                        pltpu.VMEM((cap, D), BF16), pltpu.VMEM((cap, D), F32)],
        compiler_params=_cparams(("parallel", "arbitrary", "arbitrary")),
        name="moe",
    )(pos4, aff4, h2, wg, wu, wd)


def _residual_kernel(x_ref, d_ref, o_ref):
    o_ref[0] = x_ref[0] + d_ref[0]


def _residual_norm_kernel(x_ref, d_ref, g_ref, o_ref):
    o_ref[0] = _rms(x_ref[0] + d_ref[0], g_ref[...])


def _residual(x1, delta, g, tm):
    B, S, D = x1.shape
    tok = pl.BlockSpec((1, tm, D), lambda b, i: (b, i, 0))
    if g is None:
        body, extra, extra_specs = _residual_kernel, (), []
    else:
        body, extra, extra_specs = _residual_norm_kernel, (g,), [_resident((1, D), lambda b, i: (0, 0))]
    return pl.pallas_call(
        body,
        grid=(B, S // tm),
        in_specs=[tok, tok] + extra_specs,
        out_specs=tok,
        out_shape=jax.ShapeDtypeStruct((B, S, D), F32),
        compiler_params=_cparams(("parallel", "parallel")),
        name="residual",
    )(x1, delta, *extra)


def _tile(n, pref):
    t = min(n, pref)
    assert n % t == 0, (n, t)
    return t


def _lambda_init(layer):
    return 0.8 - 0.6 * float(np.exp(-0.3 * layer))


def _dup_heads(w, n_heads, width):
    d = w.shape[0]
    w = w.reshape(d, n_heads, 1, width)
    return jnp.broadcast_to(w, (d, n_heads, 2, width)).reshape(d, n_heads * 2 * width)


def kernel(x, mem, attn_norm_g, mem_norm_g, w_in, w_mem_kv, da_lambda_q1, da_lambda_k1, da_lambda_q2,
           da_lambda_k2, da_subln_g, wa_sink, w_da_o, w_wa_o, w_xa_o, w_out, ffn_norm_g, w_router,
           w_exp_gate, w_exp_up, w_exp_down, final_norm_g):
    B, S, D = x.shape
    depth = w_in.shape[0]
    cap = max(1, EC_FACTOR * S // N_EXPERTS)
    row = lambda v: v.reshape(1, -1).astype(F32)
    da_w = DA_HEADS * 2 * DA_HEAD_DIM
    wa_kv_w = WA_KV_HEADS * WA_HEAD_DIM
    o_waq = 3 * da_w
    o_wak = o_waq + WA_HEADS * WA_HEAD_DIM
    o_wav = o_wak + wa_kv_w
    o_xaq = o_wav + wa_kv_w
    o_gate = o_xaq + XA_HEADS * XA_HEAD_DIM
    for l in range(depth):
        lam_init = _lambda_init(l)
        wl = w_in[l]
        w1 = jnp.concatenate([wl[:, :o_wak],
                              _dup_heads(wl[:, o_wak:o_wav], WA_KV_HEADS, WA_HEAD_DIM),
                              _dup_heads(wl[:, o_wav:o_xaq], WA_KV_HEADS, WA_HEAD_DIM),
                              wl[:, o_xaq:o_gate]], axis=1).astype(BF16)
        wg = wl[:, o_gate:].astype(BF16)
        wr = jnp.pad(w_router[l], ((0, 0), (0, LANES - N_EXPERTS))).astype(BF16)

        daq, dak, dav, waq, wak, wav, xaq = _in_proj(x, row(attn_norm_g[l]), w1, _tile(S, 512))
        memkv = _mem_proj(mem, row(mem_norm_g[l]), w_mem_kv[l].astype(BF16))
        lams = [row(v[l]) for v in (da_lambda_q1, da_lambda_k1, da_lambda_q2, da_lambda_k2)]
        oda = _diff_attn(daq, dak, dav, lams, da_subln_g[l].astype(F32), lam_init, _tile(S, 256))
        owa = _win_attn(waq, wak, wav, wa_sink[l].astype(F32), _tile(S, 256))
        oxa = _mem_attn(xaq, memkv, _tile(S, 512))
        x1, h2, aff = _merge(x, oda, owa, oxa, row(attn_norm_g[l]), wg, w_da_o[l].astype(BF16),
                             w_wa_o[l].astype(BF16), w_xa_o[l].astype(BF16), w_out[l].astype(BF16),
                             row(ffn_norm_g[l]), wr, _tile(S, 512))
        pos = _route(aff, cap)
        delta = _moe(pos, aff, h2, w_exp_gate[l].astype(BF16), w_exp_up[l].astype(BF16),
                     w_exp_down[l].astype(BF16), cap, _tile(w_exp_gate.shape[-1], 1024))
        last = l == depth - 1
        x = _residual(x1, delta, row(final_norm_g) if last else None, _tile(S, 512))
    return x
```

```python
import functools

import numpy as np
import jax
import jax.numpy as jnp
from jax import lax
from jax.experimental import pallas as pl
from jax.experimental.pallas import tpu as pltpu

EPS = 1e-6
BLOCK = 128
WINDOW = 128
DA_HEADS = 8
DA_HEAD_DIM = 64
DA_V_DIM = 2 * DA_HEAD_DIM
WA_HEADS = 16
WA_KV_HEADS = 4
WA_GROUPS = WA_HEADS // WA_KV_HEADS
WA_HEAD_DIM = 64
XA_HEADS = 4
XA_HEAD_DIM = 256
N_BRANCH = 3
N_EXPERTS = 16
EC_FACTOR = 2

LOG2E = 1.4426950408889634
DA_Q_SCALE = DA_HEAD_DIM ** -0.5 * LOG2E
DA_KEY_CHUNK = 256

LANES = 128
VMEM_LIMIT = 56 * 1024 * 1024

F32 = jnp.float32
BF16 = jnp.bfloat16
NEG_INF = float("-inf")


def _cparams(sem):
    return pltpu.CompilerParams(dimension_semantics=sem, vmem_limit_bytes=VMEM_LIMIT)


def _rms(xf, g_row):
    ms = jnp.mean(xf * xf, axis=-1, keepdims=True)
    return xf * lax.rsqrt(ms + EPS) * g_row


def _dot(a, b):
    return jnp.dot(a, b, preferred_element_type=F32)


def _dot_nt(a, b):
    return lax.dot_general(a, b, (((1,), (1,)), ((), ())), preferred_element_type=F32)


def _dot_tn(a, b):
    return lax.dot_general(a, b, (((0,), (0,)), ((), ())), preferred_element_type=F32)


def _resident(shape, index_map):
    return pl.BlockSpec(shape, index_map, pipeline_mode=pl.Buffered(1))


def _in_proj_kernel(x_ref, g_ref, w_ref, daq_ref, dak_ref, dav_ref, waq_ref, wak_ref, wav_ref, xaq_ref):
    d = x_ref.shape[-1]
    h = _rms(x_ref[0], g_ref[...]).astype(BF16)

    def sec(i):
        return _dot(h, w_ref[:, i * d:(i + 1) * d])

    for ref, i, scale in ((daq_ref, 0, DA_Q_SCALE), (dak_ref, 1, None), (dav_ref, 2, None)):
        r = sec(i)
        r = (r if scale is None else r * scale).astype(BF16)
        for hh in range(DA_HEADS):
            ref[0, hh] = r[:, hh * LANES:(hh + 1) * LANES]
    waq_ref[0] = sec(3).astype(BF16)
    r = sec(4).astype(BF16)
    half = d // 2
    wak_ref[0] = r[:, :half]
    wav_ref[0] = r[:, half:]
    xaq_ref[0] = sec(5).astype(BF16)


def _in_proj(x, g, w1, tm):
    B, S, D = x.shape
    n_t = S // tm
    head_shape = jax.ShapeDtypeStruct((B, DA_HEADS, S, LANES), BF16)
    head_spec = pl.BlockSpec((1, DA_HEADS, tm, LANES), lambda b, i: (b, 0, i, 0))
    tok = lambda w: pl.BlockSpec((1, tm, w), lambda b, i: (b, i, 0))
    return pl.pallas_call(
        _in_proj_kernel,
        grid=(B, n_t),
        in_specs=[tok(D), _resident((1, D), lambda b, i: (0, 0)), _resident(w1.shape, lambda b, i: (0, 0))],
        out_specs=[head_spec, head_spec, head_spec, tok(D), tok(D // 2), tok(D // 2), tok(D)],
        out_shape=[head_shape, head_shape, head_shape,
                   jax.ShapeDtypeStruct((B, S, D), BF16),
                   jax.ShapeDtypeStruct((B, S, D // 2), BF16),
                   jax.ShapeDtypeStruct((B, S, D // 2), BF16),
                   jax.ShapeDtypeStruct((B, S, D), BF16)],
        compiler_params=_cparams(("parallel", "parallel")),
        name="in_proj",
    )(x, g, w1)


def _mem_proj_kernel(m_ref, g_ref, w_ref, o_ref):
    h = _rms(m_ref[0], g_ref[...]).astype(BF16)
    o_ref[0] = _dot(h, w_ref[...]).astype(BF16)


def _mem_proj(mem, g, w):
    B, M, D = mem.shape
    N = w.shape[1]
    return pl.pallas_call(
        _mem_proj_kernel,
        grid=(B,),
        in_specs=[pl.BlockSpec((1, M, D), lambda b: (b, 0, 0)),
                  _resident((1, D), lambda b: (0, 0)),
                  _resident((D, N), lambda b: (0, 0))],
        out_specs=pl.BlockSpec((1, M, N), lambda b: (b, 0, 0)),
        out_shape=jax.ShapeDtypeStruct((B, M, N), BF16),
        compiler_params=_cparams(("parallel",)),
        name="mem_proj",
    )(mem, g, w)


def _diff_attn_kernel(lq1_ref, lk1_ref, lq2_ref, lk2_ref, q_ref, k_ref, v_ref, bias_ref, g_ref, o_ref,
                      k1_scr, k2_scr, vt_scr, sa_scr, sb_scr, ma_scr, mb_scr, *, lam_init, n_t):
    S = k_ref.shape[2]
    tq = q_ref.shape[2]
    j = pl.program_id(1)
    n_tiles = pl.num_programs(1) - 1
    ck = DA_KEY_CHUNK
    head, tile = _da_head_tile(j, n_t, n_tiles)
    prev_head, _ = _da_head_tile(j - 1, n_t, n_tiles)

    @pl.when((tile == 0) & (j < n_tiles))
    def _():
        kk = k_ref[0, 0]
        lane = lax.broadcasted_iota(jnp.int32, kk.shape, 1)
        zero = jnp.zeros_like(kk)
        k1_scr[...] = jnp.where(lane < DA_HEAD_DIM, kk, zero)
        k2_scr[...] = jnp.where(lane >= DA_HEAD_DIM, kk, zero)
        vt_scr[head % 2] = v_ref[0, 0].astype(F32).T.astype(BF16)

    @pl.when(j == 0)
    def _():
        sb_scr[...] = jnp.zeros_like(sb_scr)
        mb_scr[...] = jnp.zeros_like(mb_scr)

    lam = (jnp.exp(jnp.sum(lq1_ref[...] * lk1_ref[...], keepdims=True))
           - jnp.exp(jnp.sum(lq2_ref[...] * lk2_ref[...], keepdims=True)) + lam_init)

    q = q_ref[0, 0]
    r0 = (S - tq) - tile * tq
    vt_slot = prev_head % 2

    def step(s_w, m_w, s_r, m_r):
        m1 = m2 = None
        for c in range(S // ck):
            rows = slice(c * ck, (c + 1) * ck)
            bias = bias_ref[0, pl.ds(pl.multiple_of(r0 + c * ck, ck), ck), :]
            s1 = _dot_nt(k1_scr[rows, :], q) + bias
            s2 = _dot_nt(k2_scr[rows, :], q) + bias
            s_w[0, rows, :] = s1
            s_w[1, rows, :] = s2
            c1 = jnp.max(s1, axis=0, keepdims=True)
            c2 = jnp.max(s2, axis=0, keepdims=True)
            m1 = c1 if m1 is None else jnp.maximum(m1, c1)
            m2 = c2 if m2 is None else jnp.maximum(m2, c2)
        m_w[0:1, :] = m1
        m_w[1:2, :] = m2

        def finish(i):
            m = m_r[i:i + 1, :]
            l = acc = None
            for c in range(S // ck):
                rows = slice(c * ck, (c + 1) * ck)
                p = jnp.exp2(s_r[i, rows, :] - m)
                lc = jnp.sum(p, axis=0, keepdims=True)
                d = _dot(vt_scr[vt_slot, :, rows], p.astype(BF16))
                l = lc if l is None else l + lc
                acc = d if acc is None else acc + d
            return acc, l

        o1, l1 = finish(0)
        o2, l2 = finish(1)
        ot = o1 * (1.0 / l1) - o2 * (lam / l2)
        ms = jnp.mean(ot * ot, axis=0, keepdims=True)
        y = ot * lax.rsqrt(ms + EPS) * g_ref[...] * (1.0 - lam_init)
        o_ref[0] = y.T.astype(BF16)

    @pl.when(j % 2 == 0)
    def _():
        step(sa_scr, ma_scr, sb_scr, mb_scr)

    @pl.when(j % 2 == 1)
    def _():
        step(sb_scr, mb_scr, sa_scr, ma_scr)


def _da_head_tile(j, n_t, n_tiles):
    jc = jnp.clip(j, 0, n_tiles - 1)
    return jc // n_t, jc % n_t


def _diff_attn(daq, dak, dav, lams, subln_g, lam_init, tq):
    B, H, S, _ = daq.shape
    n_t = S // tq
    rows = 2 * S - tq
    slopes = jnp.exp2(-8.0 * jnp.arange(1, H + 1, dtype=F32) / H)
    r = jnp.arange(rows, dtype=jnp.int32)[:, None]
    c = jnp.arange(tq, dtype=jnp.int32)[None, :]
    dist = jnp.abs(r - (S - tq) - c).astype(F32)
    bias = -((slopes * LOG2E)[:, None, None] * dist[None])
    n_tiles = H * n_t
    lam_spec = _resident((1, DA_HEAD_DIM), lambda b, j: (0, 0))

    def kv_map(b, j):
        return (b, _da_head_tile(j, n_t, n_tiles)[0], 0, 0)

    def q_map(b, j):
        head, tile = _da_head_tile(j, n_t, n_tiles)
        return (b, head, tile, 0)

    def o_map(b, j):
        head, tile = _da_head_tile(j - 1, n_t, n_tiles)
        return (b, tile, head)

    kv_spec = pl.BlockSpec((1, 1, S, LANES), kv_map)
    return pl.pallas_call(
        functools.partial(_diff_attn_kernel, lam_init=lam_init, n_t=n_t),
        grid=(B, n_tiles + 1),
        in_specs=[lam_spec, lam_spec, lam_spec, lam_spec,
                  pl.BlockSpec((1, 1, tq, LANES), q_map),
                  kv_spec, kv_spec,
                  pl.BlockSpec((1, rows, tq), lambda b, j: (_da_head_tile(j, n_t, n_tiles)[0], 0, 0)),
                  _resident((DA_V_DIM, 1), lambda b, j: (0, 0))],
        out_specs=pl.BlockSpec((1, tq, LANES), o_map),
        out_shape=jax.ShapeDtypeStruct((B, S, H * LANES), BF16),
        scratch_shapes=[pltpu.VMEM((S, LANES), BF16), pltpu.VMEM((S, LANES), BF16),
                        pltpu.VMEM((2, DA_V_DIM, S), BF16),
                        pltpu.VMEM((2, S, tq), F32), pltpu.VMEM((2, S, tq), F32),
                        pltpu.VMEM((2, tq), F32), pltpu.VMEM((2, tq), F32)],
        compiler_params=_cparams(("parallel", "arbitrary")),
        name="diff_attn",
    )(*lams, daq, dak, dav, bias, subln_g.reshape(DA_V_DIM, 1))


def _win_attn_kernel(sink_ref, q_ref, k_ref, v_ref, tab_ref, o_ref, klo, khi, vlo, vhi):
    S = k_ref.shape[1]
    tq = q_ref.shape[1]
    span = tq + 2 * WINDOW
    g = pl.program_id(1)
    t = pl.program_id(2)

    @pl.when(t == 0)
    def _():
        kk = k_ref[0]
        vv = v_ref[0]
        lane = lax.broadcasted_iota(jnp.int32, kk.shape, 1)
        lo = lane < WA_HEAD_DIM
        zero = jnp.zeros_like(kk)
        pad = jnp.zeros((WINDOW, LANES), BF16)
        for scr, src, keep in ((klo, kk, lo), (khi, kk, ~lo), (vlo, vv, lo), (vhi, vv, ~lo)):
            scr[0:WINDOW, :] = pad
            scr[WINDOW + S:WINDOW + S + WINDOW, :] = pad
            scr[WINDOW:WINDOW + S, :] = jnp.where(keep, src, zero)

    start = pl.multiple_of(t * tq, tq)
    kpos = start - WINDOW + lax.broadcasted_iota(jnp.int32, (1, span), 1)
    edge = jnp.where((kpos >= 0) & (kpos < S), 0.0, NEG_INF).astype(F32)
    kz = (klo[pl.ds(start, span), :], khi[pl.ds(start, span), :])
    vz = (vlo[pl.ds(start, span), :], vhi[pl.ds(start, span), :])
    for pair in range(WA_GROUPS // 2):
        qp = (q_ref[0, :, pair * LANES:(pair + 1) * LANES].astype(F32) * (WA_HEAD_DIM ** -0.5)).astype(BF16)
        acc = None
        for half in range(2):
            j = pair * 2 + half
            s = _dot_nt(qp, kz[half]) + tab_ref[j] + edge
            sink = sink_ref[g * WA_GROUPS + j]
            m = jnp.maximum(jnp.max(s, axis=1, keepdims=True), sink)
            p = jnp.exp(s - m)
            l = jnp.sum(p, axis=1, keepdims=True) + jnp.exp(sink - m)
            pn = (p * (1.0 / l)).astype(BF16)
            o = _dot(pn, vz[half])
            acc = o if acc is None else acc + o
        o_ref[0, :, pair * LANES:(pair + 1) * LANES] = acc.astype(BF16)


def _win_attn(waq, wak, wav, sink, tq):
    B, S, D = waq.shape
    n_t = S // tq
    span = tq + 2 * WINDOW
    gw = WA_GROUPS * WA_HEAD_DIM
    slopes = jnp.exp2(-8.0 * jnp.arange(1, WA_HEADS + 1, dtype=F32) / WA_HEADS)
    ir = jnp.arange(tq, dtype=jnp.int32)[:, None]
    jr = jnp.arange(span, dtype=jnp.int32)[None, :]
    dist = jnp.abs(ir + WINDOW - jr)
    tab = jnp.where((dist <= WINDOW)[None], -(slopes[:, None, None] * dist.astype(F32)[None]), NEG_INF)
    pad_shape = pltpu.VMEM((S + 2 * WINDOW, LANES), BF16)
    return pl.pallas_call(
        _win_attn_kernel,
        grid=(B, WA_KV_HEADS, n_t),
        in_specs=[pl.BlockSpec(memory_space=pltpu.SMEM),
                  pl.BlockSpec((1, tq, gw), lambda b, g, t: (b, t, g)),
                  pl.BlockSpec((1, S, LANES), lambda b, g, t: (b, 0, g)),
                  pl.BlockSpec((1, S, LANES), lambda b, g, t: (b, 0, g)),
                  pl.BlockSpec((WA_GROUPS, tq, span), lambda b, g, t: (g, 0, 0))],
        out_specs=pl.BlockSpec((1, tq, gw), lambda b, g, t: (b, t, g)),
        out_shape=jax.ShapeDtypeStruct((B, S, D), BF16),
        scratch_shapes=[pad_shape, pad_shape, pad_shape, pad_shape],
        compiler_params=_cparams(("parallel", "parallel", "arbitrary")),
        name="win_attn",
    )(sink, waq, wak, wav, tab)


def _mem_attn_kernel(q_ref, k_ref, v_ref, o_ref):
    q = (q_ref[0].astype(F32) * (XA_HEAD_DIM ** -0.5)).astype(BF16)
    s = _dot_nt(q, k_ref[0])
    m = jnp.max(s, axis=1, keepdims=True)
    p = jnp.exp(s - m)
    l = jnp.sum(p, axis=1, keepdims=True)
    o_ref[0] = _dot((p * (1.0 / l)).astype(BF16), v_ref[0]).astype(BF16)


def _mem_attn(xaq, memkv, tq):
    B, S, D = xaq.shape
    M = memkv.shape[1]
    hd = XA_HEAD_DIM
    return pl.pallas_call(
        _mem_attn_kernel,
        grid=(B, XA_HEADS, S // tq),
        in_specs=[pl.BlockSpec((1, tq, hd), lambda b, h, t: (b, t, h)),
                  pl.BlockSpec((1, M, hd), lambda b, h, t: (b, 0, h)),
                  pl.BlockSpec((1, M, hd), lambda b, h, t: (b, 0, XA_HEADS + h))],
        out_specs=pl.BlockSpec((1, tq, hd), lambda b, h, t: (b, t, h)),
        out_shape=jax.ShapeDtypeStruct((B, S, D), BF16),
        compiler_params=_cparams(("parallel", "parallel", "parallel")),
        name="mem_attn",
    )(xaq, memkv, memkv)


def _merge_kernel(x_ref, oda_ref, owa_ref, oxa_ref, g1_ref, wg_ref, wda_ref, wwa_ref, wxa_ref, wout_ref,
                  g2_ref, wr_ref, x1_ref, h2_ref, aff_ref):
    d = x_ref.shape[-1]
    x = x_ref[0]
    h = _rms(x, g1_ref[...]).astype(BF16)
    merged = None
    for i, (o_ref, w_ref) in enumerate(((oda_ref, wda_ref), (owa_ref, wwa_ref), (oxa_ref, wxa_ref))):
        gate = jax.nn.sigmoid(_dot(h, wg_ref[:, i * d:(i + 1) * d]))
        term = gate * _dot(o_ref[0], w_ref[...])
        merged = term if merged is None else merged + term
    x1 = x + _dot(merged.astype(BF16), wout_ref[...])
    x1_ref[0] = x1
    h2 = _rms(x1, g2_ref[...]).astype(BF16)
    h2_ref[0] = h2
    logits = _dot(h2, wr_ref[...])
    lt = logits.T[:N_EXPERTS, :]
    m = jnp.max(lt, axis=0, keepdims=True)
    e = jnp.exp(lt - m)
    aff_ref[0] = e / jnp.sum(e, axis=0, keepdims=True)


def _merge(x, oda, owa, oxa, g1, wg, wda, wwa, wxa, wout, g2, wr, tm):
    B, S, D = x.shape
    tok = pl.BlockSpec((1, tm, D), lambda b, i: (b, i, 0))
    full = lambda a: _resident(a.shape, lambda b, i: (0,) * a.ndim)
    return pl.pallas_call(
        _merge_kernel,
        grid=(B, S // tm),
        in_specs=[tok, tok, tok, tok, full(g1), full(wg), full(wda), full(wwa), full(wxa), full(wout),
                  full(g2), full(wr)],
        out_specs=[tok, tok, pl.BlockSpec((1, N_EXPERTS, tm), lambda b, i: (b, 0, i))],
        out_shape=[jax.ShapeDtypeStruct((B, S, D), F32),
                   jax.ShapeDtypeStruct((B, S, D), BF16),
                   jax.ShapeDtypeStruct((B, N_EXPERTS, S), F32)],
        compiler_params=_cparams(("parallel", "parallel")),
        name="merge",
    )(x, oda, owa, oxa, g1, wg, wda, wwa, wxa, wout, g2, wr)


def _route_kernel(aff_ref, pos_ref, *, cap):
    aff = aff_ref[0]
    E, S = aff.shape
    bits = pltpu.bitcast(aff, jnp.int32)
    prefix = jnp.zeros((E, 1), jnp.int32)
    for bit in range(30, -1, -1):
        cand = prefix | (1 << bit)
        cnt = jnp.sum(jnp.where(bits >= cand, 1.0, 0.0), axis=1, keepdims=True)
        prefix = jnp.where(cnt >= cap, cand, prefix)
    gt = bits > prefix
    eq = bits == prefix
    blk = 2 * LANES
    tri = (lax.broadcasted_iota(jnp.int32, (blk, blk), 0) <= lax.broadcasted_iota(jnp.int32, (blk, blk), 1))
    tri = jnp.where(tri, 1.0, 0.0).astype(BF16)

    def cumsum(mask):
        mb = jnp.where(mask, 1.0, 0.0).astype(BF16)
        carry = jnp.zeros((E, 1), F32)
        outs = []
        for i in range(S // blk):
            cs = _dot(mb[:, i * blk:(i + 1) * blk], tri) + carry
            outs.append(cs)
            carry = cs[:, blk - 1:blk]
        return jnp.concatenate(outs, axis=1), carry

    cs_gt, n_gt = cumsum(gt)
    cs_eq, _ = cumsum(eq)
    need = cap - n_gt
    sel = gt | (eq & (cs_eq <= need))
    slot = cs_gt + jnp.minimum(cs_eq, need) - 1.0
    pos_ref[0] = jnp.where(sel, slot, -1.0).astype(jnp.int32)


def _route(aff, cap):
    B, E, S = aff.shape
    return pl.pallas_call(
        functools.partial(_route_kernel, cap=cap),
        grid=(B,),
        in_specs=[pl.BlockSpec((1, E, S), lambda b: (b, 0, 0))],
        out_specs=pl.BlockSpec((1, E, S), lambda b: (b, 0, 0)),
        out_shape=jax.ShapeDtypeStruct((B, E, S), jnp.int32),
        compiler_params=_cparams(("parallel",)),
        name="route",
    )(aff)


def _moe_kernel(pos_ref, aff_ref, h2_ref, wg_ref, wu_ref, wd_ref, o_ref, p_scr, g_scr, xe_scr, y_scr, *, cap):
    e = pl.program_id(1)
    f = pl.program_id(2)
    nf = pl.num_programs(2)
    S = h2_ref.shape[1]

    @pl.when((e == 0) & (f == 0))
    def _():
        o_ref[...] = jnp.zeros_like(o_ref)

    @pl.when(f == 0)
    def _():
        slot = lax.broadcasted_iota(jnp.int32, (cap, S), 0)
        hit = pos_ref[0, 0] == slot
        onehot = jnp.where(hit, 1.0, 0.0).astype(BF16)
        p_scr[...] = onehot
        g_scr[...] = jnp.sum(jnp.where(hit, aff_ref[0, 0], 0.0), axis=1, keepdims=True)
        xe_scr[...] = _dot(onehot, h2_ref[0]).astype(BF16)
        y_scr[...] = jnp.zeros_like(y_scr)

    xe = xe_scr[...]
    a = _dot(xe, wg_ref[0])
    u = _dot(xe, wu_ref[0])
    act = (a * jax.nn.sigmoid(a) * u).astype(BF16)
    y_scr[...] += _dot(act, wd_ref[0])

    @pl.when(f == nf - 1)
    def _():
        ye = (y_scr[...] * g_scr[...]).astype(BF16)
        o_ref[0] += _dot_tn(p_scr[...], ye)


def _moe(pos, aff, h2, wg, wu, wd, cap, tf):
    B, S, D = h2.shape
    E, _, F = wg.shape
    pos4 = pos.reshape(B, E, 1, S)
    aff4 = aff.reshape(B, E, 1, S)
    row_spec = pl.BlockSpec((1, 1, 1, S), lambda b, e, f: (b, e, 0, 0))
    return pl.pallas_call(
        functools.partial(_moe_kernel, cap=cap),
        grid=(B, E, F // tf),
        in_specs=[row_spec, row_spec,
                  pl.BlockSpec((1, S, D), lambda b, e, f: (b, 0, 0)),
                  pl.BlockSpec((1, D, tf), lambda b, e, f: (e, 0, f)),
                  pl.BlockSpec((1, D, tf), lambda b, e, f: (e, 0, f)),
                  pl.BlockSpec((1, tf, D), lambda b, e, f: (e, f, 0))],
        out_specs=pl.BlockSpec((1, S, D), lambda b, e, f: (b, 0, 0)),
        out_shape=jax.ShapeDtypeStruct((B, S, D), F32),
        scratch_shapes=[pltpu.VMEM((cap, S), BF16), pltpu.VMEM((cap, 1), F32),
                        pltpu.VMEM((cap, D), BF16), pltpu.VMEM((cap, D), F32)],
        compiler_params=_cparams(("parallel", "arbitrary", "arbitrary")),
        name="moe",
    )(pos4, aff4, h2, wg, wu, wd)


def _residual_kernel(x_ref, d_ref, o_ref):
    o_ref[0] = x_ref[0] + d_ref[0]


def _residual_norm_kernel(x_ref, d_ref, g_ref, o_ref):
    o_ref[0] = _rms(x_ref[0] + d_ref[0], g_ref[...])


def _residual(x1, delta, g, tm):
    B, S, D = x1.shape
    tok = pl.BlockSpec((1, tm, D), lambda b, i: (b, i, 0))
    if g is None:
        body, extra, extra_specs = _residual_kernel, (), []
    else:
        body, extra, extra_specs = _residual_norm_kernel, (g,), [_resident((1, D), lambda b, i: (0, 0))]
    return pl.pallas_call(
        body,
        grid=(B, S // tm),
        in_specs=[tok, tok] + extra_specs,
        out_specs=tok,
        out_shape=jax.ShapeDtypeStruct((B, S, D), F32),
        compiler_params=_cparams(("parallel", "parallel")),
        name="residual",
    )(x1, delta, *extra)


def _tile(n, pref):
    t = min(n, pref)
    assert n % t == 0, (n, t)
    return t


def _lambda_init(layer):
    return 0.8 - 0.6 * float(np.exp(-0.3 * layer))


def _dup_heads(w, n_heads, width):
    d = w.shape[0]
    w = w.reshape(d, n_heads, 1, width)
    return jnp.broadcast_to(w, (d, n_heads, 2, width)).reshape(d, n_heads * 2 * width)


def kernel(x, mem, attn_norm_g, mem_norm_g, w_in, w_mem_kv, da_lambda_q1, da_lambda_k1, da_lambda_q2,
           da_lambda_k2, da_subln_g, wa_sink, w_da_o, w_wa_o, w_xa_o, w_out, ffn_norm_g, w_router,
           w_exp_gate, w_exp_up, w_exp_down, final_norm_g):
    B, S, D = x.shape
    depth = w_in.shape[0]
    cap = max(1, EC_FACTOR * S // N_EXPERTS)
    row = lambda v: v.reshape(1, -1).astype(F32)
    da_w = DA_HEADS * 2 * DA_HEAD_DIM
    wa_kv_w = WA_KV_HEADS * WA_HEAD_DIM
    o_waq = 3 * da_w
    o_wak = o_waq + WA_HEADS * WA_HEAD_DIM
    o_wav = o_wak + wa_kv_w
    o_xaq = o_wav + wa_kv_w
    o_gate = o_xaq + XA_HEADS * XA_HEAD_DIM
    for l in range(depth):
        lam_init = _lambda_init(l)
        wl = w_in[l]
        w1 = jnp.concatenate([wl[:, :o_wak],
                              _dup_heads(wl[:, o_wak:o_wav], WA_KV_HEADS, WA_HEAD_DIM),
                              _dup_heads(wl[:, o_wav:o_xaq], WA_KV_HEADS, WA_HEAD_DIM),
                              wl[:, o_xaq:o_gate]], axis=1).astype(BF16)
        wg = wl[:, o_gate:].astype(BF16)
        wr = jnp.pad(w_router[l], ((0, 0), (0, LANES - N_EXPERTS))).astype(BF16)

        daq, dak, dav, waq, wak, wav, xaq = _in_proj(x, row(attn_norm_g[l]), w1, _tile(S, 512))
        memkv = _mem_proj(mem, row(mem_norm_g[l]), w_mem_kv[l].astype(BF16))
        lams = [row(v[l]) for v in (da_lambda_q1, da_lambda_k1, da_lambda_q2, da_lambda_k2)]
        oda = _diff_attn(daq, dak, dav, lams, da_subln_g[l].astype(F32), lam_init, _tile(S, 512))
        owa = _win_attn(waq, wak, wav, wa_sink[l].astype(F32), _tile(S, 256))
        oxa = _mem_attn(xaq, memkv, _tile(S, 512))
        x1, h2, aff = _merge(x, oda, owa, oxa, row(attn_norm_g[l]), wg, w_da_o[l].astype(BF16),
                             w_wa_o[l].astype(BF16), w_xa_o[l].astype(BF16), w_out[l].astype(BF16),
                             row(ffn_norm_g[l]), wr, _tile(S, 512))
        pos = _route(aff, cap)
        delta = _moe(pos, aff, h2, w_exp_gate[l].astype(BF16), w_exp_up[l].astype(BF16),
                     w_exp_down[l].astype(BF16), cap, _tile(w_exp_gate.shape[-1], 1024))
        last = l == depth - 1
        x = _residual(x1, delta, row(final_norm_g) if last else None, _tile(S, 512))
    return x
```

```python
import functools

import numpy as np
import jax
import jax.numpy as jnp
from jax import lax
from jax.experimental import pallas as pl
from jax.experimental.pallas import tpu as pltpu

EPS = 1e-6
BLOCK = 128
WINDOW = 128
DA_HEADS = 8
DA_HEAD_DIM = 64
DA_V_DIM = 2 * DA_HEAD_DIM
WA_HEADS = 16
WA_KV_HEADS = 4
WA_GROUPS = WA_HEADS // WA_KV_HEADS
WA_HEAD_DIM = 64
XA_HEADS = 4
XA_HEAD_DIM = 256
N_BRANCH = 3
N_EXPERTS = 16
EC_FACTOR = 2

LOG2E = 1.4426950408889634
DA_Q_SCALE = DA_HEAD_DIM ** -0.5 * LOG2E
DA_KEY_CHUNK = 256
DA_ONES_ROWS = 16

LANES = 128
VMEM_LIMIT = 56 * 1024 * 1024

F32 = jnp.float32
BF16 = jnp.bfloat16
NEG_INF = float("-inf")


def _cparams(sem):
    return pltpu.CompilerParams(dimension_semantics=sem, vmem_limit_bytes=VMEM_LIMIT)


def _rms(xf, g_row):
    ms = jnp.mean(xf * xf, axis=-1, keepdims=True)
    return xf * lax.rsqrt(ms + EPS) * g_row


def _dot(a, b):
    return jnp.dot(a, b, preferred_element_type=F32)


def _dot_nt(a, b):
    return lax.dot_general(a, b, (((1,), (1,)), ((), ())), preferred_element_type=F32)


def _dot_tn(a, b):
    return lax.dot_general(a, b, (((0,), (0,)), ((), ())), preferred_element_type=F32)


def _resident(shape, index_map):
    return pl.BlockSpec(shape, index_map, pipeline_mode=pl.Buffered(1))


def _in_proj_kernel(x_ref, g_ref, w_ref, daq_ref, dak_ref, dav_ref, waq_ref, wak_ref, wav_ref, xaq_ref):
    d = x_ref.shape[-1]
    h = _rms(x_ref[0], g_ref[...]).astype(BF16)

    def sec(i):
        return _dot(h, w_ref[:, i * d:(i + 1) * d])

    for ref, i, scale in ((daq_ref, 0, DA_Q_SCALE), (dak_ref, 1, None), (dav_ref, 2, None)):
        r = sec(i)
        r = (r if scale is None else r * scale).astype(BF16)
        for hh in range(DA_HEADS):
            ref[0, hh] = r[:, hh * LANES:(hh + 1) * LANES]
    waq_ref[0] = sec(3).astype(BF16)
    r = sec(4).astype(BF16)
    half = d // 2
    wak_ref[0] = r[:, :half]
    wav_ref[0] = r[:, half:]
    xaq_ref[0] = sec(5).astype(BF16)


def _in_proj(x, g, w1, tm):
    B, S, D = x.shape
    n_t = S // tm
    head_shape = jax.ShapeDtypeStruct((B, DA_HEADS, S, LANES), BF16)
    head_spec = pl.BlockSpec((1, DA_HEADS, tm, LANES), lambda b, i: (b, 0, i, 0))
    tok = lambda w: pl.BlockSpec((1, tm, w), lambda b, i: (b, i, 0))
    return pl.pallas_call(
        _in_proj_kernel,
        grid=(B, n_t),
        in_specs=[tok(D), _resident((1, D), lambda b, i: (0, 0)), _resident(w1.shape, lambda b, i: (0, 0))],
        out_specs=[head_spec, head_spec, head_spec, tok(D), tok(D // 2), tok(D // 2), tok(D)],
        out_shape=[head_shape, head_shape, head_shape,
                   jax.ShapeDtypeStruct((B, S, D), BF16),
                   jax.ShapeDtypeStruct((B, S, D // 2), BF16),
                   jax.ShapeDtypeStruct((B, S, D // 2), BF16),
                   jax.ShapeDtypeStruct((B, S, D), BF16)],
        compiler_params=_cparams(("parallel", "parallel")),
        name="in_proj",
    )(x, g, w1)


def _mem_proj_kernel(m_ref, g_ref, w_ref, o_ref):
    h = _rms(m_ref[0], g_ref[...]).astype(BF16)
    o_ref[0] = _dot(h, w_ref[...]).astype(BF16)


def _mem_proj(mem, g, w):
    B, M, D = mem.shape
    N = w.shape[1]
    return pl.pallas_call(
        _mem_proj_kernel,
        grid=(B,),
        in_specs=[pl.BlockSpec((1, M, D), lambda b: (b, 0, 0)),
                  _resident((1, D), lambda b: (0, 0)),
                  _resident((D, N), lambda b: (0, 0))],
        out_specs=pl.BlockSpec((1, M, N), lambda b: (b, 0, 0)),
        out_shape=jax.ShapeDtypeStruct((B, M, N), BF16),
        compiler_params=_cparams(("parallel",)),
        name="mem_proj",
    )(mem, g, w)


def _diff_attn_kernel(lq1_ref, lk1_ref, lq2_ref, lk2_ref, q_ref, k_ref, v_ref, bias_ref, g_ref, o_ref,
                      k1_scr, k2_scr, vt_scr, sa_scr, sb_scr, ma_scr, mb_scr, *, lam_init, n_t):
    S = k_ref.shape[2]
    tq = q_ref.shape[2]
    j = pl.program_id(1)
    n_tiles = pl.num_programs(1) - 1
    ck = DA_KEY_CHUNK
    seq, tile = _da_seq_tile(j, n_t, n_tiles)
    prev_seq, _ = _da_seq_tile(j - 1, n_t, n_tiles)

    @pl.when((tile == 0) & (j < n_tiles))
    def _():
        kk = k_ref[0, 0]
        lane = lax.broadcasted_iota(jnp.int32, kk.shape, 1)
        zero = jnp.zeros_like(kk)
        k1_scr[...] = jnp.where(lane < DA_HEAD_DIM, kk, zero)
        k2_scr[...] = jnp.where(lane >= DA_HEAD_DIM, kk, zero)
        vt_scr[seq % 2, 0:DA_V_DIM, :] = v_ref[0, 0].astype(F32).T.astype(BF16)
        vt_scr[seq % 2, DA_V_DIM:, :] = jnp.ones((DA_ONES_ROWS, S), BF16)

    @pl.when(j == 0)
    def _():
        sb_scr[...] = jnp.zeros_like(sb_scr)
        mb_scr[...] = jnp.zeros_like(mb_scr)

    lam = (jnp.exp(jnp.sum(lq1_ref[...] * lk1_ref[...], keepdims=True))
           - jnp.exp(jnp.sum(lq2_ref[...] * lk2_ref[...], keepdims=True)) + lam_init)

    q = q_ref[0, 0]
    r0 = (S - tq) - tile * tq
    vt_slot = prev_seq % 2

    def step(s_w, m_w, s_r, m_r):
        m1 = m2 = None
        for c in range(S // ck):
            rows = slice(c * ck, (c + 1) * ck)
            bias = bias_ref[0, pl.ds(pl.multiple_of(r0 + c * ck, ck), ck), :]
            s1 = _dot_nt(k1_scr[rows, :], q) + bias
            s2 = _dot_nt(k2_scr[rows, :], q) + bias
            s_w[0, rows, :] = s1
            s_w[1, rows, :] = s2
            c1 = jnp.max(s1, axis=0, keepdims=True)
            c2 = jnp.max(s2, axis=0, keepdims=True)
            m1 = c1 if m1 is None else jnp.maximum(m1, c1)
            m2 = c2 if m2 is None else jnp.maximum(m2, c2)
        m_w[0:1, :] = m1
        m_w[1:2, :] = m2

        def finish(i):
            m = m_r[i:i + 1, :]
            acc = None
            for c in range(S // ck):
                rows = slice(c * ck, (c + 1) * ck)
                p = jnp.exp2(s_r[i, rows, :] - m).astype(BF16)
                d = _dot(vt_scr[vt_slot, :, rows], p)
                acc = d if acc is None else acc + d
            return acc[0:DA_V_DIM, :], acc[DA_V_DIM:DA_V_DIM + 1, :]

        o1, l1 = finish(0)
        o2, l2 = finish(1)
        ot = o1 * (1.0 / l1) - o2 * (lam / l2)
        ms = jnp.mean(ot * ot, axis=0, keepdims=True)
        y = ot * lax.rsqrt(ms + EPS) * g_ref[...] * (1.0 - lam_init)
        o_ref[0] = y.T.astype(BF16)

    @pl.when(j % 2 == 0)
    def _():
        step(sa_scr, ma_scr, sb_scr, mb_scr)

    @pl.when(j % 2 == 1)
    def _():
        step(sb_scr, mb_scr, sa_scr, ma_scr)


def _da_seq_tile(j, n_t, n_tiles):
    jc = jnp.clip(j, 0, n_tiles - 1)
    return jc // n_t, jc % n_t


def _diff_attn(daq, dak, dav, lams, subln_g, lam_init, tq):
    B, H, S, _ = daq.shape
    n_t = S // tq
    rows = 2 * S - tq
    slopes = jnp.exp2(-8.0 * jnp.arange(1, H + 1, dtype=F32) / H)
    r = jnp.arange(rows, dtype=jnp.int32)[:, None]
    c = jnp.arange(tq, dtype=jnp.int32)[None, :]
    dist = jnp.abs(r - (S - tq) - c).astype(F32)
    bias = -((slopes * LOG2E)[:, None, None] * dist[None])
    n_tiles = B * n_t
    lam_spec = _resident((1, DA_HEAD_DIM), lambda h, j: (0, 0))

    def kv_map(h, j):
        return (_da_seq_tile(j, n_t, n_tiles)[0], h, 0, 0)

    def q_map(h, j):
        seq, tile = _da_seq_tile(j, n_t, n_tiles)
        return (seq, h, tile, 0)

    def o_map(h, j):
        seq, tile = _da_seq_tile(j - 1, n_t, n_tiles)
        return (seq, tile, h)

    kv_spec = pl.BlockSpec((1, 1, S, LANES), kv_map)
    return pl.pallas_call(
        functools.partial(_diff_attn_kernel, lam_init=lam_init, n_t=n_t),
        grid=(H, n_tiles + 1),
        in_specs=[lam_spec, lam_spec, lam_spec, lam_spec,
                  pl.BlockSpec((1, 1, tq, LANES), q_map),
                  kv_spec, kv_spec,
                  pl.BlockSpec((1, rows, tq), lambda h, j: (h, 0, 0)),
                  _resident((DA_V_DIM, 1), lambda h, j: (0, 0))],
        out_specs=pl.BlockSpec((1, tq, LANES), o_map),
        out_shape=jax.ShapeDtypeStruct((B, S, H * LANES), BF16),
        scratch_shapes=[pltpu.VMEM((S, LANES), BF16), pltpu.VMEM((S, LANES), BF16),
                        pltpu.VMEM((2, DA_V_DIM + DA_ONES_ROWS, S), BF16),
                        pltpu.VMEM((2, S, tq), F32), pltpu.VMEM((2, S, tq), F32),
                        pltpu.VMEM((2, tq), F32), pltpu.VMEM((2, tq), F32)],
        compiler_params=_cparams(("parallel", "arbitrary")),
        name="diff_attn",
    )(*lams, daq, dak, dav, bias, subln_g.reshape(DA_V_DIM, 1))


def _win_attn_kernel(sink_ref, q_ref, k_ref, v_ref, tab_ref, o_ref, klo, khi, vlo, vhi):
    S = k_ref.shape[1]
    tq = q_ref.shape[1]
    span = tq + 2 * WINDOW
    g = pl.program_id(1)
    t = pl.program_id(2)

    @pl.when(t == 0)
    def _():
        kk = k_ref[0]
        vv = v_ref[0]
        lane = lax.broadcasted_iota(jnp.int32, kk.shape, 1)
        lo = lane < WA_HEAD_DIM
        zero = jnp.zeros_like(kk)
        pad = jnp.zeros((WINDOW, LANES), BF16)
        for scr, src, keep in ((klo, kk, lo), (khi, kk, ~lo), (vlo, vv, lo), (vhi, vv, ~lo)):
            scr[0:WINDOW, :] = pad
            scr[WINDOW + S:WINDOW + S + WINDOW, :] = pad
            scr[WINDOW:WINDOW + S, :] = jnp.where(keep, src, zero)

    start = pl.multiple_of(t * tq, tq)
    kpos = start - WINDOW + lax.broadcasted_iota(jnp.int32, (1, span), 1)
    edge = jnp.where((kpos >= 0) & (kpos < S), 0.0, NEG_INF).astype(F32)
    kz = (klo[pl.ds(start, span), :], khi[pl.ds(start, span), :])
    vz = (vlo[pl.ds(start, span), :], vhi[pl.ds(start, span), :])
    for pair in range(WA_GROUPS // 2):
        qp = (q_ref[0, :, pair * LANES:(pair + 1) * LANES].astype(F32) * (WA_HEAD_DIM ** -0.5)).astype(BF16)
        acc = None
        for half in range(2):
            j = pair * 2 + half
            s = _dot_nt(qp, kz[half]) + tab_ref[j] + edge
            sink = sink_ref[g * WA_GROUPS + j]
            m = jnp.maximum(jnp.max(s, axis=1, keepdims=True), sink)
            p = jnp.exp(s - m)
            l = jnp.sum(p, axis=1, keepdims=True) + jnp.exp(sink - m)
            pn = (p * (1.0 / l)).astype(BF16)
            o = _dot(pn, vz[half])
            acc = o if acc is None else acc + o
        o_ref[0, :, pair * LANES:(pair + 1) * LANES] = acc.astype(BF16)


def _win_attn(waq, wak, wav, sink, tq):
    B, S, D = waq.shape
    n_t = S // tq
    span = tq + 2 * WINDOW
    gw = WA_GROUPS * WA_HEAD_DIM
    slopes = jnp.exp2(-8.0 * jnp.arange(1, WA_HEADS + 1, dtype=F32) / WA_HEADS)
    ir = jnp.arange(tq, dtype=jnp.int32)[:, None]
    jr = jnp.arange(span, dtype=jnp.int32)[None, :]
    dist = jnp.abs(ir + WINDOW - jr)
    tab = jnp.where((dist <= WINDOW)[None], -(slopes[:, None, None] * dist.astype(F32)[None]), NEG_INF)
    pad_shape = pltpu.VMEM((S + 2 * WINDOW, LANES), BF16)
    return pl.pallas_call(
        _win_attn_kernel,
        grid=(B, WA_KV_HEADS, n_t),
        in_specs=[pl.BlockSpec(memory_space=pltpu.SMEM),
                  pl.BlockSpec((1, tq, gw), lambda b, g, t: (b, t, g)),
                  pl.BlockSpec((1, S, LANES), lambda b, g, t: (b, 0, g)),
                  pl.BlockSpec((1, S, LANES), lambda b, g, t: (b, 0, g)),
                  pl.BlockSpec((WA_GROUPS, tq, span), lambda b, g, t: (g, 0, 0))],
        out_specs=pl.BlockSpec((1, tq, gw), lambda b, g, t: (b, t, g)),
        out_shape=jax.ShapeDtypeStruct((B, S, D), BF16),
        scratch_shapes=[pad_shape, pad_shape, pad_shape, pad_shape],
        compiler_params=_cparams(("parallel", "parallel", "arbitrary")),
        name="win_attn",
    )(sink, waq, wak, wav, tab)


def _mem_attn_kernel(q_ref, k_ref, v_ref, o_ref):
    q = (q_ref[0].astype(F32) * (XA_HEAD_DIM ** -0.5)).astype(BF16)
    s = _dot_nt(q, k_ref[0])
    m = jnp.max(s, axis=1, keepdims=True)
    p = jnp.exp(s - m)
    l = jnp.sum(p, axis=1, keepdims=True)
    o_ref[0] = _dot((p * (1.0 / l)).astype(BF16), v_ref[0]).astype(BF16)


def _mem_attn(xaq, memkv, tq):
    B, S, D = xaq.shape
    M = memkv.shape[1]
    hd = XA_HEAD_DIM
    return pl.pallas_call(
        _mem_attn_kernel,
        grid=(B, XA_HEADS, S // tq),
        in_specs=[pl.BlockSpec((1, tq, hd), lambda b, h, t: (b, t, h)),
                  pl.BlockSpec((1, M, hd), lambda b, h, t: (b, 0, h)),
                  pl.BlockSpec((1, M, hd), lambda b, h, t: (b, 0, XA_HEADS + h))],
        out_specs=pl.BlockSpec((1, tq, hd), lambda b, h, t: (b, t, h)),
        out_shape=jax.ShapeDtypeStruct((B, S, D), BF16),
        compiler_params=_cparams(("parallel", "parallel", "parallel")),
        name="mem_attn",
    )(xaq, memkv, memkv)


def _merge_kernel(x_ref, oda_ref, owa_ref, oxa_ref, g1_ref, wg_ref, wda_ref, wwa_ref, wxa_ref, wout_ref,
                  g2_ref, wr_ref, x1_ref, h2_ref, aff_ref):
    d = x_ref.shape[-1]
    x = x_ref[0]
    h = _rms(x, g1_ref[...]).astype(BF16)
    merged = None
    for i, (o_ref, w_ref) in enumerate(((oda_ref, wda_ref), (owa_ref, wwa_ref), (oxa_ref, wxa_ref))):
        gate = jax.nn.sigmoid(_dot(h, wg_ref[:, i * d:(i + 1) * d]))
        term = gate * _dot(o_ref[0], w_ref[...])
        merged = term if merged is None else merged + term
    x1 = x + _dot(merged.astype(BF16), wout_ref[...])
    x1_ref[0] = x1
    h2 = _rms(x1, g2_ref[...]).astype(BF16)
    h2_ref[0] = h2
    logits = _dot(h2, wr_ref[...])
    lt = logits.T[:N_EXPERTS, :]
    m = jnp.max(lt, axis=0, keepdims=True)
    e = jnp.exp(lt - m)
    aff_ref[0] = e / jnp.sum(e, axis=0, keepdims=True)


def _merge(x, oda, owa, oxa, g1, wg, wda, wwa, wxa, wout, g2, wr, tm):
    B, S, D = x.shape
    tok = pl.BlockSpec((1, tm, D), lambda b, i: (b, i, 0))
    full = lambda a: _resident(a.shape, lambda b, i: (0,) * a.ndim)
    return pl.pallas_call(
        _merge_kernel,
        grid=(B, S // tm),
        in_specs=[tok, tok, tok, tok, full(g1), full(wg), full(wda), full(wwa), full(wxa), full(wout),
                  full(g2), full(wr)],
        out_specs=[tok, tok, pl.BlockSpec((1, N_EXPERTS, tm), lambda b, i: (b, 0, i))],
        out_shape=[jax.ShapeDtypeStruct((B, S, D), F32),
                   jax.ShapeDtypeStruct((B, S, D), BF16),
                   jax.ShapeDtypeStruct((B, N_EXPERTS, S), F32)],
        compiler_params=_cparams(("parallel", "parallel")),
        name="merge",
    )(x, oda, owa, oxa, g1, wg, wda, wwa, wxa, wout, g2, wr)


def _route_kernel(aff_ref, pos_ref, *, cap):
    aff = aff_ref[0]
    E, S = aff.shape
    bits = pltpu.bitcast(aff, jnp.int32)
    prefix = jnp.zeros((E, 1), jnp.int32)
    for bit in range(30, -1, -1):
        cand = prefix | (1 << bit)
        cnt = jnp.sum(jnp.where(bits >= cand, 1.0, 0.0), axis=1, keepdims=True)
        prefix = jnp.where(cnt >= cap, cand, prefix)
    gt = bits > prefix
    eq = bits == prefix
    blk = 2 * LANES
    tri = (lax.broadcasted_iota(jnp.int32, (blk, blk), 0) <= lax.broadcasted_iota(jnp.int32, (blk, blk), 1))
    tri = jnp.where(tri, 1.0, 0.0).astype(BF16)

    def cumsum(mask):
        mb = jnp.where(mask, 1.0, 0.0).astype(BF16)
        carry = jnp.zeros((E, 1), F32)
        outs = []
        for i in range(S // blk):
            cs = _dot(mb[:, i * blk:(i + 1) * blk], tri) + carry
            outs.append(cs)
            carry = cs[:, blk - 1:blk]
        return jnp.concatenate(outs, axis=1), carry

    cs_gt, n_gt = cumsum(gt)
    cs_eq, _ = cumsum(eq)
    need = cap - n_gt
    sel = gt | (eq & (cs_eq <= need))
    slot = cs_gt + jnp.minimum(cs_eq, need) - 1.0
    pos_ref[0] = jnp.where(sel, slot, -1.0).astype(jnp.int32)


def _route(aff, cap):
    B, E, S = aff.shape
    return pl.pallas_call(
        functools.partial(_route_kernel, cap=cap),
        grid=(B,),
        in_specs=[pl.BlockSpec((1, E, S), lambda b: (b, 0, 0))],
        out_specs=pl.BlockSpec((1, E, S), lambda b: (b, 0, 0)),
        out_shape=jax.ShapeDtypeStruct((B, E, S), jnp.int32),
        compiler_params=_cparams(("parallel",)),
        name="route",
    )(aff)


def _moe_kernel(pos_ref, aff_ref, h2_ref, wg_ref, wu_ref, wd_ref, o_ref, p_scr, g_scr, xe_scr, y_scr, *, cap):
    e = pl.program_id(1)
    f = pl.program_id(2)
    nf = pl.num_programs(2)
    S = h2_ref.shape[1]

    @pl.when((e == 0) & (f == 0))
    def _():
        o_ref[...] = jnp.zeros_like(o_ref)

    @pl.when(f == 0)
    def _():
        slot = lax.broadcasted_iota(jnp.int32, (cap, S), 0)
        hit = pos_ref[0, 0] == slot
        onehot = jnp.where(hit, 1.0, 0.0).astype(BF16)
        p_scr[...] = onehot
        g_scr[...] = jnp.sum(jnp.where(hit, aff_ref[0, 0], 0.0), axis=1, keepdims=True)
        xe_scr[...] = _dot(onehot, h2_ref[0]).astype(BF16)
        y_scr[...] = jnp.zeros_like(y_scr)

    xe = xe_scr[...]
    a = _dot(xe, wg_ref[0])
    u = _dot(xe, wu_ref[0])
    act = (a * jax.nn.sigmoid(a) * u).astype(BF16)
    y_scr[...] += _dot(act, wd_ref[0])

    @pl.when(f == nf - 1)
    def _():
        ye = (y_scr[...] * g_scr[...]).astype(BF16)
        o_ref[0] += _dot_tn(p_scr[...], ye)


def _moe(pos, aff, h2, wg, wu, wd, cap, tf):
    B, S, D = h2.shape
    E, _, F = wg.shape
    pos4 = pos.reshape(B, E, 1, S)
    aff4 = aff.reshape(B, E, 1, S)
    row_spec = pl.BlockSpec((1, 1, 1, S), lambda b, e, f: (b, e, 0, 0))
    return pl.pallas_call(
        functools.partial(_moe_kernel, cap=cap),
        grid=(B, E, F // tf),
        in_specs=[row_spec, row_spec,
                  pl.BlockSpec((1, S, D), lambda b, e, f: (b, 0, 0)),
                  pl.BlockSpec((1, D, tf), lambda b, e, f: (e, 0, f)),
                  pl.BlockSpec((1, D, tf), lambda b, e, f: (e, 0, f)),
                  pl.BlockSpec((1, tf, D), lambda b, e, f: (e, f, 0))],
        out_specs=pl.BlockSpec((1, S, D), lambda b, e, f: (b, 0, 0)),
        out_shape=jax.ShapeDtypeStruct((B, S, D), F32),
        scratch_shapes=[pltpu.VMEM((cap, S), BF16), pltpu.VMEM((cap, 1), F32),
                        pltpu.VMEM((cap, D), BF16), pltpu.VMEM((cap, D), F32)],
        compiler_params=_cparams(("parallel", "arbitrary", "arbitrary")),
        name="moe",
    )(pos4, aff4, h2, wg, wu, wd)


def _residual_kernel(x_ref, d_ref, o_ref):
    o_ref[0] = x_ref[0] + d_ref[0]


def _residual_norm_kernel(x_ref, d_ref, g_ref, o_ref):
    o_ref[0] = _rms(x_ref[0] + d_ref[0], g_ref[...])


def _residual(x1, delta, g, tm):
    B, S, D = x1.shape
    tok = pl.BlockSpec((1, tm, D), lambda b, i: (b, i, 0))
    if g is None:
        body, extra, extra_specs = _residual_kernel, (), []
    else:
        body, extra, extra_specs = _residual_norm_kernel, (g,), [_resident((1, D), lambda b, i: (0, 0))]
    return pl.pallas_call(
        body,
        grid=(B, S // tm),
        in_specs=[tok, tok] + extra_specs,
        out_specs=tok,
        out_shape=jax.ShapeDtypeStruct((B, S, D), F32),
        compiler_params=_cparams(("parallel", "parallel")),
        name="residual",
    )(x1, delta, *extra)


def _tile(n, pref):
    t = min(n, pref)
    assert n % t == 0, (n, t)
    return t


def _lambda_init(layer):
    return 0.8 - 0.6 * float(np.exp(-0.3 * layer))


def _dup_heads(w, n_heads, width):
    d = w.shape[0]
    w = w.reshape(d, n_heads, 1, width)
    return jnp.broadcast_to(w, (d, n_heads, 2, width)).reshape(d, n_heads * 2 * width)


def kernel(x, mem, attn_norm_g, mem_norm_g, w_in, w_mem_kv, da_lambda_q1, da_lambda_k1, da_lambda_q2,
           da_lambda_k2, da_subln_g, wa_sink, w_da_o, w_wa_o, w_xa_o, w_out, ffn_norm_g, w_router,
           w_exp_gate, w_exp_up, w_exp_down, final_norm_g):
    B, S, D = x.shape
    depth = w_in.shape[0]
    cap = max(1, EC_FACTOR * S // N_EXPERTS)
    row = lambda v: v.reshape(1, -1).astype(F32)
    da_w = DA_HEADS * 2 * DA_HEAD_DIM
    wa_kv_w = WA_KV_HEADS * WA_HEAD_DIM
    o_waq = 3 * da_w
    o_wak = o_waq + WA_HEADS * WA_HEAD_DIM
    o_wav = o_wak + wa_kv_w
    o_xaq = o_wav + wa_kv_w
    o_gate = o_xaq + XA_HEADS * XA_HEAD_DIM
    for l in range(depth):
        lam_init = _lambda_init(l)
        wl = w_in[l]
        w1 = jnp.concatenate([wl[:, :o_wak],
                              _dup_heads(wl[:, o_wak:o_wav], WA_KV_HEADS, WA_HEAD_DIM),
                              _dup_heads(wl[:, o_wav:o_xaq], WA_KV_HEADS, WA_HEAD_DIM),
                              wl[:, o_xaq:o_gate]], axis=1).astype(BF16)
        wg = wl[:, o_gate:].astype(BF16)
        wr = jnp.pad(w_router[l], ((0, 0), (0, LANES - N_EXPERTS))).astype(BF16)

        daq, dak, dav, waq, wak, wav, xaq = _in_proj(x, row(attn_norm_g[l]), w1, _tile(S, 512))
        memkv = _mem_proj(mem, row(mem_norm_g[l]), w_mem_kv[l].astype(BF16))
        lams = [row(v[l]) for v in (da_lambda_q1, da_lambda_k1, da_lambda_q2, da_lambda_k2)]
        oda = _diff_attn(daq, dak, dav, lams, da_subln_g[l].astype(F32), lam_init, _tile(S, 512))
        owa = _win_attn(waq, wak, wav, wa_sink[l].astype(F32), _tile(S, 256))
        oxa = _mem_attn(xaq, memkv, _tile(S, 512))
        x1, h2, aff = _merge(x, oda, owa, oxa, row(attn_norm_g[l]), wg, w_da_o[l].astype(BF16),
                             w_wa_o[l].astype(BF16), w_xa_o[l].astype(BF16), w_out[l].astype(BF16),
                             row(ffn_norm_g[l]), wr, _tile(S, 512))
        pos = _route(aff, cap)
        delta = _moe(pos, aff, h2, w_exp_gate[l].astype(BF16), w_exp_up[l].astype(BF16),
                     w_exp_down[l].astype(BF16), cap, _tile(w_exp_gate.shape[-1], 1024))
        last = l == depth - 1
        x = _residual(x1, delta, row(final_norm_g) if last else None, _tile(S, 512))
    return x
```

```python
import functools

import numpy as np
import jax
import jax.numpy as jnp
from jax import lax
from jax.experimental import pallas as pl
from jax.experimental.pallas import tpu as pltpu

EPS = 1e-6
BLOCK = 128
WINDOW = 128
DA_HEADS = 8
DA_HEAD_DIM = 64
DA_V_DIM = 2 * DA_HEAD_DIM
WA_HEADS = 16
WA_KV_HEADS = 4
WA_GROUPS = WA_HEADS // WA_KV_HEADS
WA_HEAD_DIM = 64
XA_HEADS = 4
XA_HEAD_DIM = 256
N_BRANCH = 3
N_EXPERTS = 16
EC_FACTOR = 2

LOG2E = 1.4426950408889634
DA_Q_SCALE = DA_HEAD_DIM ** -0.5 * LOG2E
XA_Q_SCALE = XA_HEAD_DIM ** -0.5 * LOG2E
WA_Q_SCALE = WA_HEAD_DIM ** -0.5 * LOG2E
WA_ONES_ROWS = 16
DA_KEY_CHUNK = 256
DA_ONES_ROWS = 16

LANES = 128
VMEM_LIMIT = 56 * 1024 * 1024

F32 = jnp.float32
BF16 = jnp.bfloat16
NEG_INF = float("-inf")


def _cparams(sem):
    return pltpu.CompilerParams(dimension_semantics=sem, vmem_limit_bytes=VMEM_LIMIT)


def _rms(xf, g_row):
    ms = jnp.mean(xf * xf, axis=-1, keepdims=True)
    return xf * lax.rsqrt(ms + EPS) * g_row


def _dot(a, b):
    return jnp.dot(a, b, preferred_element_type=F32)


def _dot_nt(a, b):
    return lax.dot_general(a, b, (((1,), (1,)), ((), ())), preferred_element_type=F32)


def _dot_tn(a, b):
    return lax.dot_general(a, b, (((0,), (0,)), ((), ())), preferred_element_type=F32)


def _resident(shape, index_map):
    return pl.BlockSpec(shape, index_map, pipeline_mode=pl.Buffered(1))


def _in_proj_kernel(x_ref, g_ref, w_ref, daq_ref, dak_ref, dav_ref, waq_ref, wak_ref, wavt_ref, xaq_ref):
    d = x_ref.shape[-1]
    tm = x_ref.shape[1]
    h = _rms(x_ref[0], g_ref[...]).astype(BF16)
    col = [0]

    def sec(width):
        r = _dot(h, w_ref[:, col[0]:col[0] + width])
        col[0] += width
        return r

    for ref, scale in ((daq_ref, DA_Q_SCALE), (dak_ref, None), (dav_ref, None)):
        r = sec(d)
        r = (r if scale is None else r * scale).astype(BF16)
        for hh in range(DA_HEADS):
            ref[0, hh] = r[:, hh * LANES:(hh + 1) * LANES]
    waq_ref[0] = (sec(d) * WA_Q_SCALE).astype(BF16)
    wak_ref[0] = sec(d).astype(BF16)
    vt = sec(WA_KV_HEADS * WA_HEAD_DIM).T
    ones = jnp.ones((WA_ONES_ROWS, LANES), BF16)
    for hh in range(WA_KV_HEADS):
        for kb in range(tm // LANES):
            wavt_ref[0, hh, kb, 0:WA_HEAD_DIM, :] = vt[hh * WA_HEAD_DIM:(hh + 1) * WA_HEAD_DIM,
                                                       kb * LANES:(kb + 1) * LANES].astype(BF16)
            wavt_ref[0, hh, kb, WA_HEAD_DIM:, :] = ones
    xaq_ref[0] = (sec(d) * XA_Q_SCALE).astype(BF16)


def _in_proj(x, g, w1, tm):
    B, S, D = x.shape
    n_t = S // tm
    head_shape = jax.ShapeDtypeStruct((B, DA_HEADS, S, LANES), BF16)
    head_spec = pl.BlockSpec((1, DA_HEADS, tm, LANES), lambda b, i: (b, 0, i, 0))
    tok = pl.BlockSpec((1, tm, D), lambda b, i: (b, i, 0))
    tok_shape = jax.ShapeDtypeStruct((B, S, D), BF16)
    vt_rows = WA_HEAD_DIM + WA_ONES_ROWS
    return pl.pallas_call(
        _in_proj_kernel,
        grid=(B, n_t),
        in_specs=[tok, _resident((1, D), lambda b, i: (0, 0)), _resident(w1.shape, lambda b, i: (0, 0))],
        out_specs=[head_spec, head_spec, head_spec, tok, tok,
                   pl.BlockSpec((1, WA_KV_HEADS, tm // LANES, vt_rows, LANES), lambda b, i: (b, 0, i, 0, 0)),
                   tok],
        out_shape=[head_shape, head_shape, head_shape, tok_shape, tok_shape,
                   jax.ShapeDtypeStruct((B, WA_KV_HEADS, S // LANES, vt_rows, LANES), BF16),
                   tok_shape],
        compiler_params=_cparams(("parallel", "parallel")),
        name="in_proj",
    )(x, g, w1)


def _mem_proj_kernel(m_ref, g_ref, w_ref, o_ref):
    h = _rms(m_ref[0], g_ref[...]).astype(BF16)
    o_ref[0] = _dot(h, w_ref[...]).astype(BF16)


def _mem_proj(mem, g, w):
    B, M, D = mem.shape
    N = w.shape[1]
    return pl.pallas_call(
        _mem_proj_kernel,
        grid=(B,),
        in_specs=[pl.BlockSpec((1, M, D), lambda b: (b, 0, 0)),
                  _resident((1, D), lambda b: (0, 0)),
                  _resident((D, N), lambda b: (0, 0))],
        out_specs=pl.BlockSpec((1, M, N), lambda b: (b, 0, 0)),
        out_shape=jax.ShapeDtypeStruct((B, M, N), BF16),
        compiler_params=_cparams(("parallel",)),
        name="mem_proj",
    )(mem, g, w)


def _diff_attn_kernel(lq1_ref, lk1_ref, lq2_ref, lk2_ref, q_ref, k_ref, v_ref, bias_ref, g_ref, o_ref,
                      k1_scr, k2_scr, vt_scr, sa_scr, sb_scr, ma_scr, mb_scr, *, lam_init, n_t):
    S = k_ref.shape[2]
    tq = q_ref.shape[2]
    j = pl.program_id(1)
    n_tiles = pl.num_programs(1) - 1
    ck = DA_KEY_CHUNK
    seq, tile = _da_seq_tile(j, n_t, n_tiles)
    prev_seq, _ = _da_seq_tile(j - 1, n_t, n_tiles)

    @pl.when((tile == 0) & (j < n_tiles))
    def _():
        kk = k_ref[0, 0]
        lane = lax.broadcasted_iota(jnp.int32, kk.shape, 1)
        zero = jnp.zeros_like(kk)
        k1_scr[...] = jnp.where(lane < DA_HEAD_DIM, kk, zero)
        k2_scr[...] = jnp.where(lane >= DA_HEAD_DIM, kk, zero)
        vt_scr[seq % 2, 0:DA_V_DIM, :] = v_ref[0, 0].astype(F32).T.astype(BF16)
        vt_scr[seq % 2, DA_V_DIM:, :] = jnp.ones((DA_ONES_ROWS, S), BF16)

    @pl.when(j == 0)
    def _():
        sb_scr[...] = jnp.zeros_like(sb_scr)
        mb_scr[...] = jnp.zeros_like(mb_scr)

    lam = (jnp.exp(jnp.sum(lq1_ref[...] * lk1_ref[...], keepdims=True))
           - jnp.exp(jnp.sum(lq2_ref[...] * lk2_ref[...], keepdims=True)) + lam_init)

    q = q_ref[0, 0]
    r0 = (S - tq) - tile * tq
    vt_slot = prev_seq % 2

    def step(s_w, m_w, s_r, m_r):
        m1 = m2 = None
        for c in range(S // ck):
            rows = slice(c * ck, (c + 1) * ck)
            bias = bias_ref[0, pl.ds(pl.multiple_of(r0 + c * ck, ck), ck), :]
            s1 = _dot_nt(k1_scr[rows, :], q) + bias
            s2 = _dot_nt(k2_scr[rows, :], q) + bias
            s_w[0, rows, :] = s1
            s_w[1, rows, :] = s2
            c1 = jnp.max(s1, axis=0, keepdims=True)
            c2 = jnp.max(s2, axis=0, keepdims=True)
            m1 = c1 if m1 is None else jnp.maximum(m1, c1)
            m2 = c2 if m2 is None else jnp.maximum(m2, c2)
        m_w[0:1, :] = m1
        m_w[1:2, :] = m2

        def finish(i):
            m = m_r[i:i + 1, :]
            acc = None
            for c in range(S // ck):
                rows = slice(c * ck, (c + 1) * ck)
                p = jnp.exp2(s_r[i, rows, :] - m).astype(BF16)
                d = _dot(vt_scr[vt_slot, :, rows], p)
                acc = d if acc is None else acc + d
            return acc[0:DA_V_DIM, :], acc[DA_V_DIM:DA_V_DIM + 1, :]

        o1, l1 = finish(0)
        o2, l2 = finish(1)
        ot = o1 * (1.0 / l1) - o2 * (lam / l2)
        ms = jnp.mean(ot * ot, axis=0, keepdims=True)
        y = ot * lax.rsqrt(ms + EPS) * g_ref[...] * (1.0 - lam_init)
        o_ref[0] = y.T.astype(BF16)

    @pl.when(j % 2 == 0)
    def _():
        step(sa_scr, ma_scr, sb_scr, mb_scr)

    @pl.when(j % 2 == 1)
    def _():
        step(sb_scr, mb_scr, sa_scr, ma_scr)


def _da_seq_tile(j, n_t, n_tiles):
    jc = jnp.clip(j, 0, n_tiles - 1)
    return jc // n_t, jc % n_t


def _diff_attn(daq, dak, dav, lams, subln_g, lam_init, tq):
    B, H, S, _ = daq.shape
    n_t = S // tq
    rows = 2 * S - tq
    slopes = jnp.exp2(-8.0 * jnp.arange(1, H + 1, dtype=F32) / H)
    r = jnp.arange(rows, dtype=jnp.int32)[:, None]
    c = jnp.arange(tq, dtype=jnp.int32)[None, :]
    dist = jnp.abs(r - (S - tq) - c).astype(F32)
    bias = -((slopes * LOG2E)[:, None, None] * dist[None])
    n_tiles = B * n_t
    lam_spec = _resident((1, DA_HEAD_DIM), lambda h, j: (0, 0))

    def kv_map(h, j):
        return (_da_seq_tile(j, n_t, n_tiles)[0], h, 0, 0)

    def q_map(h, j):
        seq, tile = _da_seq_tile(j, n_t, n_tiles)
        return (seq, h, tile, 0)

    def o_map(h, j):
        seq, tile = _da_seq_tile(j - 1, n_t, n_tiles)
        return (seq, tile, h)

    kv_spec = pl.BlockSpec((1, 1, S, LANES), kv_map)
    return pl.pallas_call(
        functools.partial(_diff_attn_kernel, lam_init=lam_init, n_t=n_t),
        grid=(H, n_tiles + 1),
        in_specs=[lam_spec, lam_spec, lam_spec, lam_spec,
                  pl.BlockSpec((1, 1, tq, LANES), q_map),
                  kv_spec, kv_spec,
                  pl.BlockSpec((1, rows, tq), lambda h, j: (h, 0, 0)),
                  _resident((DA_V_DIM, 1), lambda h, j: (0, 0))],
        out_specs=pl.BlockSpec((1, tq, LANES), o_map),
        out_shape=jax.ShapeDtypeStruct((B, S, H * LANES), BF16),
        scratch_shapes=[pltpu.VMEM((S, LANES), BF16), pltpu.VMEM((S, LANES), BF16),
                        pltpu.VMEM((2, DA_V_DIM + DA_ONES_ROWS, S), BF16),
                        pltpu.VMEM((2, S, tq), F32), pltpu.VMEM((2, S, tq), F32),
                        pltpu.VMEM((2, tq), F32), pltpu.VMEM((2, tq), F32)],
        compiler_params=_cparams(("parallel", "arbitrary")),
        name="diff_attn",
    )(*lams, daq, dak, dav, bias, subln_g.reshape(DA_V_DIM, 1))


def _win_attn_kernel(sink_ref, q_ref, k_ref, vt_ref, tab_ref, o_ref, sa_scr, sb_scr, ma_scr, mb_scr, *, n_t):
    S = k_ref.shape[1]
    tq = q_ref.shape[1]
    span = tq + 2 * WINDOW
    g = pl.program_id(0)
    j = pl.program_id(1)
    n_tiles = pl.num_programs(1) - 1
    _, tile = _da_seq_tile(j, n_t, n_tiles)
    _, prev_tile = _da_seq_tile(j - 1, n_t, n_tiles)

    def window_start(t):
        return pl.multiple_of(jnp.clip(t * tq - WINDOW, 0, S - span), LANES)

    @pl.when(j == 0)
    def _():
        sb_scr[...] = jnp.zeros_like(sb_scr)
        mb_scr[...] = jnp.zeros_like(mb_scr)

    variant = jnp.where(tile == 0, 0, jnp.where(tile == n_t - 1, 2, 1))
    kwin = k_ref[0, pl.ds(window_start(tile), span), :]
    kb0 = window_start(prev_tile) // LANES

    def step(s_w, m_w, s_r, m_r):
        for jh in range(WA_GROUPS):
            pair, half = divmod(jh, 2)
            qp = q_ref[0, :, pair * LANES:(pair + 1) * LANES]
            s = _dot_nt(kwin[:, half * LANES:(half + 1) * LANES], qp) + tab_ref[variant, jh]
            s_w[jh] = s
            sink = sink_ref[g * WA_GROUPS + jh] * LOG2E
            m_w[jh:jh + 1, :] = jnp.maximum(jnp.max(s, axis=0, keepdims=True), sink)
        outs = []
        for jh in range(WA_GROUPS):
            m = m_r[jh:jh + 1, :]
            acc = None
            for kb in range(span // LANES):
                p = jnp.exp2(s_r[jh, kb * LANES:(kb + 1) * LANES, :] - m).astype(BF16)
                d = _dot(vt_ref[0, 0, kb0 + kb], p)
                acc = d if acc is None else acc + d
            sink = sink_ref[g * WA_GROUPS + jh] * LOG2E
            l = acc[WA_HEAD_DIM:WA_HEAD_DIM + 1, :] + jnp.exp2(sink - m)
            outs.append(acc[0:WA_HEAD_DIM, :] * (1.0 / l))
        o_ref[0] = jnp.concatenate(outs, axis=0).T.astype(BF16)

    @pl.when(j % 2 == 0)
    def _():
        step(sa_scr, ma_scr, sb_scr, mb_scr)

    @pl.when(j % 2 == 1)
    def _():
        step(sb_scr, mb_scr, sa_scr, ma_scr)


def _win_attn(waq, wak, wavt, sink, tq):
    B, S, D = waq.shape
    n_t = S // tq
    span = tq + 2 * WINDOW
    assert n_t >= 2 and S >= span, (S, tq)
    gw = WA_GROUPS * WA_HEAD_DIM
    slopes = jnp.exp2(-8.0 * jnp.arange(1, WA_HEADS + 1, dtype=F32) / WA_HEADS) * LOG2E
    jr = jnp.arange(span, dtype=jnp.int32)[:, None]
    ir = jnp.arange(tq, dtype=jnp.int32)[None, :]
    shifts = jnp.array([0, WINDOW, 2 * WINDOW], jnp.int32)[:, None, None]
    dist = jnp.abs(shifts + ir[None] - jr[None])
    tab = jnp.where((dist <= WINDOW)[:, None], -(slopes[None, :, None, None] * dist.astype(F32)[:, None]), NEG_INF)
    vt_rows = wavt.shape[3]
    n_tiles = B * n_t

    def q_map(g, j):
        seq, tile = _da_seq_tile(j, n_t, n_tiles)
        return (seq, tile, g)

    def o_map(g, j):
        seq, tile = _da_seq_tile(j - 1, n_t, n_tiles)
        return (seq, tile, g)

    return pl.pallas_call(
        functools.partial(_win_attn_kernel, n_t=n_t),
        grid=(WA_KV_HEADS, n_tiles + 1),
        in_specs=[pl.BlockSpec(memory_space=pltpu.SMEM),
                  pl.BlockSpec((1, tq, gw), q_map),
                  pl.BlockSpec((1, S, gw), lambda g, j: (_da_seq_tile(j, n_t, n_tiles)[0], 0, g)),
                  pl.BlockSpec((1, 1, S // LANES, vt_rows, LANES),
                               lambda g, j: (_da_seq_tile(j - 1, n_t, n_tiles)[0], g, 0, 0, 0)),
                  pl.BlockSpec((3, WA_GROUPS, span, tq), lambda g, j: (0, g, 0, 0))],
        out_specs=pl.BlockSpec((1, tq, gw), o_map),
        out_shape=jax.ShapeDtypeStruct((B, S, D), BF16),
        scratch_shapes=[pltpu.VMEM((WA_GROUPS, span, tq), F32), pltpu.VMEM((WA_GROUPS, span, tq), F32),
                        pltpu.VMEM((WA_GROUPS, tq), F32), pltpu.VMEM((WA_GROUPS, tq), F32)],
        compiler_params=_cparams(("parallel", "arbitrary")),
        name="win_attn",
    )(sink, waq, wak, wavt, tab)


def _mem_attn_proj(q_ref, kv_ref, w_ref):
    hd = XA_HEAD_DIM
    acc = None
    for hh in range(XA_HEADS):
        q = q_ref[0, :, hh * hd:(hh + 1) * hd]
        k = kv_ref[0, :, hh * hd:(hh + 1) * hd]
        v = kv_ref[0, :, (XA_HEADS + hh) * hd:(XA_HEADS + hh + 1) * hd]
        s = _dot_nt(q, k)
        p = jnp.exp2(s - jnp.max(s, axis=1, keepdims=True))
        pn = (p * (1.0 / jnp.sum(p, axis=1, keepdims=True))).astype(BF16)
        o = _dot(pn, v).astype(BF16)
        d = _dot(o, w_ref[hh * hd:(hh + 1) * hd, :])
        acc = d if acc is None else acc + d
    return acc


def _merge_kernel(x_ref, oda_ref, owa_ref, xaq_ref, memkv_ref, g1_ref, wg_ref, wda_ref, wwa_ref, wxa_ref,
                  wout_ref, g2_ref, wr_ref, x1_ref, h2_ref, aff_ref):
    d = x_ref.shape[-1]
    x = x_ref[0]
    h = _rms(x, g1_ref[...]).astype(BF16)
    branches = (_dot(oda_ref[0], wda_ref[...]), _dot(owa_ref[0], wwa_ref[...]),
                _mem_attn_proj(xaq_ref, memkv_ref, wxa_ref))
    merged = None
    for i, branch in enumerate(branches):
        term = jax.nn.sigmoid(_dot(h, wg_ref[:, i * d:(i + 1) * d])) * branch
        merged = term if merged is None else merged + term
    x1 = x + _dot(merged.astype(BF16), wout_ref[...])
    x1_ref[0] = x1
    h2 = _rms(x1, g2_ref[...]).astype(BF16)
    h2_ref[0] = h2
    logits = _dot(h2, wr_ref[...])
    lt = logits.T[:N_EXPERTS, :]
    m = jnp.max(lt, axis=0, keepdims=True)
    e = jnp.exp(lt - m)
    aff_ref[0] = e / jnp.sum(e, axis=0, keepdims=True)


def _merge(x, oda, owa, xaq, memkv, g1, wg, wda, wwa, wxa, wout, g2, wr, tm):
    B, S, D = x.shape
    tok = pl.BlockSpec((1, tm, D), lambda b, i: (b, i, 0))
    full = lambda a: _resident(a.shape, lambda b, i: (0,) * a.ndim)
    return pl.pallas_call(
        _merge_kernel,
        grid=(B, S // tm),
        in_specs=[tok, tok, tok, tok, pl.BlockSpec((1,) + memkv.shape[1:], lambda b, i: (b, 0, 0)),
                  full(g1), full(wg), full(wda), full(wwa), full(wxa), full(wout), full(g2), full(wr)],
        out_specs=[tok, tok, pl.BlockSpec((1, N_EXPERTS, tm), lambda b, i: (b, 0, i))],
        out_shape=[jax.ShapeDtypeStruct((B, S, D), F32),
                   jax.ShapeDtypeStruct((B, S, D), BF16),
                   jax.ShapeDtypeStruct((B, N_EXPERTS, S), F32)],
        compiler_params=_cparams(("parallel", "parallel")),
        name="merge",
    )(x, oda, owa, xaq, memkv, g1, wg, wda, wwa, wxa, wout, g2, wr)


def _route_kernel(aff_ref, pos_ref, *, cap):
    aff = aff_ref[0]
    E, S = aff.shape
    bits = pltpu.bitcast(aff, jnp.int32)
    prefix = jnp.zeros((E, 1), jnp.int32)
    for bit in range(30, -1, -1):
        cand = prefix | (1 << bit)
        cnt = jnp.sum(jnp.where(bits >= cand, 1.0, 0.0), axis=1, keepdims=True)
        prefix = jnp.where(cnt >= cap, cand, prefix)
    gt = bits > prefix
    eq = bits == prefix
    blk = 2 * LANES
    tri = (lax.broadcasted_iota(jnp.int32, (blk, blk), 0) <= lax.broadcasted_iota(jnp.int32, (blk, blk), 1))
    tri = jnp.where(tri, 1.0, 0.0).astype(BF16)

    def cumsum(mask):
        mb = jnp.where(mask, 1.0, 0.0).astype(BF16)
        carry = jnp.zeros((E, 1), F32)
        outs = []
        for i in range(S // blk):
            cs = _dot(mb[:, i * blk:(i + 1) * blk], tri) + carry
            outs.append(cs)
            carry = cs[:, blk - 1:blk]
        return jnp.concatenate(outs, axis=1), carry

    cs_gt, n_gt = cumsum(gt)
    cs_eq, _ = cumsum(eq)
    need = cap - n_gt
    sel = gt | (eq & (cs_eq <= need))
    slot = cs_gt + jnp.minimum(cs_eq, need) - 1.0
    pos_ref[0] = jnp.where(sel, slot, -1.0).astype(jnp.int32)


def _route(aff, cap):
    B, E, S = aff.shape
    return pl.pallas_call(
        functools.partial(_route_kernel, cap=cap),
        grid=(B,),
        in_specs=[pl.BlockSpec((1, E, S), lambda b: (b, 0, 0))],
        out_specs=pl.BlockSpec((1, E, S), lambda b: (b, 0, 0)),
        out_shape=jax.ShapeDtypeStruct((B, E, S), jnp.int32),
        compiler_params=_cparams(("parallel",)),
        name="route",
    )(aff)


def _moe_kernel(pos_ref, aff_ref, h2_ref, wg_ref, wu_ref, wd_ref, o_ref, p_scr, g_scr, xe_scr, y_scr, *, cap):
    e = pl.program_id(1)
    f = pl.program_id(2)
    nf = pl.num_programs(2)
    S = h2_ref.shape[1]

    @pl.when((e == 0) & (f == 0))
    def _():
        o_ref[...] = jnp.zeros_like(o_ref)

    @pl.when(f == 0)
    def _():
        slot = lax.broadcasted_iota(jnp.int32, (cap, S), 0)
        hit = pos_ref[0, 0] == slot
        onehot = jnp.where(hit, 1.0, 0.0).astype(BF16)
        p_scr[...] = onehot
        g_scr[...] = jnp.sum(jnp.where(hit, aff_ref[0, 0], 0.0), axis=1, keepdims=True)
        xe_scr[...] = _dot(onehot, h2_ref[0]).astype(BF16)
        y_scr[...] = jnp.zeros_like(y_scr)

    xe = xe_scr[...]
    a = _dot(xe, wg_ref[0])
    u = _dot(xe, wu_ref[0])
    act = (a * jax.nn.sigmoid(a) * u).astype(BF16)
    y_scr[...] += _dot(act, wd_ref[0])

    @pl.when(f == nf - 1)
    def _():
        ye = (y_scr[...] * g_scr[...]).astype(BF16)
        o_ref[0] += _dot_tn(p_scr[...], ye)


def _moe(pos, aff, h2, wg, wu, wd, cap, tf):
    B, S, D = h2.shape
    E, _, F = wg.shape
    pos4 = pos.reshape(B, E, 1, S)
    aff4 = aff.reshape(B, E, 1, S)
    row_spec = pl.BlockSpec((1, 1, 1, S), lambda b, e, f: (b, e, 0, 0))
    return pl.pallas_call(
        functools.partial(_moe_kernel, cap=cap),
        grid=(B, E, F // tf),
        in_specs=[row_spec, row_spec,
                  pl.BlockSpec((1, S, D), lambda b, e, f: (b, 0, 0)),
                  pl.BlockSpec((1, D, tf), lambda b, e, f: (e, 0, f)),
                  pl.BlockSpec((1, D, tf), lambda b, e, f: (e, 0, f)),
                  pl.BlockSpec((1, tf, D), lambda b, e, f: (e, f, 0))],
        out_specs=pl.BlockSpec((1, S, D), lambda b, e, f: (b, 0, 0)),
        out_shape=jax.ShapeDtypeStruct((B, S, D), F32),
        scratch_shapes=[pltpu.VMEM((cap, S), BF16), pltpu.VMEM((cap, 1), F32),
                        pltpu.VMEM((cap, D), BF16), pltpu.VMEM((cap, D), F32)],
        compiler_params=_cparams(("parallel", "arbitrary", "arbitrary")),
        name="moe",
    )(pos4, aff4, h2, wg, wu, wd)


def _residual_kernel(x_ref, d_ref, o_ref):
    o_ref[0] = x_ref[0] + d_ref[0]


def _residual_norm_kernel(x_ref, d_ref, g_ref, o_ref):
    o_ref[0] = _rms(x_ref[0] + d_ref[0], g_ref[...])


def _residual(x1, delta, g, tm):
    B, S, D = x1.shape
    tok = pl.BlockSpec((1, tm, D), lambda b, i: (b, i, 0))
    if g is None:
        body, extra, extra_specs = _residual_kernel, (), []
    else:
        body, extra, extra_specs = _residual_norm_kernel, (g,), [_resident((1, D), lambda b, i: (0, 0))]
    return pl.pallas_call(
        body,
        grid=(B, S // tm),
        in_specs=[tok, tok] + extra_specs,
        out_specs=tok,
        out_shape=jax.ShapeDtypeStruct((B, S, D), F32),
        compiler_params=_cparams(("parallel", "parallel")),
        name="residual",
    )(x1, delta, *extra)


def _tile(n, pref):
    t = min(n, pref)
    assert n % t == 0, (n, t)
    return t


def _lambda_init(layer):
    return 0.8 - 0.6 * float(np.exp(-0.3 * layer))


def _lane_half_forms(w, n_heads, width):
    d = w.shape[0]
    w = w.reshape(d, n_heads, width)
    z = jnp.zeros_like(w)
    return jnp.concatenate([w, z, z, w], axis=-1).reshape(d, n_heads * 4 * width)


def kernel(x, mem, attn_norm_g, mem_norm_g, w_in, w_mem_kv, da_lambda_q1, da_lambda_k1, da_lambda_q2,
           da_lambda_k2, da_subln_g, wa_sink, w_da_o, w_wa_o, w_xa_o, w_out, ffn_norm_g, w_router,
           w_exp_gate, w_exp_up, w_exp_down, final_norm_g):
    B, S, D = x.shape
    depth = w_in.shape[0]
    cap = max(1, EC_FACTOR * S // N_EXPERTS)
    row = lambda v: v.reshape(1, -1).astype(F32)
    da_w = DA_HEADS * 2 * DA_HEAD_DIM
    wa_kv_w = WA_KV_HEADS * WA_HEAD_DIM
    o_waq = 3 * da_w
    o_wak = o_waq + WA_HEADS * WA_HEAD_DIM
    o_wav = o_wak + wa_kv_w
    o_xaq = o_wav + wa_kv_w
    o_gate = o_xaq + XA_HEADS * XA_HEAD_DIM
    for l in range(depth):
        lam_init = _lambda_init(l)
        wl = w_in[l]
        w1 = jnp.concatenate([wl[:, :o_wak],
                              _lane_half_forms(wl[:, o_wak:o_wav], WA_KV_HEADS, WA_HEAD_DIM),
                              wl[:, o_wav:o_gate]], axis=1).astype(BF16)
        wg = wl[:, o_gate:].astype(BF16)
        wr = jnp.pad(w_router[l], ((0, 0), (0, LANES - N_EXPERTS))).astype(BF16)

        daq, dak, dav, waq, wak, wavt, xaq = _in_proj(x, row(attn_norm_g[l]), w1, _tile(S, 512))
        memkv = _mem_proj(mem, row(mem_norm_g[l]), w_mem_kv[l].astype(BF16))
        lams = [row(v[l]) for v in (da_lambda_q1, da_lambda_k1, da_lambda_q2, da_lambda_k2)]
        oda = _diff_attn(daq, dak, dav, lams, da_subln_g[l].astype(F32), lam_init, _tile(S, 512))
        owa = _win_attn(waq, wak, wavt, wa_sink[l].astype(F32), _tile(S, 256))
        x1, h2, aff = _merge(x, oda, owa, xaq, memkv, row(attn_norm_g[l]), wg, w_da_o[l].astype(BF16),
                             w_wa_o[l].astype(BF16), w_xa_o[l].astype(BF16), w_out[l].astype(BF16),
                             row(ffn_norm_g[l]), wr, _tile(S, 512))
        pos = _route(aff, cap)
        delta = _moe(pos, aff, h2, w_exp_gate[l].astype(BF16), w_exp_up[l].astype(BF16),
                     w_exp_down[l].astype(BF16), cap, _tile(w_exp_gate.shape[-1], 1024))
        last = l == depth - 1
        x = _residual(x1, delta, row(final_norm_g) if last else None, _tile(S, 512))
    return x
```

```python
import functools

import numpy as np
import jax
import jax.numpy as jnp
from jax import lax
from jax.experimental import pallas as pl
from jax.experimental.pallas import tpu as pltpu

EPS = 1e-6
BLOCK = 128
WINDOW = 128
DA_HEADS = 8
DA_HEAD_DIM = 64
DA_V_DIM = 2 * DA_HEAD_DIM
WA_HEADS = 16
WA_KV_HEADS = 4
WA_GROUPS = WA_HEADS // WA_KV_HEADS
WA_HEAD_DIM = 64
XA_HEADS = 4
XA_HEAD_DIM = 256
N_BRANCH = 3
N_EXPERTS = 16
EC_FACTOR = 2

LOG2E = 1.4426950408889634
DA_Q_SCALE = DA_HEAD_DIM ** -0.5 * LOG2E
XA_Q_SCALE = XA_HEAD_DIM ** -0.5 * LOG2E
WA_Q_SCALE = WA_HEAD_DIM ** -0.5 * LOG2E
WA_ONES_ROWS = 16
DA_KEY_CHUNK = 256
DA_ONES_ROWS = 16

LANES = 128
VMEM_LIMIT = 56 * 1024 * 1024

F32 = jnp.float32
BF16 = jnp.bfloat16
NEG_INF = float("-inf")


def _cparams(sem):
    return pltpu.CompilerParams(dimension_semantics=sem, vmem_limit_bytes=VMEM_LIMIT)


def _rms(xf, g_row):
    ms = jnp.mean(xf * xf, axis=-1, keepdims=True)
    return xf * lax.rsqrt(ms + EPS) * g_row


def _dot(a, b):
    return jnp.dot(a, b, preferred_element_type=F32)


def _dot_nt(a, b):
    return lax.dot_general(a, b, (((1,), (1,)), ((), ())), preferred_element_type=F32)


def _dot_tn(a, b):
    return lax.dot_general(a, b, (((0,), (0,)), ((), ())), preferred_element_type=F32)


def _resident(shape, index_map):
    return pl.BlockSpec(shape, index_map, pipeline_mode=pl.Buffered(1))


def _in_proj_kernel(x_ref, g_ref, w_ref, daq_ref, dak_ref, dav_ref, waq_ref, wak_ref, wavt_ref, xaq_ref):
    d = x_ref.shape[-1]
    tm = x_ref.shape[1]
    h = _rms(x_ref[0], g_ref[...]).astype(BF16)
    col = [0]

    def sec(width):
        r = _dot(h, w_ref[:, col[0]:col[0] + width])
        col[0] += width
        return r

    for ref, scale in ((daq_ref, DA_Q_SCALE), (dak_ref, None), (dav_ref, None)):
        r = sec(d)
        r = (r if scale is None else r * scale).astype(BF16)
        for hh in range(DA_HEADS):
            ref[0, hh] = r[:, hh * LANES:(hh + 1) * LANES]
    waq_ref[0] = (sec(d) * WA_Q_SCALE).astype(BF16)
    wak_ref[0] = sec(d).astype(BF16)
    vt = sec(WA_KV_HEADS * WA_HEAD_DIM).T
    ones = jnp.ones((WA_ONES_ROWS, LANES), BF16)
    for hh in range(WA_KV_HEADS):
        for kb in range(tm // LANES):
            wavt_ref[0, hh, kb, 0:WA_HEAD_DIM, :] = vt[hh * WA_HEAD_DIM:(hh + 1) * WA_HEAD_DIM,
                                                       kb * LANES:(kb + 1) * LANES].astype(BF16)
            wavt_ref[0, hh, kb, WA_HEAD_DIM:, :] = ones
    xaq_ref[0] = (sec(d) * XA_Q_SCALE).astype(BF16)


def _in_proj(x, g, w1, tm):
    B, S, D = x.shape
    n_t = S // tm
    head_shape = jax.ShapeDtypeStruct((B, DA_HEADS, S, LANES), BF16)
    head_spec = pl.BlockSpec((1, DA_HEADS, tm, LANES), lambda b, i: (b, 0, i, 0))
    tok = pl.BlockSpec((1, tm, D), lambda b, i: (b, i, 0))
    tok_shape = jax.ShapeDtypeStruct((B, S, D), BF16)
    vt_rows = WA_HEAD_DIM + WA_ONES_ROWS
    return pl.pallas_call(
        _in_proj_kernel,
        grid=(B, n_t),
        in_specs=[tok, _resident((1, D), lambda b, i: (0, 0)), _resident(w1.shape, lambda b, i: (0, 0))],
        out_specs=[head_spec, head_spec, head_spec, tok, tok,
                   pl.BlockSpec((1, WA_KV_HEADS, tm // LANES, vt_rows, LANES), lambda b, i: (b, 0, i, 0, 0)),
                   tok],
        out_shape=[head_shape, head_shape, head_shape, tok_shape, tok_shape,
                   jax.ShapeDtypeStruct((B, WA_KV_HEADS, S // LANES, vt_rows, LANES), BF16),
                   tok_shape],
        compiler_params=_cparams(("parallel", "parallel")),
        name="in_proj",
    )(x, g, w1)


def _mem_proj_kernel(m_ref, g_ref, w_ref, o_ref):
    h = _rms(m_ref[0], g_ref[...]).astype(BF16)
    o_ref[0] = _dot(h, w_ref[...]).astype(BF16)


def _mem_proj(mem, g, w):
    B, M, D = mem.shape
    N = w.shape[1]
    return pl.pallas_call(
        _mem_proj_kernel,
        grid=(B,),
        in_specs=[pl.BlockSpec((1, M, D), lambda b: (b, 0, 0)),
                  _resident((1, D), lambda b: (0, 0)),
                  _resident((D, N), lambda b: (0, 0))],
        out_specs=pl.BlockSpec((1, M, N), lambda b: (b, 0, 0)),
        out_shape=jax.ShapeDtypeStruct((B, M, N), BF16),
        compiler_params=_cparams(("parallel",)),
        name="mem_proj",
    )(mem, g, w)


def _diff_attn_kernel(lq1_ref, lk1_ref, lq2_ref, lk2_ref, q_ref, k_ref, v_ref, bias_ref, g_ref, o_ref,
                      k1_scr, k2_scr, vt_scr, sa_scr, sb_scr, ma_scr, mb_scr, *, lam_init, n_t):
    S = k_ref.shape[2]
    tq = q_ref.shape[2]
    j = pl.program_id(1)
    n_tiles = pl.num_programs(1) - 1
    ck = DA_KEY_CHUNK
    seq, tile = _da_seq_tile(j, n_t, n_tiles)
    prev_seq, _ = _da_seq_tile(j - 1, n_t, n_tiles)

    @pl.when((tile == 0) & (j < n_tiles))
    def _():
        kk = k_ref[0, 0]
        lane = lax.broadcasted_iota(jnp.int32, kk.shape, 1)
        zero = jnp.zeros_like(kk)
        k1_scr[...] = jnp.where(lane < DA_HEAD_DIM, kk, zero)
        k2_scr[...] = jnp.where(lane >= DA_HEAD_DIM, kk, zero)
        vt_scr[seq % 2, 0:DA_V_DIM, :] = v_ref[0, 0].astype(F32).T.astype(BF16)
        vt_scr[seq % 2, DA_V_DIM:, :] = jnp.ones((DA_ONES_ROWS, S), BF16)

    @pl.when(j == 0)
    def _():
        sb_scr[...] = jnp.zeros_like(sb_scr)
        mb_scr[...] = jnp.zeros_like(mb_scr)

    lam = (jnp.exp(jnp.sum(lq1_ref[...] * lk1_ref[...], keepdims=True))
           - jnp.exp(jnp.sum(lq2_ref[...] * lk2_ref[...], keepdims=True)) + lam_init)

    q = q_ref[0, 0]
    r0 = (S - tq) - tile * tq
    vt_slot = prev_seq % 2

    def step(s_w, m_w, s_r, m_r):
        m1 = m2 = None
        for c in range(S // ck):
            rows = slice(c * ck, (c + 1) * ck)
            bias = bias_ref[0, pl.ds(pl.multiple_of(r0 + c * ck, ck), ck), :]
            s1 = _dot_nt(k1_scr[rows, :], q) + bias
            s2 = _dot_nt(k2_scr[rows, :], q) + bias
            s_w[0, rows, :] = s1
            s_w[1, rows, :] = s2
            c1 = jnp.max(s1, axis=0, keepdims=True)
            c2 = jnp.max(s2, axis=0, keepdims=True)
            m1 = c1 if m1 is None else jnp.maximum(m1, c1)
            m2 = c2 if m2 is None else jnp.maximum(m2, c2)
        m_w[0:1, :] = m1
        m_w[1:2, :] = m2

        def finish(i):
            m = m_r[i:i + 1, :]
            acc = None
            for c in range(S // ck):
                rows = slice(c * ck, (c + 1) * ck)
                p = jnp.exp2((s_r[i, rows, :] - m).astype(BF16))
                d = _dot(vt_scr[vt_slot, :, rows], p)
                acc = d if acc is None else acc + d
            return acc[0:DA_V_DIM, :], acc[DA_V_DIM:DA_V_DIM + 1, :]

        o1, l1 = finish(0)
        o2, l2 = finish(1)
        ot = o1 * (1.0 / l1) - o2 * (lam / l2)
        ms = jnp.mean(ot * ot, axis=0, keepdims=True)
        y = ot * lax.rsqrt(ms + EPS) * g_ref[...] * (1.0 - lam_init)
        o_ref[0] = y.T.astype(BF16)

    @pl.when(j % 2 == 0)
    def _():
        step(sa_scr, ma_scr, sb_scr, mb_scr)

    @pl.when(j % 2 == 1)
    def _():
        step(sb_scr, mb_scr, sa_scr, ma_scr)


def _da_seq_tile(j, n_t, n_tiles):
    jc = jnp.clip(j, 0, n_tiles - 1)
    return jc // n_t, jc % n_t


def _diff_attn(daq, dak, dav, lams, subln_g, lam_init, tq):
    B, H, S, _ = daq.shape
    n_t = S // tq
    rows = 2 * S - tq
    slopes = jnp.exp2(-8.0 * jnp.arange(1, H + 1, dtype=F32) / H)
    r = jnp.arange(rows, dtype=jnp.int32)[:, None]
    c = jnp.arange(tq, dtype=jnp.int32)[None, :]
    dist = jnp.abs(r - (S - tq) - c).astype(F32)
    bias = -((slopes * LOG2E)[:, None, None] * dist[None])
    n_tiles = B * n_t
    lam_spec = _resident((1, DA_HEAD_DIM), lambda h, j: (0, 0))

    def kv_map(h, j):
        return (_da_seq_tile(j, n_t, n_tiles)[0], h, 0, 0)

    def q_map(h, j):
        seq, tile = _da_seq_tile(j, n_t, n_tiles)
        return (seq, h, tile, 0)

    def o_map(h, j):
        seq, tile = _da_seq_tile(j - 1, n_t, n_tiles)
        return (seq, tile, h)

    kv_spec = pl.BlockSpec((1, 1, S, LANES), kv_map)
    return pl.pallas_call(
        functools.partial(_diff_attn_kernel, lam_init=lam_init, n_t=n_t),
        grid=(H, n_tiles + 1),
        in_specs=[lam_spec, lam_spec, lam_spec, lam_spec,
                  pl.BlockSpec((1, 1, tq, LANES), q_map),
                  kv_spec, kv_spec,
                  pl.BlockSpec((1, rows, tq), lambda h, j: (h, 0, 0)),
                  _resident((DA_V_DIM, 1), lambda h, j: (0, 0))],
        out_specs=pl.BlockSpec((1, tq, LANES), o_map),
        out_shape=jax.ShapeDtypeStruct((B, S, H * LANES), BF16),
        scratch_shapes=[pltpu.VMEM((S, LANES), BF16), pltpu.VMEM((S, LANES), BF16),
                        pltpu.VMEM((2, DA_V_DIM + DA_ONES_ROWS, S), BF16),
                        pltpu.VMEM((2, S, tq), F32), pltpu.VMEM((2, S, tq), F32),
                        pltpu.VMEM((2, tq), F32), pltpu.VMEM((2, tq), F32)],
        compiler_params=_cparams(("parallel", "arbitrary")),
        name="diff_attn",
    )(*lams, daq, dak, dav, bias, subln_g.reshape(DA_V_DIM, 1))


def _win_attn_kernel(sink_ref, q_ref, k_ref, vt_ref, tab_ref, o_ref, sa_scr, sb_scr, ma_scr, mb_scr, *, n_t):
    S = k_ref.shape[1]
    tq = q_ref.shape[1]
    span = tq + 2 * WINDOW
    g = pl.program_id(0)
    j = pl.program_id(1)
    n_tiles = pl.num_programs(1) - 1
    _, tile = _da_seq_tile(j, n_t, n_tiles)
    _, prev_tile = _da_seq_tile(j - 1, n_t, n_tiles)

    def window_start(t):
        return pl.multiple_of(jnp.clip(t * tq - WINDOW, 0, S - span), LANES)

    @pl.when(j == 0)
    def _():
        sb_scr[...] = jnp.zeros_like(sb_scr)
        mb_scr[...] = jnp.zeros_like(mb_scr)

    variant = jnp.where(tile == 0, 0, jnp.where(tile == n_t - 1, 2, 1))
    kwin = k_ref[0, pl.ds(window_start(tile), span), :]
    kb0 = window_start(prev_tile) // LANES

    def step(s_w, m_w, s_r, m_r):
        for jh in range(WA_GROUPS):
            pair, half = divmod(jh, 2)
            qp = q_ref[0, :, pair * LANES:(pair + 1) * LANES]
            s = _dot_nt(kwin[:, half * LANES:(half + 1) * LANES], qp) + tab_ref[variant, jh]
            s_w[jh] = s
            sink = sink_ref[g * WA_GROUPS + jh] * LOG2E
            m_w[jh:jh + 1, :] = jnp.maximum(jnp.max(s, axis=0, keepdims=True), sink)
        outs = []
        for jh in range(WA_GROUPS):
            m = m_r[jh:jh + 1, :]
            acc = None
            for kb in range(span // LANES):
                p = jnp.exp2((s_r[jh, kb * LANES:(kb + 1) * LANES, :] - m).astype(BF16))
                d = _dot(vt_ref[0, 0, kb0 + kb], p)
                acc = d if acc is None else acc + d
            sink = sink_ref[g * WA_GROUPS + jh] * LOG2E
            l = acc[WA_HEAD_DIM:WA_HEAD_DIM + 1, :] + jnp.exp2(sink - m)
            outs.append(acc[0:WA_HEAD_DIM, :] * (1.0 / l))
        o_ref[0] = jnp.concatenate(outs, axis=0).T.astype(BF16)

    @pl.when(j % 2 == 0)
    def _():
        step(sa_scr, ma_scr, sb_scr, mb_scr)

    @pl.when(j % 2 == 1)
    def _():
        step(sb_scr, mb_scr, sa_scr, ma_scr)


def _win_attn(waq, wak, wavt, sink, tq):
    B, S, D = waq.shape
    n_t = S // tq
    span = tq + 2 * WINDOW
    assert n_t >= 2 and S >= span, (S, tq)
    gw = WA_GROUPS * WA_HEAD_DIM
    slopes = jnp.exp2(-8.0 * jnp.arange(1, WA_HEADS + 1, dtype=F32) / WA_HEADS) * LOG2E
    jr = jnp.arange(span, dtype=jnp.int32)[:, None]
    ir = jnp.arange(tq, dtype=jnp.int32)[None, :]
    shifts = jnp.array([0, WINDOW, 2 * WINDOW], jnp.int32)[:, None, None]
    dist = jnp.abs(shifts + ir[None] - jr[None])
    tab = jnp.where((dist <= WINDOW)[:, None], -(slopes[None, :, None, None] * dist.astype(F32)[:, None]), NEG_INF)
    vt_rows = wavt.shape[3]
    n_tiles = B * n_t

    def q_map(g, j):
        seq, tile = _da_seq_tile(j, n_t, n_tiles)
        return (seq, tile, g)

    def o_map(g, j):
        seq, tile = _da_seq_tile(j - 1, n_t, n_tiles)
        return (seq, tile, g)

    return pl.pallas_call(
        functools.partial(_win_attn_kernel, n_t=n_t),
        grid=(WA_KV_HEADS, n_tiles + 1),
        in_specs=[pl.BlockSpec(memory_space=pltpu.SMEM),
                  pl.BlockSpec((1, tq, gw), q_map),
                  pl.BlockSpec((1, S, gw), lambda g, j: (_da_seq_tile(j, n_t, n_tiles)[0], 0, g)),
                  pl.BlockSpec((1, 1, S // LANES, vt_rows, LANES),
                               lambda g, j: (_da_seq_tile(j - 1, n_t, n_tiles)[0], g, 0, 0, 0)),
                  pl.BlockSpec((3, WA_GROUPS, span, tq), lambda g, j: (0, g, 0, 0))],
        out_specs=pl.BlockSpec((1, tq, gw), o_map),
        out_shape=jax.ShapeDtypeStruct((B, S, D), BF16),
        scratch_shapes=[pltpu.VMEM((WA_GROUPS, span, tq), F32), pltpu.VMEM((WA_GROUPS, span, tq), F32),
                        pltpu.VMEM((WA_GROUPS, tq), F32), pltpu.VMEM((WA_GROUPS, tq), F32)],
        compiler_params=_cparams(("parallel", "arbitrary")),
        name="win_attn",
    )(sink, waq, wak, wavt, tab)


def _mem_attn_proj(q_ref, kv_ref, w_ref):
    hd = XA_HEAD_DIM
    acc = None
    for hh in range(XA_HEADS):
        q = q_ref[0, :, hh * hd:(hh + 1) * hd]
        k = kv_ref[0, :, hh * hd:(hh + 1) * hd]
        v = kv_ref[0, :, (XA_HEADS + hh) * hd:(XA_HEADS + hh + 1) * hd]
        s = _dot_nt(q, k)
        p = jnp.exp2(s - jnp.max(s, axis=1, keepdims=True))
        pn = (p * (1.0 / jnp.sum(p, axis=1, keepdims=True))).astype(BF16)
        o = _dot(pn, v).astype(BF16)
        d = _dot(o, w_ref[hh * hd:(hh + 1) * hd, :])
        acc = d if acc is None else acc + d
    return acc


def _merge_kernel(x_ref, oda_ref, owa_ref, xaq_ref, memkv_ref, g1_ref, wg_ref, wda_ref, wwa_ref, wxa_ref,
                  wout_ref, g2_ref, wr_ref, x1_ref, h2_ref, aff_ref):
    d = x_ref.shape[-1]
    x = x_ref[0]
    h = _rms(x, g1_ref[...]).astype(BF16)
    branches = (_dot(oda_ref[0], wda_ref[...]), _dot(owa_ref[0], wwa_ref[...]),
                _mem_attn_proj(xaq_ref, memkv_ref, wxa_ref))
    merged = None
    for i, branch in enumerate(branches):
        term = jax.nn.sigmoid(_dot(h, wg_ref[:, i * d:(i + 1) * d])) * branch
        merged = term if merged is None else merged + term
    x1 = x + _dot(merged.astype(BF16), wout_ref[...])
    x1_ref[0] = x1
    h2 = _rms(x1, g2_ref[...]).astype(BF16)
    h2_ref[0] = h2
    logits = _dot(h2, wr_ref[...])
    lt = logits.T[:N_EXPERTS, :]
    m = jnp.max(lt, axis=0, keepdims=True)
    e = jnp.exp(lt - m)
    aff_ref[0] = e / jnp.sum(e, axis=0, keepdims=True)


def _merge(x, oda, owa, xaq, memkv, g1, wg, wda, wwa, wxa, wout, g2, wr, tm):
    B, S, D = x.shape
    tok = pl.BlockSpec((1, tm, D), lambda b, i: (b, i, 0))
    full = lambda a: _resident(a.shape, lambda b, i: (0,) * a.ndim)
    return pl.pallas_call(
        _merge_kernel,
        grid=(B, S // tm),
        in_specs=[tok, tok, tok, tok, pl.BlockSpec((1,) + memkv.shape[1:], lambda b, i: (b, 0, 0)),
                  full(g1), full(wg), full(wda), full(wwa), full(wxa), full(wout), full(g2), full(wr)],
        out_specs=[tok, tok, pl.BlockSpec((1, N_EXPERTS, tm), lambda b, i: (b, 0, i))],
        out_shape=[jax.ShapeDtypeStruct((B, S, D), F32),
                   jax.ShapeDtypeStruct((B, S, D), BF16),
                   jax.ShapeDtypeStruct((B, N_EXPERTS, S), F32)],
        compiler_params=_cparams(("parallel", "parallel")),
        name="merge",
    )(x, oda, owa, xaq, memkv, g1, wg, wda, wwa, wxa, wout, g2, wr)


def _route_kernel(aff_ref, pos_ref, *, cap):
    aff = aff_ref[0]
    E, S = aff.shape
    bits = pltpu.bitcast(aff, jnp.int32)
    prefix = jnp.zeros((E, 1), jnp.int32)
    for bit in range(30, -1, -1):
        cand = prefix | (1 << bit)
        cnt = jnp.sum(jnp.where(bits >= cand, 1.0, 0.0), axis=1, keepdims=True)
        prefix = jnp.where(cnt >= cap, cand, prefix)
    gt = bits > prefix
    eq = bits == prefix
    blk = 2 * LANES
    tri = (lax.broadcasted_iota(jnp.int32, (blk, blk), 0) <= lax.broadcasted_iota(jnp.int32, (blk, blk), 1))
    tri = jnp.where(tri, 1.0, 0.0).astype(BF16)

    def cumsum(mask):
        mb = jnp.where(mask, 1.0, 0.0).astype(BF16)
        carry = jnp.zeros((E, 1), F32)
        outs = []
        for i in range(S // blk):
            cs = _dot(mb[:, i * blk:(i + 1) * blk], tri) + carry
            outs.append(cs)
            carry = cs[:, blk - 1:blk]
        return jnp.concatenate(outs, axis=1), carry

    cs_gt, n_gt = cumsum(gt)
    cs_eq, _ = cumsum(eq)
    need = cap - n_gt
    sel = gt | (eq & (cs_eq <= need))
    slot = cs_gt + jnp.minimum(cs_eq, need) - 1.0
    pos_ref[0] = jnp.where(sel, slot, -1.0).astype(jnp.int32)


def _route(aff, cap):
    B, E, S = aff.shape
    return pl.pallas_call(
        functools.partial(_route_kernel, cap=cap),
        grid=(B,),
        in_specs=[pl.BlockSpec((1, E, S), lambda b: (b, 0, 0))],
        out_specs=pl.BlockSpec((1, E, S), lambda b: (b, 0, 0)),
        out_shape=jax.ShapeDtypeStruct((B, E, S), jnp.int32),
        compiler_params=_cparams(("parallel",)),
        name="route",
    )(aff)


def _moe_kernel(pos_ref, aff_ref, h2_ref, wg_ref, wu_ref, wd_ref, o_ref, p_scr, g_scr, xe_scr, y_scr, *, cap):
    e = pl.program_id(1)
    f = pl.program_id(2)
    nf = pl.num_programs(2)
    S = h2_ref.shape[1]

    @pl.when((e == 0) & (f == 0))
    def _():
        o_ref[...] = jnp.zeros_like(o_ref)

    @pl.when(f == 0)
    def _():
        slot = lax.broadcasted_iota(jnp.int32, (cap, S), 0)
        hit = pos_ref[0, 0] == slot
        onehot = jnp.where(hit, 1.0, 0.0).astype(BF16)
        p_scr[...] = onehot
        g_scr[...] = jnp.sum(jnp.where(hit, aff_ref[0, 0], 0.0), axis=1, keepdims=True)
        xe_scr[...] = _dot(onehot, h2_ref[0]).astype(BF16)
        y_scr[...] = jnp.zeros_like(y_scr)

    xe = xe_scr[...]
    a = _dot(xe, wg_ref[0])
    u = _dot(xe, wu_ref[0])
    act = (a * jax.nn.sigmoid(a) * u).astype(BF16)
    y_scr[...] += _dot(act, wd_ref[0])

    @pl.when(f == nf - 1)
    def _():
        ye = (y_scr[...] * g_scr[...]).astype(BF16)
        o_ref[0] += _dot_tn(p_scr[...], ye)


def _moe(pos, aff, h2, wg, wu, wd, cap, tf):
    B, S, D = h2.shape
    E, _, F = wg.shape
    pos4 = pos.reshape(B, E, 1, S)
    aff4 = aff.reshape(B, E, 1, S)
    row_spec = pl.BlockSpec((1, 1, 1, S), lambda b, e, f: (b, e, 0, 0))
    return pl.pallas_call(
        functools.partial(_moe_kernel, cap=cap),
        grid=(B, E, F // tf),
        in_specs=[row_spec, row_spec,
                  pl.BlockSpec((1, S, D), lambda b, e, f: (b, 0, 0)),
                  pl.BlockSpec((1, D, tf), lambda b, e, f: (e, 0, f)),
                  pl.BlockSpec((1, D, tf), lambda b, e, f: (e, 0, f)),
                  pl.BlockSpec((1, tf, D), lambda b, e, f: (e, f, 0))],
        out_specs=pl.BlockSpec((1, S, D), lambda b, e, f: (b, 0, 0)),
        out_shape=jax.ShapeDtypeStruct((B, S, D), F32),
        scratch_shapes=[pltpu.VMEM((cap, S), BF16), pltpu.VMEM((cap, 1), F32),
                        pltpu.VMEM((cap, D), BF16), pltpu.VMEM((cap, D), F32)],
        compiler_params=_cparams(("parallel", "arbitrary", "arbitrary")),
        name="moe",
    )(pos4, aff4, h2, wg, wu, wd)


def _residual_kernel(x_ref, d_ref, o_ref):
    o_ref[0] = x_ref[0] + d_ref[0]


def _residual_norm_kernel(x_ref, d_ref, g_ref, o_ref):
    o_ref[0] = _rms(x_ref[0] + d_ref[0], g_ref[...])


def _residual(x1, delta, g, tm):
    B, S, D = x1.shape
    tok = pl.BlockSpec((1, tm, D), lambda b, i: (b, i, 0))
    if g is None:
        body, extra, extra_specs = _residual_kernel, (), []
    else:
        body, extra, extra_specs = _residual_norm_kernel, (g,), [_resident((1, D), lambda b, i: (0, 0))]
    return pl.pallas_call(
        body,
        grid=(B, S // tm),
        in_specs=[tok, tok] + extra_specs,
        out_specs=tok,
        out_shape=jax.ShapeDtypeStruct((B, S, D), F32),
        compiler_params=_cparams(("parallel", "parallel")),
        name="residual",
    )(x1, delta, *extra)


def _tile(n, pref):
    t = min(n, pref)
    assert n % t == 0, (n, t)
    return t


def _lambda_init(layer):
    return 0.8 - 0.6 * float(np.exp(-0.3 * layer))


def _lane_half_forms(w, n_heads, width):
    d = w.shape[0]
    w = w.reshape(d, n_heads, width)
    z = jnp.zeros_like(w)
    return jnp.concatenate([w, z, z, w], axis=-1).reshape(d, n_heads * 4 * width)


def kernel(x, mem, attn_norm_g, mem_norm_g, w_in, w_mem_kv, da_lambda_q1, da_lambda_k1, da_lambda_q2,
           da_lambda_k2, da_subln_g, wa_sink, w_da_o, w_wa_o, w_xa_o, w_out, ffn_norm_g, w_router,
           w_exp_gate, w_exp_up, w_exp_down, final_norm_g):
    B, S, D = x.shape
    depth = w_in.shape[0]
    cap = max(1, EC_FACTOR * S // N_EXPERTS)
    row = lambda v: v.reshape(1, -1).astype(F32)
    da_w = DA_HEADS * 2 * DA_HEAD_DIM
    wa_kv_w = WA_KV_HEADS * WA_HEAD_DIM
    o_waq = 3 * da_w
    o_wak = o_waq + WA_HEADS * WA_HEAD_DIM
    o_wav = o_wak + wa_kv_w
    o_xaq = o_wav + wa_kv_w
    o_gate = o_xaq + XA_HEADS * XA_HEAD_DIM
    for l in range(depth):
        lam_init = _lambda_init(l)
        wl = w_in[l]
        w1 = jnp.concatenate([wl[:, :o_wak],
                              _lane_half_forms(wl[:, o_wak:o_wav], WA_KV_HEADS, WA_HEAD_DIM),
                              wl[:, o_wav:o_gate]], axis=1).astype(BF16)
        wg = wl[:, o_gate:].astype(BF16)
        wr = jnp.pad(w_router[l], ((0, 0), (0, LANES - N_EXPERTS))).astype(BF16)

        daq, dak, dav, waq, wak, wavt, xaq = _in_proj(x, row(attn_norm_g[l]), w1, _tile(S, 512))
        memkv = _mem_proj(mem, row(mem_norm_g[l]), w_mem_kv[l].astype(BF16))
        lams = [row(v[l]) for v in (da_lambda_q1, da_lambda_k1, da_lambda_q2, da_lambda_k2)]
        oda = _diff_attn(daq, dak, dav, lams, da_subln_g[l].astype(F32), lam_init, _tile(S, 512))
        owa = _win_attn(waq, wak, wavt, wa_sink[l].astype(F32), _tile(S, 256))
        x1, h2, aff = _merge(x, oda, owa, xaq, memkv, row(attn_norm_g[l]), wg, w_da_o[l].astype(BF16),
                             w_wa_o[l].astype(BF16), w_xa_o[l].astype(BF16), w_out[l].astype(BF16),
                             row(ffn_norm_g[l]), wr, _tile(S, 512))
        pos = _route(aff, cap)
        delta = _moe(pos, aff, h2, w_exp_gate[l].astype(BF16), w_exp_up[l].astype(BF16),
                     w_exp_down[l].astype(BF16), cap, _tile(w_exp_gate.shape[-1], 1024))
        last = l == depth - 1
        x = _residual(x1, delta, row(final_norm_g) if last else None, _tile(S, 512))
    return x
```

```python
import functools

import numpy as np
import jax
import jax.numpy as jnp
from jax import lax
from jax.experimental import pallas as pl
from jax.experimental.pallas import tpu as pltpu

EPS = 1e-6
BLOCK = 128
WINDOW = 128
DA_HEADS = 8
DA_HEAD_DIM = 64
DA_V_DIM = 2 * DA_HEAD_DIM
WA_HEADS = 16
WA_KV_HEADS = 4
WA_GROUPS = WA_HEADS // WA_KV_HEADS
WA_HEAD_DIM = 64
XA_HEADS = 4
XA_HEAD_DIM = 256
N_BRANCH = 3
N_EXPERTS = 16
EC_FACTOR = 2

LOG2E = 1.4426950408889634
DA_Q_SCALE = DA_HEAD_DIM ** -0.5 * LOG2E
XA_Q_SCALE = XA_HEAD_DIM ** -0.5 * LOG2E
WA_Q_SCALE = WA_HEAD_DIM ** -0.5 * LOG2E
WA_ONES_ROWS = 16
DA_KEY_CHUNK = 256
DA_ONES_ROWS = 16

LANES = 128
VMEM_LIMIT = 56 * 1024 * 1024

F32 = jnp.float32
BF16 = jnp.bfloat16
NEG_INF = float("-inf")


def _cparams(sem, flags=None):
    return pltpu.CompilerParams(dimension_semantics=sem, vmem_limit_bytes=VMEM_LIMIT, flags=flags)


def _rms(xf, g_row):
    ms = jnp.mean(xf * xf, axis=-1, keepdims=True)
    return xf * lax.rsqrt(ms + EPS) * g_row


def _dot(a, b):
    return jnp.dot(a, b, preferred_element_type=F32)


def _dot_nt(a, b):
    return lax.dot_general(a, b, (((1,), (1,)), ((), ())), preferred_element_type=F32)


def _dot_tn(a, b):
    return lax.dot_general(a, b, (((0,), (0,)), ((), ())), preferred_element_type=F32)


def _resident(shape, index_map):
    return pl.BlockSpec(shape, index_map, pipeline_mode=pl.Buffered(1))


def _in_proj_kernel(x_ref, g_ref, w_ref, daq_ref, dak_ref, dav_ref, waq_ref, wak_ref, wavt_ref, xaq_ref):
    d = x_ref.shape[-1]
    tm = x_ref.shape[1]
    h = _rms(x_ref[0], g_ref[...]).astype(BF16)
    col = [0]

    def sec(width):
        r = _dot(h, w_ref[:, col[0]:col[0] + width])
        col[0] += width
        return r

    for ref, scale in ((daq_ref, DA_Q_SCALE), (dak_ref, None), (dav_ref, None)):
        r = sec(d)
        r = (r if scale is None else r * scale).astype(BF16)
        for hh in range(DA_HEADS):
            ref[0, hh] = r[:, hh * LANES:(hh + 1) * LANES]
    waq_ref[0] = (sec(d) * WA_Q_SCALE).astype(BF16)
    wak_ref[0] = sec(d).astype(BF16)
    vt = sec(WA_KV_HEADS * WA_HEAD_DIM).T
    ones = jnp.ones((WA_ONES_ROWS, LANES), BF16)
    for hh in range(WA_KV_HEADS):
        for kb in range(tm // LANES):
            wavt_ref[0, hh, kb, 0:WA_HEAD_DIM, :] = vt[hh * WA_HEAD_DIM:(hh + 1) * WA_HEAD_DIM,
                                                       kb * LANES:(kb + 1) * LANES].astype(BF16)
            wavt_ref[0, hh, kb, WA_HEAD_DIM:, :] = ones
    xaq_ref[0] = (sec(d) * XA_Q_SCALE).astype(BF16)


def _in_proj(x, g, w1, tm):
    B, S, D = x.shape
    n_t = S // tm
    head_shape = jax.ShapeDtypeStruct((B, DA_HEADS, S, LANES), BF16)
    head_spec = pl.BlockSpec((1, DA_HEADS, tm, LANES), lambda b, i: (b, 0, i, 0))
    tok = pl.BlockSpec((1, tm, D), lambda b, i: (b, i, 0))
    tok_shape = jax.ShapeDtypeStruct((B, S, D), BF16)
    vt_rows = WA_HEAD_DIM + WA_ONES_ROWS
    return pl.pallas_call(
        _in_proj_kernel,
        grid=(B, n_t),
        in_specs=[tok, _resident((1, D), lambda b, i: (0, 0)), _resident(w1.shape, lambda b, i: (0, 0))],
        out_specs=[head_spec, head_spec, head_spec, tok, tok,
                   pl.BlockSpec((1, WA_KV_HEADS, tm // LANES, vt_rows, LANES), lambda b, i: (b, 0, i, 0, 0)),
                   tok],
        out_shape=[head_shape, head_shape, head_shape, tok_shape, tok_shape,
                   jax.ShapeDtypeStruct((B, WA_KV_HEADS, S // LANES, vt_rows, LANES), BF16),
                   tok_shape],
        compiler_params=_cparams(("parallel", "parallel")),
        name="in_proj",
    )(x, g, w1)


def _mem_proj_kernel(m_ref, g_ref, w_ref, o_ref):
    h = _rms(m_ref[0], g_ref[...]).astype(BF16)
    o_ref[0] = _dot(h, w_ref[...]).astype(BF16)


def _mem_proj(mem, g, w):
    B, M, D = mem.shape
    N = w.shape[1]
    return pl.pallas_call(
        _mem_proj_kernel,
        grid=(B,),
        in_specs=[pl.BlockSpec((1, M, D), lambda b: (b, 0, 0)),
                  _resident((1, D), lambda b: (0, 0)),
                  _resident((D, N), lambda b: (0, 0))],
        out_specs=pl.BlockSpec((1, M, N), lambda b: (b, 0, 0)),
        out_shape=jax.ShapeDtypeStruct((B, M, N), BF16),
        compiler_params=_cparams(("parallel",)),
        name="mem_proj",
    )(mem, g, w)


def _diff_attn_kernel(lq1_ref, lk1_ref, lq2_ref, lk2_ref, q_ref, k_ref, v_ref, bias_ref, g_ref, o_ref,
                      k1_scr, k2_scr, vt_scr, sa_scr, sb_scr, ma_scr, mb_scr, *, lam_init, n_t):
    S = k_ref.shape[2]
    tq = q_ref.shape[2]
    j = pl.program_id(1)
    n_tiles = pl.num_programs(1) - 1
    ck = DA_KEY_CHUNK
    seq, tile = _da_seq_tile(j, n_t, n_tiles)
    prev_seq, _ = _da_seq_tile(j - 1, n_t, n_tiles)

    @pl.when((tile == 0) & (j < n_tiles))
    def _():
        kk = k_ref[0, 0]
        lane = lax.broadcasted_iota(jnp.int32, kk.shape, 1)
        zero = jnp.zeros_like(kk)
        k1_scr[...] = jnp.where(lane < DA_HEAD_DIM, kk, zero)
        k2_scr[...] = jnp.where(lane >= DA_HEAD_DIM, kk, zero)
        vt_scr[seq % 2, 0:DA_V_DIM, :] = v_ref[0, 0].astype(F32).T.astype(BF16)
        vt_scr[seq % 2, DA_V_DIM:, :] = jnp.ones((DA_ONES_ROWS, S), BF16)

    @pl.when(j == 0)
    def _():
        sb_scr[...] = jnp.zeros_like(sb_scr)
        mb_scr[...] = jnp.zeros_like(mb_scr)

    lam = (jnp.exp(jnp.sum(lq1_ref[...] * lk1_ref[...], keepdims=True))
           - jnp.exp(jnp.sum(lq2_ref[...] * lk2_ref[...], keepdims=True)) + lam_init)

    q = q_ref[0, 0]
    r0 = (S - tq) - tile * tq
    vt_slot = prev_seq % 2

    def step(s_w, m_w, s_r, m_r):
        m_prev = (m_r[0:1, :], m_r[1:2, :])
        m_new = [None, None]
        acc = [None, None]
        for c in range(S // ck):
            rows = slice(c * ck, (c + 1) * ck)
            for i in range(2):
                p = jnp.exp2((s_r[i, rows, :] - m_prev[i]).astype(BF16))
                d = _dot(vt_scr[vt_slot, :, rows], p)
                acc[i] = d if acc[i] is None else acc[i] + d
            bias = bias_ref[0, pl.ds(pl.multiple_of(r0 + c * ck, ck), ck), :]
            for i, k_scr in enumerate((k1_scr, k2_scr)):
                s = _dot_nt(k_scr[rows, :], q) + bias
                s_w[i, rows, :] = s
                cm = jnp.max(s, axis=0, keepdims=True)
                m_new[i] = cm if m_new[i] is None else jnp.maximum(m_new[i], cm)
        m_w[0:1, :] = m_new[0]
        m_w[1:2, :] = m_new[1]
        o1, l1 = acc[0][0:DA_V_DIM, :], acc[0][DA_V_DIM:DA_V_DIM + 1, :]
        o2, l2 = acc[1][0:DA_V_DIM, :], acc[1][DA_V_DIM:DA_V_DIM + 1, :]
        ot = o1 * (1.0 / l1) - o2 * (lam / l2)
        ms = jnp.mean(ot * ot, axis=0, keepdims=True)
        y = ot * lax.rsqrt(ms + EPS) * g_ref[...] * (1.0 - lam_init)
        o_ref[0] = y.T.astype(BF16)

    @pl.when(j % 2 == 0)
    def _():
        step(sa_scr, ma_scr, sb_scr, mb_scr)

    @pl.when(j % 2 == 1)
    def _():
        step(sb_scr, mb_scr, sa_scr, ma_scr)


def _da_seq_tile(j, n_t, n_tiles):
    jc = jnp.clip(j, 0, n_tiles - 1)
    return jc // n_t, jc % n_t


def _diff_attn(daq, dak, dav, lams, subln_g, lam_init, tq):
    B, H, S, _ = daq.shape
    n_t = S // tq
    rows = 2 * S - tq
    slopes = jnp.exp2(-8.0 * jnp.arange(1, H + 1, dtype=F32) / H)
    r = jnp.arange(rows, dtype=jnp.int32)[:, None]
    c = jnp.arange(tq, dtype=jnp.int32)[None, :]
    dist = jnp.abs(r - (S - tq) - c).astype(F32)
    bias = -((slopes * LOG2E)[:, None, None] * dist[None])
    n_tiles = B * n_t
    lam_spec = _resident((1, DA_HEAD_DIM), lambda h, j: (0, 0))

    def kv_map(h, j):
        return (_da_seq_tile(j, n_t, n_tiles)[0], h, 0, 0)

    def q_map(h, j):
        seq, tile = _da_seq_tile(j, n_t, n_tiles)
        return (seq, h, tile, 0)

    def o_map(h, j):
        seq, tile = _da_seq_tile(j - 1, n_t, n_tiles)
        return (seq, tile, h)

    kv_spec = pl.BlockSpec((1, 1, S, LANES), kv_map)
    return pl.pallas_call(
        functools.partial(_diff_attn_kernel, lam_init=lam_init, n_t=n_t),
        grid=(H, n_tiles + 1),
        in_specs=[lam_spec, lam_spec, lam_spec, lam_spec,
                  pl.BlockSpec((1, 1, tq, LANES), q_map),
                  kv_spec, kv_spec,
                  pl.BlockSpec((1, rows, tq), lambda h, j: (h, 0, 0)),
                  _resident((DA_V_DIM, 1), lambda h, j: (0, 0))],
        out_specs=pl.BlockSpec((1, tq, LANES), o_map),
        out_shape=jax.ShapeDtypeStruct((B, S, H * LANES), BF16),
        scratch_shapes=[pltpu.VMEM((S, LANES), BF16), pltpu.VMEM((S, LANES), BF16),
                        pltpu.VMEM((2, DA_V_DIM + DA_ONES_ROWS, S), BF16),
                        pltpu.VMEM((2, S, tq), F32), pltpu.VMEM((2, S, tq), F32),
                        pltpu.VMEM((2, tq), F32), pltpu.VMEM((2, tq), F32)],
        compiler_params=_cparams(("parallel", "arbitrary")),
        name="diff_attn",
    )(*lams, daq, dak, dav, bias, subln_g.reshape(DA_V_DIM, 1))


def _win_attn_kernel(sink_ref, q_ref, k_ref, vt_ref, tab_ref, o_ref, sa_scr, sb_scr, ma_scr, mb_scr, *, n_t):
    S = k_ref.shape[1]
    tq = q_ref.shape[1]
    span = tq + 2 * WINDOW
    g = pl.program_id(0)
    j = pl.program_id(1)
    n_tiles = pl.num_programs(1) - 1
    _, tile = _da_seq_tile(j, n_t, n_tiles)
    _, prev_tile = _da_seq_tile(j - 1, n_t, n_tiles)

    def window_start(t):
        return pl.multiple_of(jnp.clip(t * tq - WINDOW, 0, S - span), LANES)

    @pl.when(j == 0)
    def _():
        sb_scr[...] = jnp.zeros_like(sb_scr)
        mb_scr[...] = jnp.zeros_like(mb_scr)

    variant = jnp.where(tile == 0, 0, jnp.where(tile == n_t - 1, 2, 1))
    kwin = k_ref[0, pl.ds(window_start(tile), span), :]
    kb0 = window_start(prev_tile) // LANES

    def step(s_w, m_w, s_r, m_r):
        outs = []
        for jh in range(WA_GROUPS):
            sink = sink_ref[g * WA_GROUPS + jh] * LOG2E
            m = m_r[jh:jh + 1, :]
            acc = None
            for kb in range(span // LANES):
                p = jnp.exp2((s_r[jh, kb * LANES:(kb + 1) * LANES, :] - m).astype(BF16))
                d = _dot(vt_ref[0, 0, kb0 + kb], p)
                acc = d if acc is None else acc + d
            l = acc[WA_HEAD_DIM:WA_HEAD_DIM + 1, :] + jnp.exp2(sink - m)
            outs.append(acc[0:WA_HEAD_DIM, :] * (1.0 / l))
            pair, half = divmod(jh, 2)
            qp = q_ref[0, :, pair * LANES:(pair + 1) * LANES]
            s = _dot_nt(kwin[:, half * LANES:(half + 1) * LANES], qp) + tab_ref[variant, jh]
            s_w[jh] = s
            m_w[jh:jh + 1, :] = jnp.maximum(jnp.max(s, axis=0, keepdims=True), sink)
        o_ref[0] = jnp.concatenate(outs, axis=0).T.astype(BF16)

    @pl.when(j % 2 == 0)
    def _():
        step(sa_scr, ma_scr, sb_scr, mb_scr)

    @pl.when(j % 2 == 1)
    def _():
        step(sb_scr, mb_scr, sa_scr, ma_scr)


def _win_attn(waq, wak, wavt, sink, tq):
    B, S, D = waq.shape
    n_t = S // tq
    span = tq + 2 * WINDOW
    assert n_t >= 2 and S >= span, (S, tq)
    gw = WA_GROUPS * WA_HEAD_DIM
    slopes = jnp.exp2(-8.0 * jnp.arange(1, WA_HEADS + 1, dtype=F32) / WA_HEADS) * LOG2E
    jr = jnp.arange(span, dtype=jnp.int32)[:, None]
    ir = jnp.arange(tq, dtype=jnp.int32)[None, :]
    shifts = jnp.array([0, WINDOW, 2 * WINDOW], jnp.int32)[:, None, None]
    dist = jnp.abs(shifts + ir[None] - jr[None])
    tab = jnp.where((dist <= WINDOW)[:, None], -(slopes[None, :, None, None] * dist.astype(F32)[:, None]), NEG_INF)
    vt_rows = wavt.shape[3]
    n_tiles = B * n_t

    def q_map(g, j):
        seq, tile = _da_seq_tile(j, n_t, n_tiles)
        return (seq, tile, g)

    def o_map(g, j):
        seq, tile = _da_seq_tile(j - 1, n_t, n_tiles)
        return (seq, tile, g)

    return pl.pallas_call(
        functools.partial(_win_attn_kernel, n_t=n_t),
        grid=(WA_KV_HEADS, n_tiles + 1),
        in_specs=[pl.BlockSpec(memory_space=pltpu.SMEM),
                  pl.BlockSpec((1, tq, gw), q_map),
                  pl.BlockSpec((1, S, gw), lambda g, j: (_da_seq_tile(j, n_t, n_tiles)[0], 0, g)),
                  pl.BlockSpec((1, 1, S // LANES, vt_rows, LANES),
                               lambda g, j: (_da_seq_tile(j - 1, n_t, n_tiles)[0], g, 0, 0, 0)),
                  pl.BlockSpec((3, WA_GROUPS, span, tq), lambda g, j: (0, g, 0, 0))],
        out_specs=pl.BlockSpec((1, tq, gw), o_map),
        out_shape=jax.ShapeDtypeStruct((B, S, D), BF16),
        scratch_shapes=[pltpu.VMEM((WA_GROUPS, span, tq), F32), pltpu.VMEM((WA_GROUPS, span, tq), F32),
                        pltpu.VMEM((WA_GROUPS, tq), F32), pltpu.VMEM((WA_GROUPS, tq), F32)],
        compiler_params=_cparams(("parallel", "arbitrary")),
        name="win_attn",
    )(sink, waq, wak, wavt, tab)


def _mem_attn_proj(q_ref, kv_ref, w_ref):
    hd = XA_HEAD_DIM
    acc = None
    for hh in range(XA_HEADS):
        q = q_ref[0, :, hh * hd:(hh + 1) * hd]
        k = kv_ref[0, :, hh * hd:(hh + 1) * hd]
        v = kv_ref[0, :, (XA_HEADS + hh) * hd:(XA_HEADS + hh + 1) * hd]
        s = _dot_nt(q, k)
        p = jnp.exp2(s - jnp.max(s, axis=1, keepdims=True))
        pn = (p * (1.0 / jnp.sum(p, axis=1, keepdims=True))).astype(BF16)
        o = _dot(pn, v).astype(BF16)
        d = _dot(o, w_ref[hh * hd:(hh + 1) * hd, :])
        acc = d if acc is None else acc + d
    return acc


def _merge_kernel(x_ref, oda_ref, owa_ref, xaq_ref, memkv_ref, g1_ref, wg_ref, wda_ref, wwa_ref, wxa_ref,
                  wout_ref, g2_ref, wr_ref, x1_ref, h2_ref, aff_ref):
    d = x_ref.shape[-1]
    x = x_ref[0]
    h = _rms(x, g1_ref[...]).astype(BF16)
    branches = (_dot(oda_ref[0], wda_ref[...]), _dot(owa_ref[0], wwa_ref[...]),
                _mem_attn_proj(xaq_ref, memkv_ref, wxa_ref))
    merged = None
    for i, branch in enumerate(branches):
        term = jax.nn.sigmoid(_dot(h, wg_ref[:, i * d:(i + 1) * d])) * branch
        merged = term if merged is None else merged + term
    x1 = x + _dot(merged.astype(BF16), wout_ref[...])
    x1_ref[0] = x1
    h2 = _rms(x1, g2_ref[...]).astype(BF16)
    h2_ref[0] = h2
    logits = _dot(h2, wr_ref[...])
    lt = logits.T[:N_EXPERTS, :]
    m = jnp.max(lt, axis=0, keepdims=True)
    e = jnp.exp(lt - m)
    aff_ref[0] = e / jnp.sum(e, axis=0, keepdims=True)


def _merge(x, oda, owa, xaq, memkv, g1, wg, wda, wwa, wxa, wout, g2, wr, tm):
    B, S, D = x.shape
    tok = pl.BlockSpec((1, tm, D), lambda b, i: (b, i, 0))
    full = lambda a: _resident(a.shape, lambda b, i: (0,) * a.ndim)
    return pl.pallas_call(
        _merge_kernel,
        grid=(B, S // tm),
        in_specs=[tok, tok, tok, tok, pl.BlockSpec((1,) + memkv.shape[1:], lambda b, i: (b, 0, 0)),
                  full(g1), full(wg), full(wda), full(wwa), full(wxa), full(wout), full(g2), full(wr)],
        out_specs=[tok, tok, pl.BlockSpec((1, N_EXPERTS, tm), lambda b, i: (b, 0, i))],
        out_shape=[jax.ShapeDtypeStruct((B, S, D), F32),
                   jax.ShapeDtypeStruct((B, S, D), BF16),
                   jax.ShapeDtypeStruct((B, N_EXPERTS, S), F32)],
        compiler_params=_cparams(("parallel", "parallel")),
        name="merge",
    )(x, oda, owa, xaq, memkv, g1, wg, wda, wwa, wxa, wout, g2, wr)


def _route_kernel(aff_ref, pos_ref, *, cap):
    aff = aff_ref[0]
    E, S = aff.shape
    bits = pltpu.bitcast(aff, jnp.int32)
    prefix = jnp.zeros((E, 1), jnp.int32)
    for bit in range(30, -1, -1):
        cand = prefix | (1 << bit)
        cnt = jnp.sum(jnp.where(bits >= cand, 1.0, 0.0), axis=1, keepdims=True)
        prefix = jnp.where(cnt >= cap, cand, prefix)
    gt = bits > prefix
    eq = bits == prefix
    blk = 2 * LANES
    tri = (lax.broadcasted_iota(jnp.int32, (blk, blk), 0) <= lax.broadcasted_iota(jnp.int32, (blk, blk), 1))
    tri = jnp.where(tri, 1.0, 0.0).astype(BF16)

    def cumsum(mask):
        mb = jnp.where(mask, 1.0, 0.0).astype(BF16)
        carry = jnp.zeros((E, 1), F32)
        outs = []
        for i in range(S // blk):
            cs = _dot(mb[:, i * blk:(i + 1) * blk], tri) + carry
            outs.append(cs)
            carry = cs[:, blk - 1:blk]
        return jnp.concatenate(outs, axis=1), carry

    cs_gt, n_gt = cumsum(gt)
    cs_eq, _ = cumsum(eq)
    need = cap - n_gt
    sel = gt | (eq & (cs_eq <= need))
    slot = cs_gt + jnp.minimum(cs_eq, need) - 1.0
    pos_ref[0] = jnp.where(sel, slot, -1.0).astype(jnp.int32)


def _route(aff, cap):
    B, E, S = aff.shape
    return pl.pallas_call(
        functools.partial(_route_kernel, cap=cap),
        grid=(B,),
        in_specs=[pl.BlockSpec((1, E, S), lambda b: (b, 0, 0))],
        out_specs=pl.BlockSpec((1, E, S), lambda b: (b, 0, 0)),
        out_shape=jax.ShapeDtypeStruct((B, E, S), jnp.int32),
        compiler_params=_cparams(("parallel",)),
        name="route",
    )(aff)


def _moe_kernel(pos_ref, aff_ref, h2_ref, wg_ref, wu_ref, wd_ref, o_ref, p_scr, g_scr, xe_scr, y_scr, *, cap):
    e = pl.program_id(1)
    f = pl.program_id(2)
    nf = pl.num_programs(2)
    S = h2_ref.shape[1]

    @pl.when((e == 0) & (f == 0))
    def _():
        o_ref[...] = jnp.zeros_like(o_ref)

    @pl.when(f == 0)
    def _():
        slot = lax.broadcasted_iota(jnp.int32, (cap, S), 0)
        hit = pos_ref[0, 0] == slot
        onehot = jnp.where(hit, 1.0, 0.0).astype(BF16)
        p_scr[...] = onehot
        g_scr[...] = jnp.sum(jnp.where(hit, aff_ref[0, 0], 0.0), axis=1, keepdims=True)
        xe_scr[...] = _dot(onehot, h2_ref[0]).astype(BF16)
        y_scr[...] = jnp.zeros_like(y_scr)

    xe = xe_scr[...]
    a = _dot(xe, wg_ref[0])
    u = _dot(xe, wu_ref[0])
    act = (a * jax.nn.sigmoid(a) * u).astype(BF16)
    y_scr[...] += _dot(act, wd_ref[0])

    @pl.when(f == nf - 1)
    def _():
        ye = (y_scr[...] * g_scr[...]).astype(BF16)
        o_ref[0] += _dot_tn(p_scr[...], ye)


def _moe(pos, aff, h2, wg, wu, wd, cap, tf):
    B, S, D = h2.shape
    E, _, F = wg.shape
    pos4 = pos.reshape(B, E, 1, S)
    aff4 = aff.reshape(B, E, 1, S)
    row_spec = pl.BlockSpec((1, 1, 1, S), lambda b, e, f: (b, e, 0, 0))
    return pl.pallas_call(
        functools.partial(_moe_kernel, cap=cap),
        grid=(B, E, F // tf),
        in_specs=[row_spec, row_spec,
                  pl.BlockSpec((1, S, D), lambda b, e, f: (b, 0, 0)),
                  pl.BlockSpec((1, D, tf), lambda b, e, f: (e, 0, f)),
                  pl.BlockSpec((1, D, tf), lambda b, e, f: (e, 0, f)),
                  pl.BlockSpec((1, tf, D), lambda b, e, f: (e, f, 0))],
        out_specs=pl.BlockSpec((1, S, D), lambda b, e, f: (b, 0, 0)),
        out_shape=jax.ShapeDtypeStruct((B, S, D), F32),
        scratch_shapes=[pltpu.VMEM((cap, S), BF16), pltpu.VMEM((cap, 1), F32),
                        pltpu.VMEM((cap, D), BF16), pltpu.VMEM((cap, D), F32)],
        compiler_params=_cparams(("parallel", "arbitrary", "arbitrary")),
        name="moe",
    )(pos4, aff4, h2, wg, wu, wd)


def _residual_kernel(x_ref, d_ref, o_ref):
    o_ref[0] = x_ref[0] + d_ref[0]


def _residual_norm_kernel(x_ref, d_ref, g_ref, o_ref):
    o_ref[0] = _rms(x_ref[0] + d_ref[0], g_ref[...])


def _residual(x1, delta, g, tm):
    B, S, D = x1.shape
    tok = pl.BlockSpec((1, tm, D), lambda b, i: (b, i, 0))
    if g is None:
        body, extra, extra_specs = _residual_kernel, (), []
    else:
        body, extra, extra_specs = _residual_norm_kernel, (g,), [_resident((1, D), lambda b, i: (0, 0))]
    return pl.pallas_call(
        body,
        grid=(B, S // tm),
        in_specs=[tok, tok] + extra_specs,
        out_specs=tok,
        out_shape=jax.ShapeDtypeStruct((B, S, D), F32),
        compiler_params=_cparams(("parallel", "parallel")),
        name="residual",
    )(x1, delta, *extra)


def _tile(n, pref):
    t = min(n, pref)
    assert n % t == 0, (n, t)
    return t


def _lambda_init(layer):
    return 0.8 - 0.6 * float(np.exp(-0.3 * layer))


def _lane_half_forms(w, n_heads, width):
    d = w.shape[0]
    w = w.reshape(d, n_heads, width)
    z = jnp.zeros_like(w)
    return jnp.concatenate([w, z, z, w], axis=-1).reshape(d, n_heads * 4 * width)


def kernel(x, mem, attn_norm_g, mem_norm_g, w_in, w_mem_kv, da_lambda_q1, da_lambda_k1, da_lambda_q2,
           da_lambda_k2, da_subln_g, wa_sink, w_da_o, w_wa_o, w_xa_o, w_out, ffn_norm_g, w_router,
           w_exp_gate, w_exp_up, w_exp_down, final_norm_g):
    B, S, D = x.shape
    depth = w_in.shape[0]
    cap = max(1, EC_FACTOR * S // N_EXPERTS)
    row = lambda v: v.reshape(1, -1).astype(F32)
    da_w = DA_HEADS * 2 * DA_HEAD_DIM
    wa_kv_w = WA_KV_HEADS * WA_HEAD_DIM
    o_waq = 3 * da_w
    o_wak = o_waq + WA_HEADS * WA_HEAD_DIM
    o_wav = o_wak + wa_kv_w
    o_xaq = o_wav + wa_kv_w
    o_gate = o_xaq + XA_HEADS * XA_HEAD_DIM
    for l in range(depth):
        lam_init = _lambda_init(l)
        wl = w_in[l]
        w1 = jnp.concatenate([wl[:, :o_wak],
                              _lane_half_forms(wl[:, o_wak:o_wav], WA_KV_HEADS, WA_HEAD_DIM),
                              wl[:, o_wav:o_gate]], axis=1).astype(BF16)
        wg = wl[:, o_gate:].astype(BF16)
        wr = jnp.pad(w_router[l], ((0, 0), (0, LANES - N_EXPERTS))).astype(BF16)

        daq, dak, dav, waq, wak, wavt, xaq = _in_proj(x, row(attn_norm_g[l]), w1, _tile(S, 512))
        memkv = _mem_proj(mem, row(mem_norm_g[l]), w_mem_kv[l].astype(BF16))
        lams = [row(v[l]) for v in (da_lambda_q1, da_lambda_k1, da_lambda_q2, da_lambda_k2)]
        oda = _diff_attn(daq, dak, dav, lams, da_subln_g[l].astype(F32), lam_init, _tile(S, 512))
        owa = _win_attn(waq, wak, wavt, wa_sink[l].astype(F32), _tile(S, 256))
        x1, h2, aff = _merge(x, oda, owa, xaq, memkv, row(attn_norm_g[l]), wg, w_da_o[l].astype(BF16),
                             w_wa_o[l].astype(BF16), w_xa_o[l].astype(BF16), w_out[l].astype(BF16),
                             row(ffn_norm_g[l]), wr, _tile(S, 512))
        pos = _route(aff, cap)
        delta = _moe(pos, aff, h2, w_exp_gate[l].astype(BF16), w_exp_up[l].astype(BF16),
                     w_exp_down[l].astype(BF16), cap, _tile(w_exp_gate.shape[-1], 1024))
        last = l == depth - 1
        x = _residual(x1, delta, row(final_norm_g) if last else None, _tile(S, 512))
    return x
```

```python
import functools

import numpy as np
import jax
import jax.numpy as jnp
from jax import lax
from jax.experimental import pallas as pl
from jax.experimental.pallas import tpu as pltpu

EPS = 1e-6
BLOCK = 128
WINDOW = 128
DA_HEADS = 8
DA_HEAD_DIM = 64
DA_V_DIM = 2 * DA_HEAD_DIM
WA_HEADS = 16
WA_KV_HEADS = 4
WA_GROUPS = WA_HEADS // WA_KV_HEADS
WA_HEAD_DIM = 64
XA_HEADS = 4
XA_HEAD_DIM = 256
N_BRANCH = 3
N_EXPERTS = 16
EC_FACTOR = 2

LOG2E = 1.4426950408889634
DA_Q_SCALE = DA_HEAD_DIM ** -0.5 * LOG2E
XA_Q_SCALE = XA_HEAD_DIM ** -0.5 * LOG2E
WA_Q_SCALE = WA_HEAD_DIM ** -0.5 * LOG2E
WA_ONES_ROWS = 16
DA_KEY_CHUNK = 256
DA_ONES_ROWS = 16

LANES = 128
VMEM_LIMIT = 56 * 1024 * 1024

F32 = jnp.float32
BF16 = jnp.bfloat16
NEG_INF = float("-inf")


def _cparams(sem, flags=None):
    return pltpu.CompilerParams(dimension_semantics=sem, vmem_limit_bytes=VMEM_LIMIT, flags=flags)


def _rms(xf, g_row):
    ms = jnp.mean(xf * xf, axis=-1, keepdims=True)
    return xf * lax.rsqrt(ms + EPS) * g_row


def _dot(a, b):
    return jnp.dot(a, b, preferred_element_type=F32)


def _dot_nt(a, b):
    return lax.dot_general(a, b, (((1,), (1,)), ((), ())), preferred_element_type=F32)


def _dot_tn(a, b):
    return lax.dot_general(a, b, (((0,), (0,)), ((), ())), preferred_element_type=F32)


def _resident(shape, index_map):
    return pl.BlockSpec(shape, index_map, pipeline_mode=pl.Buffered(1))


def _in_proj_kernel(x_ref, g_ref, w_ref, daq_ref, dak_ref, dav_ref):
    d = x_ref.shape[-1]
    h = _rms(x_ref[0], g_ref[...]).astype(BF16)
    for i, (ref, scale) in enumerate(((daq_ref, DA_Q_SCALE), (dak_ref, None), (dav_ref, None))):
        r = _dot(h, w_ref[:, i * d:(i + 1) * d])
        r = (r if scale is None else r * scale).astype(BF16)
        for hh in range(DA_HEADS):
            ref[0, hh] = r[:, hh * LANES:(hh + 1) * LANES]


def _in_proj(x, g, w1, tm):
    B, S, D = x.shape
    n_t = S // tm
    head_shape = jax.ShapeDtypeStruct((B, DA_HEADS, S, LANES), BF16)
    head_spec = pl.BlockSpec((1, DA_HEADS, tm, LANES), lambda b, i: (b, 0, i, 0))
    tok = pl.BlockSpec((1, tm, D), lambda b, i: (b, i, 0))
    return pl.pallas_call(
        _in_proj_kernel,
        grid=(B, n_t),
        in_specs=[tok, _resident((1, D), lambda b, i: (0, 0)), _resident(w1.shape, lambda b, i: (0, 0))],
        out_specs=[head_spec, head_spec, head_spec],
        out_shape=[head_shape, head_shape, head_shape],
        compiler_params=_cparams(("parallel", "parallel")),
        name="in_proj",
    )(x, g, w1)


def _mem_proj_kernel(m_ref, g_ref, w_ref, o_ref):
    h = _rms(m_ref[0], g_ref[...]).astype(BF16)
    o_ref[0] = _dot(h, w_ref[...]).astype(BF16)


def _mem_proj(mem, g, w):
    B, M, D = mem.shape
    N = w.shape[1]
    return pl.pallas_call(
        _mem_proj_kernel,
        grid=(B,),
        in_specs=[pl.BlockSpec((1, M, D), lambda b: (b, 0, 0)),
                  _resident((1, D), lambda b: (0, 0)),
                  _resident((D, N), lambda b: (0, 0))],
        out_specs=pl.BlockSpec((1, M, N), lambda b: (b, 0, 0)),
        out_shape=jax.ShapeDtypeStruct((B, M, N), BF16),
        compiler_params=_cparams(("parallel",)),
        name="mem_proj",
    )(mem, g, w)


def _side_projection(scale, x_ref, g1_ref, wb_ref, pb_ref, wavt_ref):
    tq = x_ref.shape[1]
    hb = _rms(x_ref[0], g1_ref[...]).astype(BF16)
    r = _dot(hb, wb_ref[0]) * scale
    pb_ref[0] = r.astype(BF16)
    vt = r[:, 0:WA_KV_HEADS * WA_HEAD_DIM].T
    ones = jnp.ones((WA_ONES_ROWS, LANES), BF16)
    for hh in range(WA_KV_HEADS):
        for kb in range(tq // LANES):
            wavt_ref[0, 0, hh, kb, 0:WA_HEAD_DIM, :] = vt[hh * WA_HEAD_DIM:(hh + 1) * WA_HEAD_DIM,
                                                          kb * LANES:(kb + 1) * LANES].astype(BF16)
            wavt_ref[0, 0, hh, kb, WA_HEAD_DIM:, :] = ones


def _diff_attn_kernel(scale_ref, lq1_ref, lk1_ref, lq2_ref, lk2_ref, q_ref, k_ref, v_ref, bias_ref, g_ref,
                      x_ref, g1_ref, wb_ref, o_ref, pb_ref, wavt_ref,
                      k1_scr, k2_scr, vt_scr, sa_scr, sb_scr, ma_scr, mb_scr, *, lam_init, n_t):
    S = k_ref.shape[2]
    tq = q_ref.shape[2]
    j = pl.program_id(1)
    n_tiles = pl.num_programs(1) - 1
    ck = DA_KEY_CHUNK
    seq, tile = _da_seq_tile(j, n_t, n_tiles)
    prev_seq, _ = _da_seq_tile(j - 1, n_t, n_tiles)

    @pl.when((tile == 0) & (j < n_tiles))
    def _():
        kk = k_ref[0, 0]
        lane = lax.broadcasted_iota(jnp.int32, kk.shape, 1)
        zero = jnp.zeros_like(kk)
        k1_scr[...] = jnp.where(lane < DA_HEAD_DIM, kk, zero)
        k2_scr[...] = jnp.where(lane >= DA_HEAD_DIM, kk, zero)
        vt_scr[seq % 2, 0:DA_V_DIM, :] = v_ref[0, 0].astype(F32).T.astype(BF16)
        vt_scr[seq % 2, DA_V_DIM:, :] = jnp.ones((DA_ONES_ROWS, S), BF16)

    @pl.when(j == 0)
    def _():
        sb_scr[...] = jnp.zeros_like(sb_scr)
        mb_scr[...] = jnp.zeros_like(mb_scr)

    lam = (jnp.exp(jnp.sum(lq1_ref[...] * lk1_ref[...], keepdims=True))
           - jnp.exp(jnp.sum(lq2_ref[...] * lk2_ref[...], keepdims=True)) + lam_init)

    q = q_ref[0, 0]
    r0 = (S - tq) - tile * tq
    vt_slot = prev_seq % 2

    def step(s_w, m_w, s_r, m_r):
        _side_projection(scale_ref[pl.program_id(0)], x_ref, g1_ref, wb_ref, pb_ref, wavt_ref)
        m_prev = (m_r[0:1, :], m_r[1:2, :])
        m_new = [None, None]
        acc = [None, None]
        for c in range(S // ck):
            rows = slice(c * ck, (c + 1) * ck)
            for i in range(2):
                p = jnp.exp2((s_r[i, rows, :] - m_prev[i]).astype(BF16))
                d = _dot(vt_scr[vt_slot, :, rows], p)
                acc[i] = d if acc[i] is None else acc[i] + d
            bias = bias_ref[0, pl.ds(pl.multiple_of(r0 + c * ck, ck), ck), :]
            for i, k_scr in enumerate((k1_scr, k2_scr)):
                s = _dot_nt(k_scr[rows, :], q) + bias
                s_w[i, rows, :] = s
                cm = jnp.max(s, axis=0, keepdims=True)
                m_new[i] = cm if m_new[i] is None else jnp.maximum(m_new[i], cm)
        m_w[0:1, :] = m_new[0]
        m_w[1:2, :] = m_new[1]
        o1, l1 = acc[0][0:DA_V_DIM, :], acc[0][DA_V_DIM:DA_V_DIM + 1, :]
        o2, l2 = acc[1][0:DA_V_DIM, :], acc[1][DA_V_DIM:DA_V_DIM + 1, :]
        ot = o1 * (1.0 / l1) - o2 * (lam / l2)
        ms = jnp.mean(ot * ot, axis=0, keepdims=True)
        y = ot * lax.rsqrt(ms + EPS) * g_ref[...] * (1.0 - lam_init)
        o_ref[0] = y.T.astype(BF16)

    @pl.when(j % 2 == 0)
    def _():
        step(sa_scr, ma_scr, sb_scr, mb_scr)

    @pl.when(j % 2 == 1)
    def _():
        step(sb_scr, mb_scr, sa_scr, ma_scr)


def _da_seq_tile(j, n_t, n_tiles):
    jc = jnp.clip(j, 0, n_tiles - 1)
    return jc // n_t, jc % n_t


def _diff_attn(daq, dak, dav, lams, subln_g, lam_init, x, g1, wb, wb_scale, tq):
    B, H, S, _ = daq.shape
    D = x.shape[-1]
    sec_w = wb.shape[-1]
    assert wb.shape[0] == H and sec_w >= WA_KV_HEADS * WA_HEAD_DIM
    vt_rows = WA_HEAD_DIM + WA_ONES_ROWS
    n_t = S // tq
    rows = 2 * S - tq
    slopes = jnp.exp2(-8.0 * jnp.arange(1, H + 1, dtype=F32) / H)
    r = jnp.arange(rows, dtype=jnp.int32)[:, None]
    c = jnp.arange(tq, dtype=jnp.int32)[None, :]
    dist = jnp.abs(r - (S - tq) - c).astype(F32)
    bias = -((slopes * LOG2E)[:, None, None] * dist[None])
    n_tiles = B * n_t
    lam_spec = _resident((1, DA_HEAD_DIM), lambda h, j: (0, 0))

    def kv_map(h, j):
        return (_da_seq_tile(j, n_t, n_tiles)[0], h, 0, 0)

    def q_map(h, j):
        seq, tile = _da_seq_tile(j, n_t, n_tiles)
        return (seq, h, tile, 0)

    def o_map(h, j):
        seq, tile = _da_seq_tile(j - 1, n_t, n_tiles)
        return (seq, tile, h)

    def x_map(h, j):
        seq, tile = _da_seq_tile(j, n_t, n_tiles)
        return (seq, tile, 0)

    def pb_map(h, j):
        seq, tile = _da_seq_tile(j, n_t, n_tiles)
        return (seq, tile, h)

    def wavt_map(h, j):
        seq, tile = _da_seq_tile(j, n_t, n_tiles)
        return (h, seq, 0, tile, 0, 0)

    kv_spec = pl.BlockSpec((1, 1, S, LANES), kv_map)
    return pl.pallas_call(
        functools.partial(_diff_attn_kernel, lam_init=lam_init, n_t=n_t),
        grid=(H, n_tiles + 1),
        in_specs=[pl.BlockSpec(memory_space=pltpu.SMEM),
                  lam_spec, lam_spec, lam_spec, lam_spec,
                  pl.BlockSpec((1, 1, tq, LANES), q_map),
                  kv_spec, kv_spec,
                  pl.BlockSpec((1, rows, tq), lambda h, j: (h, 0, 0)),
                  _resident((DA_V_DIM, 1), lambda h, j: (0, 0)),
                  pl.BlockSpec((1, tq, D), x_map),
                  _resident((1, D), lambda h, j: (0, 0)),
                  pl.BlockSpec((1, D, sec_w), lambda h, j: (h, 0, 0))],
        out_specs=[pl.BlockSpec((1, tq, LANES), o_map),
                   pl.BlockSpec((1, tq, sec_w), pb_map),
                   pl.BlockSpec((1, 1, WA_KV_HEADS, tq // LANES, vt_rows, LANES), wavt_map)],
        out_shape=[jax.ShapeDtypeStruct((B, S, H * LANES), BF16),
                   jax.ShapeDtypeStruct((B, S, H * sec_w), BF16),
                   jax.ShapeDtypeStruct((H, B, WA_KV_HEADS, S // LANES, vt_rows, LANES), BF16)],
        scratch_shapes=[pltpu.VMEM((S, LANES), BF16), pltpu.VMEM((S, LANES), BF16),
                        pltpu.VMEM((2, DA_V_DIM + DA_ONES_ROWS, S), BF16),
                        pltpu.VMEM((2, S, tq), F32), pltpu.VMEM((2, S, tq), F32),
                        pltpu.VMEM((2, tq), F32), pltpu.VMEM((2, tq), F32)],
        compiler_params=_cparams(("parallel", "arbitrary")),
        name="diff_attn",
    )(wb_scale, *lams, daq, dak, dav, bias, subln_g.reshape(DA_V_DIM, 1), x, g1, wb)


def _win_attn_kernel(sink_ref, q_ref, k_ref, vt_ref, tab_ref, o_ref, sa_scr, sb_scr, ma_scr, mb_scr, *, n_t):
    S = k_ref.shape[1]
    tq = q_ref.shape[1]
    span = tq + 2 * WINDOW
    g = pl.program_id(0)
    j = pl.program_id(1)
    n_tiles = pl.num_programs(1) - 1
    _, tile = _da_seq_tile(j, n_t, n_tiles)
    _, prev_tile = _da_seq_tile(j - 1, n_t, n_tiles)

    def window_start(t):
        return pl.multiple_of(jnp.clip(t * tq - WINDOW, 0, S - span), LANES)

    @pl.when(j == 0)
    def _():
        sb_scr[...] = jnp.zeros_like(sb_scr)
        mb_scr[...] = jnp.zeros_like(mb_scr)

    variant = jnp.where(tile == 0, 0, jnp.where(tile == n_t - 1, 2, 1))
    kwin = k_ref[0, pl.ds(window_start(tile), span), :]
    kb0 = window_start(prev_tile) // LANES

    def step(s_w, m_w, s_r, m_r):
        outs = []
        for jh in range(WA_GROUPS):
            sink = sink_ref[g * WA_GROUPS + jh] * LOG2E
            m = m_r[jh:jh + 1, :]
            acc = None
            for kb in range(span // LANES):
                p = jnp.exp2((s_r[jh, kb * LANES:(kb + 1) * LANES, :] - m).astype(BF16))
                d = _dot(vt_ref[0, 0, 0, kb0 + kb], p)
                acc = d if acc is None else acc + d
            l = acc[WA_HEAD_DIM:WA_HEAD_DIM + 1, :] + jnp.exp2(sink - m)
            outs.append(acc[0:WA_HEAD_DIM, :] * (1.0 / l))
            pair, half = divmod(jh, 2)
            qp = q_ref[0, :, pair * LANES:(pair + 1) * LANES]
            s = _dot_nt(kwin[:, half * LANES:(half + 1) * LANES], qp) + tab_ref[variant, jh]
            s_w[jh] = s
            m_w[jh:jh + 1, :] = jnp.maximum(jnp.max(s, axis=0, keepdims=True), sink)
        o_ref[0] = jnp.concatenate(outs, axis=0).T.astype(BF16)

    @pl.when(j % 2 == 0)
    def _():
        step(sa_scr, ma_scr, sb_scr, mb_scr)

    @pl.when(j % 2 == 1)
    def _():
        step(sb_scr, mb_scr, sa_scr, ma_scr)


def _win_attn(pb, q_col, k_col, wavt, v_section, sink, tq):
    B, S, _ = pb.shape
    D = WA_HEADS * WA_HEAD_DIM
    n_t = S // tq
    span = tq + 2 * WINDOW
    assert n_t >= 2 and S >= span, (S, tq)
    gw = WA_GROUPS * WA_HEAD_DIM
    assert q_col % gw == 0 and k_col % gw == 0
    q_blk, k_blk = q_col // gw, k_col // gw
    slopes = jnp.exp2(-8.0 * jnp.arange(1, WA_HEADS + 1, dtype=F32) / WA_HEADS) * LOG2E
    jr = jnp.arange(span, dtype=jnp.int32)[:, None]
    ir = jnp.arange(tq, dtype=jnp.int32)[None, :]
    shifts = jnp.array([0, WINDOW, 2 * WINDOW], jnp.int32)[:, None, None]
    dist = jnp.abs(shifts + ir[None] - jr[None])
    tab = jnp.where((dist <= WINDOW)[:, None], -(slopes[None, :, None, None] * dist.astype(F32)[:, None]), NEG_INF)
    vt_rows = wavt.shape[4]
    n_tiles = B * n_t

    def q_map(g, j):
        seq, tile = _da_seq_tile(j, n_t, n_tiles)
        return (seq, tile, q_blk + g)

    def o_map(g, j):
        seq, tile = _da_seq_tile(j - 1, n_t, n_tiles)
        return (seq, tile, g)

    return pl.pallas_call(
        functools.partial(_win_attn_kernel, n_t=n_t),
        grid=(WA_KV_HEADS, n_tiles + 1),
        in_specs=[pl.BlockSpec(memory_space=pltpu.SMEM),
                  pl.BlockSpec((1, tq, gw), q_map),
                  pl.BlockSpec((1, S, gw), lambda g, j: (_da_seq_tile(j, n_t, n_tiles)[0], 0, k_blk + g)),
                  pl.BlockSpec((1, 1, 1, S // LANES, vt_rows, LANES),
                               lambda g, j: (v_section, _da_seq_tile(j - 1, n_t, n_tiles)[0], g, 0, 0, 0)),
                  pl.BlockSpec((3, WA_GROUPS, span, tq), lambda g, j: (0, g, 0, 0))],
        out_specs=pl.BlockSpec((1, tq, gw), o_map),
        out_shape=jax.ShapeDtypeStruct((B, S, D), BF16),
        scratch_shapes=[pltpu.VMEM((WA_GROUPS, span, tq), F32), pltpu.VMEM((WA_GROUPS, span, tq), F32),
                        pltpu.VMEM((WA_GROUPS, tq), F32), pltpu.VMEM((WA_GROUPS, tq), F32)],
        compiler_params=_cparams(("parallel", "arbitrary")),
        name="win_attn",
    )(sink, pb, pb, wavt, tab)


def _mem_attn_proj(q_ref, kv_ref, w_ref):
    hd = XA_HEAD_DIM
    acc = None
    for hh in range(XA_HEADS):
        q = q_ref[0, :, hh * hd:(hh + 1) * hd]
        k = kv_ref[0, :, hh * hd:(hh + 1) * hd]
        v = kv_ref[0, :, (XA_HEADS + hh) * hd:(XA_HEADS + hh + 1) * hd]
        s = _dot_nt(q, k)
        p = jnp.exp2(s - jnp.max(s, axis=1, keepdims=True))
        pn = (p * (1.0 / jnp.sum(p, axis=1, keepdims=True))).astype(BF16)
        o = _dot(pn, v).astype(BF16)
        d = _dot(o, w_ref[hh * hd:(hh + 1) * hd, :])
        acc = d if acc is None else acc + d
    return acc


def _merge_kernel(x_ref, oda_ref, owa_ref, xaq_ref, memkv_ref, g1_ref, wg_ref, wda_ref, wwa_ref, wxa_ref,
                  wout_ref, g2_ref, wr_ref, x1_ref, h2_ref, aff_ref):
    d = x_ref.shape[-1]
    x = x_ref[0]
    h = _rms(x, g1_ref[...]).astype(BF16)
    branches = (_dot(oda_ref[0], wda_ref[...]), _dot(owa_ref[0], wwa_ref[...]),
                _mem_attn_proj(xaq_ref, memkv_ref, wxa_ref))
    merged = None
    for i, branch in enumerate(branches):
        term = jax.nn.sigmoid(_dot(h, wg_ref[:, i * d:(i + 1) * d])) * branch
        merged = term if merged is None else merged + term
    x1 = x + _dot(merged.astype(BF16), wout_ref[...])
    x1_ref[0] = x1
    h2 = _rms(x1, g2_ref[...]).astype(BF16)
    h2_ref[0] = h2
    logits = _dot(h2, wr_ref[...])
    lt = logits.T[:N_EXPERTS, :]
    m = jnp.max(lt, axis=0, keepdims=True)
    e = jnp.exp(lt - m)
    aff_ref[0] = e / jnp.sum(e, axis=0, keepdims=True)


def _merge(x, oda, owa, pb, xaq_col, memkv, g1, wg, wda, wwa, wxa, wout, g2, wr, tm):
    B, S, D = x.shape
    assert xaq_col % D == 0
    tok = pl.BlockSpec((1, tm, D), lambda b, i: (b, i, 0))
    xaq_spec = pl.BlockSpec((1, tm, D), lambda b, i: (b, i, xaq_col // D))
    full = lambda a: _resident(a.shape, lambda b, i: (0,) * a.ndim)
    return pl.pallas_call(
        _merge_kernel,
        grid=(B, S // tm),
        in_specs=[tok, tok, tok, xaq_spec, pl.BlockSpec((1,) + memkv.shape[1:], lambda b, i: (b, 0, 0)),
                  full(g1), full(wg), full(wda), full(wwa), full(wxa), full(wout), full(g2), full(wr)],
        out_specs=[tok, tok, pl.BlockSpec((1, N_EXPERTS, tm), lambda b, i: (b, 0, i))],
        out_shape=[jax.ShapeDtypeStruct((B, S, D), F32),
                   jax.ShapeDtypeStruct((B, S, D), BF16),
                   jax.ShapeDtypeStruct((B, N_EXPERTS, S), F32)],
        compiler_params=_cparams(("parallel", "parallel")),
        name="merge",
    )(x, oda, owa, pb, memkv, g1, wg, wda, wwa, wxa, wout, g2, wr)


def _route_kernel(aff_ref, pos_ref, *, cap):
    aff = aff_ref[0]
    E, S = aff.shape
    bits = pltpu.bitcast(aff, jnp.int32)
    prefix = jnp.zeros((E, 1), jnp.int32)
    for bit in range(30, -1, -1):
        cand = prefix | (1 << bit)
        cnt = jnp.sum(jnp.where(bits >= cand, 1.0, 0.0), axis=1, keepdims=True)
        prefix = jnp.where(cnt >= cap, cand, prefix)
    gt = bits > prefix
    eq = bits == prefix
    blk = 2 * LANES
    tri = (lax.broadcasted_iota(jnp.int32, (blk, blk), 0) <= lax.broadcasted_iota(jnp.int32, (blk, blk), 1))
    tri = jnp.where(tri, 1.0, 0.0).astype(BF16)

    def cumsum(mask):
        mb = jnp.where(mask, 1.0, 0.0).astype(BF16)
        carry = jnp.zeros((E, 1), F32)
        outs = []
        for i in range(S // blk):
            cs = _dot(mb[:, i * blk:(i + 1) * blk], tri) + carry
            outs.append(cs)
            carry = cs[:, blk - 1:blk]
        return jnp.concatenate(outs, axis=1), carry

    cs_gt, n_gt = cumsum(gt)
    cs_eq, _ = cumsum(eq)
    need = cap - n_gt
    sel = gt | (eq & (cs_eq <= need))
    slot = cs_gt + jnp.minimum(cs_eq, need) - 1.0
    pos_ref[0] = jnp.where(sel, slot, -1.0).astype(jnp.int32)


def _route(aff, cap):
    B, E, S = aff.shape
    return pl.pallas_call(
        functools.partial(_route_kernel, cap=cap),
        grid=(B,),
        in_specs=[pl.BlockSpec((1, E, S), lambda b: (b, 0, 0))],
        out_specs=pl.BlockSpec((1, E, S), lambda b: (b, 0, 0)),
        out_shape=jax.ShapeDtypeStruct((B, E, S), jnp.int32),
        compiler_params=_cparams(("parallel",)),
        name="route",
    )(aff)


def _moe_kernel(pos_ref, aff_ref, h2_ref, wg_ref, wu_ref, wd_ref, o_ref, p_scr, g_scr, xe_scr, y_scr, *, cap):
    e = pl.program_id(1)
    f = pl.program_id(2)
    nf = pl.num_programs(2)
    S = h2_ref.shape[1]

    @pl.when((e == 0) & (f == 0))
    def _():
        o_ref[...] = jnp.zeros_like(o_ref)

    @pl.when(f == 0)
    def _():
        slot = lax.broadcasted_iota(jnp.int32, (cap, S), 0)
        hit = pos_ref[0, 0] == slot
        onehot = jnp.where(hit, 1.0, 0.0).astype(BF16)
        p_scr[...] = onehot
        g_scr[...] = jnp.sum(jnp.where(hit, aff_ref[0, 0], 0.0), axis=1, keepdims=True)
        xe_scr[...] = _dot(onehot, h2_ref[0]).astype(BF16)
        y_scr[...] = jnp.zeros_like(y_scr)

    xe = xe_scr[...]
    a = _dot(xe, wg_ref[0])
    u = _dot(xe, wu_ref[0])
    act = (a * jax.nn.sigmoid(a) * u).astype(BF16)
    y_scr[...] += _dot(act, wd_ref[0])

    @pl.when(f == nf - 1)
    def _():
        ye = (y_scr[...] * g_scr[...]).astype(BF16)
        o_ref[0] += _dot_tn(p_scr[...], ye)


def _moe(pos, aff, h2, wg, wu, wd, cap, tf):
    B, S, D = h2.shape
    E, _, F = wg.shape
    pos4 = pos.reshape(B, E, 1, S)
    aff4 = aff.reshape(B, E, 1, S)
    row_spec = pl.BlockSpec((1, 1, 1, S), lambda b, e, f: (b, e, 0, 0))
    return pl.pallas_call(
        functools.partial(_moe_kernel, cap=cap),
        grid=(B, E, F // tf),
        in_specs=[row_spec, row_spec,
                  pl.BlockSpec((1, S, D), lambda b, e, f: (b, 0, 0)),
                  pl.BlockSpec((1, D, tf), lambda b, e, f: (e, 0, f)),
                  pl.BlockSpec((1, D, tf), lambda b, e, f: (e, 0, f)),
                  pl.BlockSpec((1, tf, D), lambda b, e, f: (e, f, 0))],
        out_specs=pl.BlockSpec((1, S, D), lambda b, e, f: (b, 0, 0)),
        out_shape=jax.ShapeDtypeStruct((B, S, D), F32),
        scratch_shapes=[pltpu.VMEM((cap, S), BF16), pltpu.VMEM((cap, 1), F32),
                        pltpu.VMEM((cap, D), BF16), pltpu.VMEM((cap, D), F32)],
        compiler_params=_cparams(("parallel", "arbitrary", "arbitrary")),
        name="moe",
    )(pos4, aff4, h2, wg, wu, wd)


def _residual_kernel(x_ref, d_ref, o_ref):
    o_ref[0] = x_ref[0] + d_ref[0]


def _residual_norm_kernel(x_ref, d_ref, g_ref, o_ref):
    o_ref[0] = _rms(x_ref[0] + d_ref[0], g_ref[...])


def _residual(x1, delta, g, tm):
    B, S, D = x1.shape
    tok = pl.BlockSpec((1, tm, D), lambda b, i: (b, i, 0))
    if g is None:
        body, extra, extra_specs = _residual_kernel, (), []
    else:
        body, extra, extra_specs = _residual_norm_kernel, (g,), [_resident((1, D), lambda b, i: (0, 0))]
    return pl.pallas_call(
        body,
        grid=(B, S // tm),
        in_specs=[tok, tok] + extra_specs,
        out_specs=tok,
        out_shape=jax.ShapeDtypeStruct((B, S, D), F32),
        compiler_params=_cparams(("parallel", "parallel")),
        name="residual",
    )(x1, delta, *extra)


def _tile(n, pref):
    t = min(n, pref)
    assert n % t == 0, (n, t)
    return t


def _lambda_init(layer):
    return 0.8 - 0.6 * float(np.exp(-0.3 * layer))


def _lane_half_forms(w, n_heads, width):
    d = w.shape[0]
    w = w.reshape(d, n_heads, width)
    z = jnp.zeros_like(w)
    return jnp.concatenate([w, z, z, w], axis=-1).reshape(d, n_heads * 4 * width)


def kernel(x, mem, attn_norm_g, mem_norm_g, w_in, w_mem_kv, da_lambda_q1, da_lambda_k1, da_lambda_q2,
           da_lambda_k2, da_subln_g, wa_sink, w_da_o, w_wa_o, w_xa_o, w_out, ffn_norm_g, w_router,
           w_exp_gate, w_exp_up, w_exp_down, final_norm_g):
    B, S, D = x.shape
    depth = w_in.shape[0]
    cap = max(1, EC_FACTOR * S // N_EXPERTS)
    row = lambda v: v.reshape(1, -1).astype(F32)
    da_w = DA_HEADS * 2 * DA_HEAD_DIM
    wa_kv_w = WA_KV_HEADS * WA_HEAD_DIM
    o_waq = 3 * da_w
    o_wak = o_waq + WA_HEADS * WA_HEAD_DIM
    o_wav = o_wak + wa_kv_w
    o_xaq = o_wav + wa_kv_w
    o_gate = o_xaq + XA_HEADS * XA_HEAD_DIM
    for l in range(depth):
        lam_init = _lambda_init(l)
        wl = w_in[l]
        w1 = wl[:, :o_waq].astype(BF16)
        sec_w = (3 * D) // (DA_HEADS - 2)
        wav_pad = jnp.pad(wl[:, o_wav:o_xaq], ((0, 0), (0, sec_w - wa_kv_w)))
        wb = jnp.concatenate([wl[:, o_waq:o_wak],
                              _lane_half_forms(wl[:, o_wak:o_wav], WA_KV_HEADS, WA_HEAD_DIM),
                              wl[:, o_xaq:o_gate], wav_pad, jnp.zeros((D, sec_w), wl.dtype)], axis=1)
        assert wb.shape[1] == DA_HEADS * sec_w
        wb = wb.reshape(D, DA_HEADS, sec_w).transpose(1, 0, 2).astype(BF16)
        per_d = D // sec_w
        wb_scale = jnp.array([WA_Q_SCALE] * per_d + [1.0] * per_d + [XA_Q_SCALE] * per_d + [1.0, 1.0], F32)
        waq_col, wak_col, xaq_col, wav_section = 0, D, 2 * D, 3 * per_d
        wg = wl[:, o_gate:].astype(BF16)
        wr = jnp.pad(w_router[l], ((0, 0), (0, LANES - N_EXPERTS))).astype(BF16)

        daq, dak, dav = _in_proj(x, row(attn_norm_g[l]), w1, _tile(S, 512))
        memkv = _mem_proj(mem, row(mem_norm_g[l]), w_mem_kv[l].astype(BF16))
        lams = [row(v[l]) for v in (da_lambda_q1, da_lambda_k1, da_lambda_q2, da_lambda_k2)]
        oda, pb, wavt = _diff_attn(daq, dak, dav, lams, da_subln_g[l].astype(F32), lam_init,
                                   x, row(attn_norm_g[l]), wb, wb_scale, _tile(S, 512))
        owa = _win_attn(pb, waq_col, wak_col, wavt, wav_section, wa_sink[l].astype(F32), _tile(S, 256))
        x1, h2, aff = _merge(x, oda, owa, pb, xaq_col, memkv, row(attn_norm_g[l]), wg, w_da_o[l].astype(BF16),
                             w_wa_o[l].astype(BF16), w_xa_o[l].astype(BF16), w_out[l].astype(BF16),
                             row(ffn_norm_g[l]), wr, _tile(S, 512))
        pos = _route(aff, cap)
        delta = _moe(pos, aff, h2, w_exp_gate[l].astype(BF16), w_exp_up[l].astype(BF16),
                     w_exp_down[l].astype(BF16), cap, _tile(w_exp_gate.shape[-1], 1024))
        last = l == depth - 1
        x = _residual(x1, delta, row(final_norm_g) if last else None, _tile(S, 512))
    return x
```

```python
import functools

import numpy as np
import jax
import jax.numpy as jnp
from jax import lax
from jax.experimental import pallas as pl
from jax.experimental.pallas import tpu as pltpu

EPS = 1e-6
BLOCK = 128
WINDOW = 128
DA_HEADS = 8
DA_HEAD_DIM = 64
DA_V_DIM = 2 * DA_HEAD_DIM
WA_HEADS = 16
WA_KV_HEADS = 4
WA_GROUPS = WA_HEADS // WA_KV_HEADS
WA_HEAD_DIM = 64
XA_HEADS = 4
XA_HEAD_DIM = 256
N_BRANCH = 3
N_EXPERTS = 16
EC_FACTOR = 2

LOG2E = 1.4426950408889634
DA_Q_SCALE = DA_HEAD_DIM ** -0.5 * LOG2E
XA_Q_SCALE = XA_HEAD_DIM ** -0.5 * LOG2E
WA_Q_SCALE = WA_HEAD_DIM ** -0.5 * LOG2E
WA_ONES_ROWS = 16
WA_KV_PER_STEP = 2
DA_KEY_CHUNK = 256
DA_ONES_ROWS = 16
DA_DRAIN_STEPS = 2

LANES = 128
VMEM_LIMIT = 56 * 1024 * 1024

F32 = jnp.float32
BF16 = jnp.bfloat16
NEG_INF = float("-inf")


def _cparams(sem, flags=None):
    return pltpu.CompilerParams(dimension_semantics=sem, vmem_limit_bytes=VMEM_LIMIT, flags=flags)


def _rms(xf, g_row):
    ms = jnp.mean(xf * xf, axis=-1, keepdims=True)
    return xf * lax.rsqrt(ms + EPS) * g_row


def _dot(a, b):
    return jnp.dot(a, b, preferred_element_type=F32)


def _dot_nt(a, b):
    return lax.dot_general(a, b, (((1,), (1,)), ((), ())), preferred_element_type=F32)


def _dot_tn(a, b):
    return lax.dot_general(a, b, (((0,), (0,)), ((), ())), preferred_element_type=F32)


def _resident(shape, index_map):
    return pl.BlockSpec(shape, index_map, pipeline_mode=pl.Buffered(1))


def _in_proj_kernel(x_ref, g_ref, w_ref, daq_ref, dak_ref, dav_ref, waq_ref, wak_ref, wavt_ref, xaq_ref):
    d = x_ref.shape[-1]
    tm = x_ref.shape[1]
    h = _rms(x_ref[0], g_ref[...]).astype(BF16)
    col = [0]

    def sec(width):
        r = _dot(h, w_ref[:, col[0]:col[0] + width])
        col[0] += width
        return r

    for ref, scale in ((daq_ref, DA_Q_SCALE), (dak_ref, None), (dav_ref, None)):
        r = sec(d)
        r = (r if scale is None else r * scale).astype(BF16)
        for hh in range(DA_HEADS):
            ref[0, hh] = r[:, hh * LANES:(hh + 1) * LANES]
    waq_ref[0] = (sec(d) * WA_Q_SCALE).astype(BF16)
    wak_ref[0] = sec(d).astype(BF16)
    vt = sec(WA_KV_HEADS * WA_HEAD_DIM).T
    ones = jnp.ones((WA_ONES_ROWS, LANES), BF16)
    for hh in range(WA_KV_HEADS):
        for kb in range(tm // LANES):
            wavt_ref[0, hh, kb, 0:WA_HEAD_DIM, :] = vt[hh * WA_HEAD_DIM:(hh + 1) * WA_HEAD_DIM,
                                                       kb * LANES:(kb + 1) * LANES].astype(BF16)
            wavt_ref[0, hh, kb, WA_HEAD_DIM:, :] = ones
    xaq_ref[0] = (sec(d) * XA_Q_SCALE).astype(BF16)


def _in_proj(x, g, w1, tm):
    B, S, D = x.shape
    n_t = S // tm
    head_shape = jax.ShapeDtypeStruct((B, DA_HEADS, S, LANES), BF16)
    head_spec = pl.BlockSpec((1, DA_HEADS, tm, LANES), lambda b, i: (b, 0, i, 0))
    tok = pl.BlockSpec((1, tm, D), lambda b, i: (b, i, 0))
    tok_shape = jax.ShapeDtypeStruct((B, S, D), BF16)
    vt_rows = WA_HEAD_DIM + WA_ONES_ROWS
    return pl.pallas_call(
        _in_proj_kernel,
        grid=(B, n_t),
        in_specs=[tok, _resident((1, D), lambda b, i: (0, 0)), _resident(w1.shape, lambda b, i: (0, 0))],
        out_specs=[head_spec, head_spec, head_spec, tok, tok,
                   pl.BlockSpec((1, WA_KV_HEADS, tm // LANES, vt_rows, LANES), lambda b, i: (b, 0, i, 0, 0)),
                   tok],
        out_shape=[head_shape, head_shape, head_shape, tok_shape, tok_shape,
                   jax.ShapeDtypeStruct((B, WA_KV_HEADS, S // LANES, vt_rows, LANES), BF16),
                   tok_shape],
        compiler_params=_cparams(("parallel", "parallel")),
        name="in_proj",
    )(x, g, w1)


def _mem_proj_kernel(m_ref, g_ref, w_ref, o_ref):
    h = _rms(m_ref[0], g_ref[...]).astype(BF16)
    o_ref[0] = _dot(h, w_ref[...]).astype(BF16)


def _mem_proj(mem, g, w):
    B, M, D = mem.shape
    N = w.shape[1]
    return pl.pallas_call(
        _mem_proj_kernel,
        grid=(B,),
        in_specs=[pl.BlockSpec((1, M, D), lambda b: (b, 0, 0)),
                  _resident((1, D), lambda b: (0, 0)),
                  _resident((D, N), lambda b: (0, 0))],
        out_specs=pl.BlockSpec((1, M, N), lambda b: (b, 0, 0)),
        out_shape=jax.ShapeDtypeStruct((B, M, N), BF16),
        compiler_params=_cparams(("parallel",)),
        name="mem_proj",
    )(mem, g, w)


def _diff_attn_kernel(lq1_ref, lk1_ref, lq2_ref, lk2_ref, q_ref, k_ref, v_ref, bias_ref, g_ref, o_ref,
                      k1_scr, k2_scr, vt_scr, sa_scr, sb_scr, ma_scr, mb_scr, ea_scr, eb_scr, *, lam_init, n_t):
    S = k_ref.shape[2]
    tq = q_ref.shape[2]
    j = pl.program_id(1)
    n_tiles = pl.num_programs(1) - DA_DRAIN_STEPS
    ck = DA_KEY_CHUNK
    seq, tile = _da_seq_tile(j, n_t, n_tiles)
    prev_seq, _ = _da_seq_tile(j - 1, n_t, n_tiles)

    @pl.when((tile == 0) & (j < n_tiles))
    def _():
        kk = k_ref[0, 0]
        lane = lax.broadcasted_iota(jnp.int32, kk.shape, 1)
        zero = jnp.zeros_like(kk)
        k1_scr[...] = jnp.where(lane < DA_HEAD_DIM, kk, zero)
        k2_scr[...] = jnp.where(lane >= DA_HEAD_DIM, kk, zero)
        vt_scr[seq % 2, 0:DA_V_DIM, :] = v_ref[0, 0].astype(F32).T.astype(BF16)
        vt_scr[seq % 2, DA_V_DIM:, :] = jnp.ones((DA_ONES_ROWS, S), BF16)

    @pl.when(j == 0)
    def _():
        sb_scr[...] = jnp.zeros_like(sb_scr)
        mb_scr[...] = jnp.zeros_like(mb_scr)
        eb_scr[...] = jnp.ones_like(eb_scr)

    lam = (jnp.exp(jnp.sum(lq1_ref[...] * lk1_ref[...], keepdims=True))
           - jnp.exp(jnp.sum(lq2_ref[...] * lk2_ref[...], keepdims=True)) + lam_init)

    q = q_ref[0, 0]
    r0 = (S - tq) - tile * tq
    vt_slot = prev_seq % 2

    def step(s_w, m_w, e_w, s_r, m_r, e_r):
        o1, l1 = e_r[0, 0:DA_V_DIM, :], e_r[0, DA_V_DIM:DA_V_DIM + 1, :]
        o2, l2 = e_r[1, 0:DA_V_DIM, :], e_r[1, DA_V_DIM:DA_V_DIM + 1, :]
        ot = o1 * (1.0 / l1) - o2 * (lam / l2)
        ms = jnp.mean(ot * ot, axis=0, keepdims=True)
        y = ot * lax.rsqrt(ms + EPS) * g_ref[...] * (1.0 - lam_init)
        o_ref[0] = y.T.astype(BF16)
        m_prev = (m_r[0:1, :], m_r[1:2, :])
        m_new = [None, None]
        acc = [None, None]
        for c in range(S // ck):
            rows = slice(c * ck, (c + 1) * ck)
            for i in range(2):
                p = jnp.exp2((s_r[i, rows, :] - m_prev[i]).astype(BF16))
                d = _dot(vt_scr[vt_slot, :, rows], p)
                acc[i] = d if acc[i] is None else acc[i] + d
            bias = bias_ref[0, pl.ds(pl.multiple_of(r0 + c * ck, ck), ck), :]
            for i, k_scr in enumerate((k1_scr, k2_scr)):
                s = _dot_nt(k_scr[rows, :], q) + bias
                s_w[i, rows, :] = s
                cm = jnp.max(s, axis=0, keepdims=True)
                m_new[i] = cm if m_new[i] is None else jnp.maximum(m_new[i], cm)
        m_w[0:1, :] = m_new[0]
        m_w[1:2, :] = m_new[1]
        e_w[0] = acc[0]
        e_w[1] = acc[1]

    @pl.when(j % 2 == 0)
    def _():
        step(sa_scr, ma_scr, ea_scr, sb_scr, mb_scr, eb_scr)

    @pl.when(j % 2 == 1)
    def _():
        step(sb_scr, mb_scr, eb_scr, sa_scr, ma_scr, ea_scr)


def _da_seq_tile(j, n_t, n_tiles):
    jc = jnp.clip(j, 0, n_tiles - 1)
    return jc // n_t, jc % n_t


def _diff_attn(daq, dak, dav, lams, subln_g, lam_init, tq):
    B, H, S, _ = daq.shape
    n_t = S // tq
    rows = 2 * S - tq
    slopes = jnp.exp2(-8.0 * jnp.arange(1, H + 1, dtype=F32) / H)
    r = jnp.arange(rows, dtype=jnp.int32)[:, None]
    c = jnp.arange(tq, dtype=jnp.int32)[None, :]
    dist = jnp.abs(r - (S - tq) - c).astype(F32)
    bias = -((slopes * LOG2E)[:, None, None] * dist[None])
    n_tiles = B * n_t
    lam_spec = _resident((1, DA_HEAD_DIM), lambda h, j: (0, 0))

    def kv_map(h, j):
        return (_da_seq_tile(j, n_t, n_tiles)[0], h, 0, 0)

    def q_map(h, j):
        seq, tile = _da_seq_tile(j, n_t, n_tiles)
        return (seq, h, tile, 0)

    def o_map(h, j):
        seq, tile = _da_seq_tile(j - DA_DRAIN_STEPS, n_t, n_tiles)
        return (seq, tile, h)

    kv_spec = pl.BlockSpec((1, 1, S, LANES), kv_map)
    acc_shape = pltpu.VMEM((2, DA_V_DIM + DA_ONES_ROWS, tq), F32)
    return pl.pallas_call(
        functools.partial(_diff_attn_kernel, lam_init=lam_init, n_t=n_t),
        grid=(H, n_tiles + DA_DRAIN_STEPS),
        in_specs=[lam_spec, lam_spec, lam_spec, lam_spec,
                  pl.BlockSpec((1, 1, tq, LANES), q_map),
                  kv_spec, kv_spec,
                  pl.BlockSpec((1, rows, tq), lambda h, j: (h, 0, 0)),
                  _resident((DA_V_DIM, 1), lambda h, j: (0, 0))],
        out_specs=pl.BlockSpec((1, tq, LANES), o_map),
        out_shape=jax.ShapeDtypeStruct((B, S, H * LANES), BF16),
        scratch_shapes=[pltpu.VMEM((S, LANES), BF16), pltpu.VMEM((S, LANES), BF16),
                        pltpu.VMEM((2, DA_V_DIM + DA_ONES_ROWS, S), BF16),
                        pltpu.VMEM((2, S, tq), F32), pltpu.VMEM((2, S, tq), F32),
                        pltpu.VMEM((2, tq), F32), pltpu.VMEM((2, tq), F32), acc_shape, acc_shape],
        compiler_params=_cparams(("parallel", "arbitrary")),
        name="diff_attn",
    )(*lams, daq, dak, dav, bias, subln_g.reshape(DA_V_DIM, 1))


def _win_attn_kernel(sink_ref, q_ref, k_ref, vt_ref, tab_ref, o_ref, sa_scr, sb_scr, ma_scr, mb_scr, *, n_t):
    S = k_ref.shape[1]
    tq = q_ref.shape[1]
    span = tq + 2 * WINDOW
    g = pl.program_id(0)
    j = pl.program_id(1)
    n_tiles = pl.num_programs(1) - 1
    _, tile = _da_seq_tile(j, n_t, n_tiles)
    _, prev_tile = _da_seq_tile(j - 1, n_t, n_tiles)

    def window_start(t):
        return pl.multiple_of(jnp.clip(t * tq - WINDOW, 0, S - span), LANES)

    @pl.when(j == 0)
    def _():
        sb_scr[...] = jnp.zeros_like(sb_scr)
        mb_scr[...] = jnp.zeros_like(mb_scr)

    shift = jnp.where(tile == 0, 0, jnp.where(tile == n_t - 1, 2 * WINDOW, WINDOW))
    tab_row = pl.multiple_of(2 * WINDOW - shift, LANES)
    kwin = k_ref[0, pl.ds(window_start(tile), span), :]
    kb0 = window_start(prev_tile) // LANES
    heads = WA_KV_PER_STEP * WA_GROUPS

    def step(s_w, m_w, s_r, m_r):
        outs = []
        for jh in range(heads):
            kv, pair_half = divmod(jh, WA_GROUPS)
            pair, half = divmod(pair_half, 2)
            sink = sink_ref[g * heads + jh] * LOG2E
            m = m_r[jh:jh + 1, :]
            acc = None
            for kb in range(span // LANES):
                p = jnp.exp2((s_r[jh, kb * LANES:(kb + 1) * LANES, :] - m).astype(BF16))
                d = _dot(vt_ref[0, kv, kb0 + kb], p)
                acc = d if acc is None else acc + d
            l = acc[WA_HEAD_DIM:WA_HEAD_DIM + 1, :] + jnp.exp2(sink - m)
            outs.append(acc[0:WA_HEAD_DIM, :] * (1.0 / l))
            qcol = (kv * 2 + pair) * LANES
            kcol = (kv * 2 + half) * LANES
            qp = q_ref[0, :, qcol:qcol + LANES]
            s = _dot_nt(kwin[:, kcol:kcol + LANES], qp) + tab_ref[jh, pl.ds(tab_row, span), :]
            s_w[jh] = s
            m_w[jh:jh + 1, :] = jnp.maximum(jnp.max(s, axis=0, keepdims=True), sink)
        o_ref[0] = jnp.concatenate(outs, axis=0).T.astype(BF16)

    @pl.when(j % 2 == 0)
    def _():
        step(sa_scr, ma_scr, sb_scr, mb_scr)

    @pl.when(j % 2 == 1)
    def _():
        step(sb_scr, mb_scr, sa_scr, ma_scr)


def _win_attn(waq, wak, wavt, sink, tq):
    B, S, D = waq.shape
    n_t = S // tq
    span = tq + 2 * WINDOW
    assert n_t >= 2 and S >= span, (S, tq)
    heads = WA_KV_PER_STEP * WA_GROUPS
    gw = heads * WA_HEAD_DIM
    slopes = jnp.exp2(-8.0 * jnp.arange(1, WA_HEADS + 1, dtype=F32) / WA_HEADS) * LOG2E
    r = jnp.arange(span + 2 * WINDOW, dtype=jnp.int32)[:, None]
    ir = jnp.arange(tq, dtype=jnp.int32)[None, :]
    dist = jnp.abs(2 * WINDOW + ir - r)
    tab = jnp.where((dist <= WINDOW)[None], -(slopes[:, None, None] * dist.astype(F32)[None]), NEG_INF)
    vt_rows = wavt.shape[3]
    n_tiles = B * n_t

    def q_map(g, j):
        seq, tile = _da_seq_tile(j, n_t, n_tiles)
        return (seq, tile, g)

    def o_map(g, j):
        seq, tile = _da_seq_tile(j - 1, n_t, n_tiles)
        return (seq, tile, g)

    return pl.pallas_call(
        functools.partial(_win_attn_kernel, n_t=n_t),
        grid=(WA_KV_HEADS // WA_KV_PER_STEP, n_tiles + 1),
        in_specs=[pl.BlockSpec(memory_space=pltpu.SMEM),
                  pl.BlockSpec((1, tq, gw), q_map),
                  pl.BlockSpec((1, S, gw), lambda g, j: (_da_seq_tile(j, n_t, n_tiles)[0], 0, g)),
                  pl.BlockSpec((1, WA_KV_PER_STEP, S // LANES, vt_rows, LANES),
                               lambda g, j: (_da_seq_tile(j - 1, n_t, n_tiles)[0], g, 0, 0, 0)),
                  pl.BlockSpec((heads, span + 2 * WINDOW, tq), lambda g, j: (g, 0, 0))],
        out_specs=pl.BlockSpec((1, tq, gw), o_map),
        out_shape=jax.ShapeDtypeStruct((B, S, D), BF16),
        scratch_shapes=[pltpu.VMEM((heads, span, tq), F32), pltpu.VMEM((heads, span, tq), F32),
                        pltpu.VMEM((heads, tq), F32), pltpu.VMEM((heads, tq), F32)],
        compiler_params=_cparams(("parallel", "arbitrary")),
        name="win_attn",
    )(sink, waq, wak, wavt, tab)


def _mem_attn_proj(q_ref, kv_ref, w_ref):
    hd = XA_HEAD_DIM
    acc = None
    for hh in range(XA_HEADS):
        q = q_ref[0, :, hh * hd:(hh + 1) * hd]
        k = kv_ref[0, :, hh * hd:(hh + 1) * hd]
        v = kv_ref[0, :, (XA_HEADS + hh) * hd:(XA_HEADS + hh + 1) * hd]
        s = _dot_nt(q, k)
        p = jnp.exp2(s - jnp.max(s, axis=1, keepdims=True))
        pn = (p * (1.0 / jnp.sum(p, axis=1, keepdims=True))).astype(BF16)
        o = _dot(pn, v).astype(BF16)
        d = _dot(o, w_ref[hh * hd:(hh + 1) * hd, :])
        acc = d if acc is None else acc + d
    return acc


def _merge_kernel(x_ref, oda_ref, owa_ref, xaq_ref, memkv_ref, g1_ref, wg_ref, wda_ref, wwa_ref, wxa_ref,
                  wout_ref, g2_ref, wr_ref, x1_ref, h2_ref, aff_ref):
    d = x_ref.shape[-1]
    x = x_ref[0]
    h = _rms(x, g1_ref[...]).astype(BF16)
    branches = (_dot(oda_ref[0], wda_ref[...]), _dot(owa_ref[0], wwa_ref[...]),
                _mem_attn_proj(xaq_ref, memkv_ref, wxa_ref))
    merged = None
    for i, branch in enumerate(branches):
        term = jax.nn.sigmoid(_dot(h, wg_ref[:, i * d:(i + 1) * d])) * branch
        merged = term if merged is None else merged + term
    x1 = x + _dot(merged.astype(BF16), wout_ref[...])
    x1_ref[0] = x1
    h2 = _rms(x1, g2_ref[...]).astype(BF16)
    h2_ref[0] = h2
    logits = _dot(h2, wr_ref[...])
    lt = logits.T[:N_EXPERTS, :]
    m = jnp.max(lt, axis=0, keepdims=True)
    e = jnp.exp(lt - m)
    aff_ref[0] = e / jnp.sum(e, axis=0, keepdims=True)


def _merge(x, oda, owa, xaq, memkv, g1, wg, wda, wwa, wxa, wout, g2, wr, tm):
    B, S, D = x.shape
    tok = pl.BlockSpec((1, tm, D), lambda b, i: (b, i, 0))
    full = lambda a: _resident(a.shape, lambda b, i: (0,) * a.ndim)
    return pl.pallas_call(
        _merge_kernel,
        grid=(B, S // tm),
        in_specs=[tok, tok, tok, tok, pl.BlockSpec((1,) + memkv.shape[1:], lambda b, i: (b, 0, 0)),
                  full(g1), full(wg), full(wda), full(wwa), full(wxa), full(wout), full(g2), full(wr)],
        out_specs=[tok, tok, pl.BlockSpec((1, N_EXPERTS, tm), lambda b, i: (b, 0, i))],
        out_shape=[jax.ShapeDtypeStruct((B, S, D), F32),
                   jax.ShapeDtypeStruct((B, S, D), BF16),
                   jax.ShapeDtypeStruct((B, N_EXPERTS, S), F32)],
        compiler_params=_cparams(("parallel", "parallel")),
        name="merge",
    )(x, oda, owa, xaq, memkv, g1, wg, wda, wwa, wxa, wout, g2, wr)


def _route_kernel(aff_ref, pos_ref, *, cap):
    aff = aff_ref[0]
    E, S = aff.shape
    bits = pltpu.bitcast(aff, jnp.int32)
    prefix = jnp.zeros((E, 1), jnp.int32)
    for bit in range(30, -1, -1):
        cand = prefix | (1 << bit)
        cnt = jnp.sum(jnp.where(bits >= cand, 1.0, 0.0), axis=1, keepdims=True)
        prefix = jnp.where(cnt >= cap, cand, prefix)
    gt = bits > prefix
    eq = bits == prefix
    blk = 2 * LANES
    tri = (lax.broadcasted_iota(jnp.int32, (blk, blk), 0) <= lax.broadcasted_iota(jnp.int32, (blk, blk), 1))
    tri = jnp.where(tri, 1.0, 0.0).astype(BF16)

    def cumsum(mask):
        mb = jnp.where(mask, 1.0, 0.0).astype(BF16)
        carry = jnp.zeros((E, 1), F32)
        outs = []
        for i in range(S // blk):
            cs = _dot(mb[:, i * blk:(i + 1) * blk], tri) + carry
            outs.append(cs)
            carry = cs[:, blk - 1:blk]
        return jnp.concatenate(outs, axis=1), carry

    cs_gt, n_gt = cumsum(gt)
    cs_eq, _ = cumsum(eq)
    need = cap - n_gt
    sel = gt | (eq & (cs_eq <= need))
    slot = cs_gt + jnp.minimum(cs_eq, need) - 1.0
    pos_ref[0] = jnp.where(sel, slot, -1.0).astype(jnp.int32)


def _route(aff, cap):
    B, E, S = aff.shape
    return pl.pallas_call(
        functools.partial(_route_kernel, cap=cap),
        grid=(B,),
        in_specs=[pl.BlockSpec((1, E, S), lambda b: (b, 0, 0))],
        out_specs=pl.BlockSpec((1, E, S), lambda b: (b, 0, 0)),
        out_shape=jax.ShapeDtypeStruct((B, E, S), jnp.int32),
        compiler_params=_cparams(("parallel",)),
        name="route",
    )(aff)


def _moe_kernel(pos_ref, aff_ref, h2_ref, wg_ref, wu_ref, wd_ref, o_ref, p_scr, g_scr, xe_scr, y_scr, *, cap):
    e = pl.program_id(1)
    f = pl.program_id(2)
    nf = pl.num_programs(2)
    S = h2_ref.shape[1]

    @pl.when((e == 0) & (f == 0))
    def _():
        o_ref[...] = jnp.zeros_like(o_ref)

    @pl.when(f == 0)
    def _():
        slot = lax.broadcasted_iota(jnp.int32, (cap, S), 0)
        hit = pos_ref[0, 0] == slot
        onehot = jnp.where(hit, 1.0, 0.0).astype(BF16)
        p_scr[...] = onehot
        g_scr[...] = jnp.sum(jnp.where(hit, aff_ref[0, 0], 0.0), axis=1, keepdims=True)
        xe_scr[...] = _dot(onehot, h2_ref[0]).astype(BF16)
        y_scr[...] = jnp.zeros_like(y_scr)

    xe = xe_scr[...]
    a = _dot(xe, wg_ref[0])
    u = _dot(xe, wu_ref[0])
    act = (a * jax.nn.sigmoid(a) * u).astype(BF16)
    y_scr[...] += _dot(act, wd_ref[0])

    @pl.when(f == nf - 1)
    def _():
        ye = (y_scr[...] * g_scr[...]).astype(BF16)
        o_ref[0] += _dot_tn(p_scr[...], ye)


def _moe(pos, aff, h2, wg, wu, wd, cap, tf):
    B, S, D = h2.shape
    E, _, F = wg.shape
    pos4 = pos.reshape(B, E, 1, S)
    aff4 = aff.reshape(B, E, 1, S)
    row_spec = pl.BlockSpec((1, 1, 1, S), lambda b, e, f: (b, e, 0, 0))
    return pl.pallas_call(
        functools.partial(_moe_kernel, cap=cap),
        grid=(B, E, F // tf),
        in_specs=[row_spec, row_spec,
                  pl.BlockSpec((1, S, D), lambda b, e, f: (b, 0, 0)),
                  pl.BlockSpec((1, D, tf), lambda b, e, f: (e, 0, f)),
                  pl.BlockSpec((1, D, tf), lambda b, e, f: (e, 0, f)),
                  pl.BlockSpec((1, tf, D), lambda b, e, f: (e, f, 0))],
        out_specs=pl.BlockSpec((1, S, D), lambda b, e, f: (b, 0, 0)),
        out_shape=jax.ShapeDtypeStruct((B, S, D), F32),
        scratch_shapes=[pltpu.VMEM((cap, S), BF16), pltpu.VMEM((cap, 1), F32),
                        pltpu.VMEM((cap, D), BF16), pltpu.VMEM((cap, D), F32)],
        compiler_params=_cparams(("parallel", "arbitrary", "arbitrary")),
        name="moe",
    )(pos4, aff4, h2, wg, wu, wd)


def _residual_kernel(x_ref, d_ref, o_ref):
    o_ref[0] = x_ref[0] + d_ref[0]


def _residual_norm_kernel(x_ref, d_ref, g_ref, o_ref):
    o_ref[0] = _rms(x_ref[0] + d_ref[0], g_ref[...])


def _residual(x1, delta, g, tm):
    B, S, D = x1.shape
    tok = pl.BlockSpec((1, tm, D), lambda b, i: (b, i, 0))
    if g is None:
        body, extra, extra_specs = _residual_kernel, (), []
    else:
        body, extra, extra_specs = _residual_norm_kernel, (g,), [_resident((1, D), lambda b, i: (0, 0))]
    return pl.pallas_call(
        body,
        grid=(B, S // tm),
        in_specs=[tok, tok] + extra_specs,
        out_specs=tok,
        out_shape=jax.ShapeDtypeStruct((B, S, D), F32),
        compiler_params=_cparams(("parallel", "parallel")),
        name="residual",
    )(x1, delta, *extra)


def _tile(n, pref):
    t = min(n, pref)
    assert n % t == 0, (n, t)
    return t


def _lambda_init(layer):
    return 0.8 - 0.6 * float(np.exp(-0.3 * layer))


def _lane_half_forms(w, n_heads, width):
    d = w.shape[0]
    w = w.reshape(d, n_heads, width)
    z = jnp.zeros_like(w)
    return jnp.concatenate([w, z, z, w], axis=-1).reshape(d, n_heads * 4 * width)


def kernel(x, mem, attn_norm_g, mem_norm_g, w_in, w_mem_kv, da_lambda_q1, da_lambda_k1, da_lambda_q2,
           da_lambda_k2, da_subln_g, wa_sink, w_da_o, w_wa_o, w_xa_o, w_out, ffn_norm_g, w_router,
           w_exp_gate, w_exp_up, w_exp_down, final_norm_g):
    B, S, D = x.shape
    depth = w_in.shape[0]
    cap = max(1, EC_FACTOR * S // N_EXPERTS)
    row = lambda v: v.reshape(1, -1).astype(F32)
    da_w = DA_HEADS * 2 * DA_HEAD_DIM
    wa_kv_w = WA_KV_HEADS * WA_HEAD_DIM
    o_waq = 3 * da_w
    o_wak = o_waq + WA_HEADS * WA_HEAD_DIM
    o_wav = o_wak + wa_kv_w
    o_xaq = o_wav + wa_kv_w
    o_gate = o_xaq + XA_HEADS * XA_HEAD_DIM
    for l in range(depth):
        lam_init = _lambda_init(l)
        wl = w_in[l]
        w1 = jnp.concatenate([wl[:, :o_wak],
                              _lane_half_forms(wl[:, o_wak:o_wav], WA_KV_HEADS, WA_HEAD_DIM),
                              wl[:, o_wav:o_gate]], axis=1).astype(BF16)
        wg = wl[:, o_gate:].astype(BF16)
        wr = jnp.pad(w_router[l], ((0, 0), (0, LANES - N_EXPERTS))).astype(BF16)

        daq, dak, dav, waq, wak, wavt, xaq = _in_proj(x, row(attn_norm_g[l]), w1, _tile(S, 512))
        memkv = _mem_proj(mem, row(mem_norm_g[l]), w_mem_kv[l].astype(BF16))
        lams = [row(v[l]) for v in (da_lambda_q1, da_lambda_k1, da_lambda_q2, da_lambda_k2)]
        oda = _diff_attn(daq, dak, dav, lams, da_subln_g[l].astype(F32), lam_init, _tile(S, 512))
        owa = _win_attn(waq, wak, wavt, wa_sink[l].astype(F32), _tile(S, 256))
        x1, h2, aff = _merge(x, oda, owa, xaq, memkv, row(attn_norm_g[l]), wg, w_da_o[l].astype(BF16),
                             w_wa_o[l].astype(BF16), w_xa_o[l].astype(BF16), w_out[l].astype(BF16),
                             row(ffn_norm_g[l]), wr, _tile(S, 512))
        pos = _route(aff, cap)
        delta = _moe(pos, aff, h2, w_exp_gate[l].astype(BF16), w_exp_up[l].astype(BF16),
                     w_exp_down[l].astype(BF16), cap, _tile(w_exp_gate.shape[-1], 1024))
        last = l == depth - 1
        x = _residual(x1, delta, row(final_norm_g) if last else None, _tile(S, 512))
    return x
```

```python
import functools

import numpy as np
import jax
import jax.numpy as jnp
from jax import lax
from jax.experimental import pallas as pl
from jax.experimental.pallas import tpu as pltpu

EPS = 1e-6
BLOCK = 128
WINDOW = 128
DA_HEADS = 8
DA_HEAD_DIM = 64
DA_V_DIM = 2 * DA_HEAD_DIM
WA_HEADS = 16
WA_KV_HEADS = 4
WA_GROUPS = WA_HEADS // WA_KV_HEADS
WA_HEAD_DIM = 64
XA_HEADS = 4
XA_HEAD_DIM = 256
N_BRANCH = 3
N_EXPERTS = 16
EC_FACTOR = 2

LOG2E = 1.4426950408889634
DA_Q_SCALE = DA_HEAD_DIM ** -0.5 * LOG2E
XA_Q_SCALE = XA_HEAD_DIM ** -0.5 * LOG2E
WA_Q_SCALE = WA_HEAD_DIM ** -0.5 * LOG2E
WA_ONES_ROWS = 16
WA_KV_PER_STEP = 4
DA_KEY_CHUNK = 256
DA_ONES_ROWS = 16
DA_DRAIN_STEPS = 2

LANES = 128
VMEM_LIMIT = 56 * 1024 * 1024

F32 = jnp.float32
BF16 = jnp.bfloat16
NEG_INF = float("-inf")


def _cparams(sem, flags=None):
    return pltpu.CompilerParams(dimension_semantics=sem, vmem_limit_bytes=VMEM_LIMIT, flags=flags)


def _rms(xf, g_row):
    ms = jnp.mean(xf * xf, axis=-1, keepdims=True)
    return xf * lax.rsqrt(ms + EPS) * g_row


def _dot(a, b):
    return jnp.dot(a, b, preferred_element_type=F32)


def _dot_nt(a, b):
    return lax.dot_general(a, b, (((1,), (1,)), ((), ())), preferred_element_type=F32)


def _dot_tn(a, b):
    return lax.dot_general(a, b, (((0,), (0,)), ((), ())), preferred_element_type=F32)


def _resident(shape, index_map):
    return pl.BlockSpec(shape, index_map, pipeline_mode=pl.Buffered(1))


def _in_proj_kernel(x_ref, g_ref, w_ref, daq_ref, dak_ref, dav_ref, waq_ref, wak_ref, wavt_ref, xaq_ref):
    d = x_ref.shape[-1]
    tm = x_ref.shape[1]
    h = _rms(x_ref[0], g_ref[...]).astype(BF16)
    col = [0]

    def sec(width):
        r = _dot(h, w_ref[:, col[0]:col[0] + width])
        col[0] += width
        return r

    for ref, scale in ((daq_ref, DA_Q_SCALE), (dak_ref, None), (dav_ref, None)):
        r = sec(d)
        r = (r if scale is None else r * scale).astype(BF16)
        for hh in range(DA_HEADS):
            ref[0, hh] = r[:, hh * LANES:(hh + 1) * LANES]
    waq_ref[0] = (sec(d) * WA_Q_SCALE).astype(BF16)
    wak_ref[0] = sec(d).astype(BF16)
    vt = sec(WA_KV_HEADS * WA_HEAD_DIM).T
    ones = jnp.ones((WA_ONES_ROWS, LANES), BF16)
    for hh in range(WA_KV_HEADS):
        for kb in range(tm // LANES):
            wavt_ref[0, hh, kb, 0:WA_HEAD_DIM, :] = vt[hh * WA_HEAD_DIM:(hh + 1) * WA_HEAD_DIM,
                                                       kb * LANES:(kb + 1) * LANES].astype(BF16)
            wavt_ref[0, hh, kb, WA_HEAD_DIM:, :] = ones
    xaq_ref[0] = (sec(d) * XA_Q_SCALE).astype(BF16)


def _in_proj(x, g, w1, tm):
    B, S, D = x.shape
    n_t = S // tm
    head_shape = jax.ShapeDtypeStruct((B, DA_HEADS, S, LANES), BF16)
    head_spec = pl.BlockSpec((1, DA_HEADS, tm, LANES), lambda b, i: (b, 0, i, 0))
    tok = pl.BlockSpec((1, tm, D), lambda b, i: (b, i, 0))
    tok_shape = jax.ShapeDtypeStruct((B, S, D), BF16)
    vt_rows = WA_HEAD_DIM + WA_ONES_ROWS
    return pl.pallas_call(
        _in_proj_kernel,
        grid=(B, n_t),
        in_specs=[tok, _resident((1, D), lambda b, i: (0, 0)), _resident(w1.shape, lambda b, i: (0, 0))],
        out_specs=[head_spec, head_spec, head_spec, tok, tok,
                   pl.BlockSpec((1, WA_KV_HEADS, tm // LANES, vt_rows, LANES), lambda b, i: (b, 0, i, 0, 0)),
                   tok],
        out_shape=[head_shape, head_shape, head_shape, tok_shape, tok_shape,
                   jax.ShapeDtypeStruct((B, WA_KV_HEADS, S // LANES, vt_rows, LANES), BF16),
                   tok_shape],
        compiler_params=_cparams(("parallel", "parallel")),
        name="in_proj",
    )(x, g, w1)


def _mem_proj_kernel(m_ref, g_ref, w_ref, o_ref):
    h = _rms(m_ref[0], g_ref[...]).astype(BF16)
    o_ref[0] = _dot(h, w_ref[...]).astype(BF16)


def _mem_proj(mem, g, w):
    B, M, D = mem.shape
    N = w.shape[1]
    return pl.pallas_call(
        _mem_proj_kernel,
        grid=(B,),
        in_specs=[pl.BlockSpec((1, M, D), lambda b: (b, 0, 0)),
                  _resident((1, D), lambda b: (0, 0)),
                  _resident((D, N), lambda b: (0, 0))],
        out_specs=pl.BlockSpec((1, M, N), lambda b: (b, 0, 0)),
        out_shape=jax.ShapeDtypeStruct((B, M, N), BF16),
        compiler_params=_cparams(("parallel",)),
        name="mem_proj",
    )(mem, g, w)


def _diff_attn_kernel(lq1_ref, lk1_ref, lq2_ref, lk2_ref, q_ref, k_ref, v_ref, bias_ref, g_ref, o_ref,
                      k1_scr, k2_scr, vt_scr, sa_scr, sb_scr, ma_scr, mb_scr, ea_scr, eb_scr, *, lam_init, n_t):
    S = k_ref.shape[2]
    tq = q_ref.shape[2]
    j = pl.program_id(1)
    n_tiles = pl.num_programs(1) - DA_DRAIN_STEPS
    ck = DA_KEY_CHUNK
    seq, tile = _da_seq_tile(j, n_t, n_tiles)
    prev_seq, _ = _da_seq_tile(j - 1, n_t, n_tiles)

    @pl.when((tile == 0) & (j < n_tiles))
    def _():
        kk = k_ref[0, 0]
        lane = lax.broadcasted_iota(jnp.int32, kk.shape, 1)
        zero = jnp.zeros_like(kk)
        k1_scr[...] = jnp.where(lane < DA_HEAD_DIM, kk, zero)
        k2_scr[...] = jnp.where(lane >= DA_HEAD_DIM, kk, zero)
        vt_scr[seq % 2, 0:DA_V_DIM, :] = v_ref[0, 0].astype(F32).T.astype(BF16)
        vt_scr[seq % 2, DA_V_DIM:, :] = jnp.ones((DA_ONES_ROWS, S), BF16)

    @pl.when(j == 0)
    def _():
        sb_scr[...] = jnp.zeros_like(sb_scr)
        mb_scr[...] = jnp.zeros_like(mb_scr)
        eb_scr[...] = jnp.ones_like(eb_scr)

    lam = (jnp.exp(jnp.sum(lq1_ref[...] * lk1_ref[...], keepdims=True))
           - jnp.exp(jnp.sum(lq2_ref[...] * lk2_ref[...], keepdims=True)) + lam_init)

    q = q_ref[0, 0]
    r0 = (S - tq) - tile * tq
    vt_slot = prev_seq % 2

    def step(s_w, m_w, e_w, s_r, m_r, e_r):
        o1, l1 = e_r[0, 0:DA_V_DIM, :], e_r[0, DA_V_DIM:DA_V_DIM + 1, :]
        o2, l2 = e_r[1, 0:DA_V_DIM, :], e_r[1, DA_V_DIM:DA_V_DIM + 1, :]
        ot = o1 * (1.0 / l1) - o2 * (lam / l2)
        ms = jnp.mean(ot * ot, axis=0, keepdims=True)
        y = ot * lax.rsqrt(ms + EPS) * g_ref[...] * (1.0 - lam_init)
        o_ref[0] = y.T.astype(BF16)
        m_prev = (m_r[0:1, :], m_r[1:2, :])
        m_new = [None, None]
        acc = [None, None]
        for c in range(S // ck):
            rows = slice(c * ck, (c + 1) * ck)
            for i in range(2):
                p = jnp.exp2((s_r[i, rows, :] - m_prev[i]).astype(BF16))
                d = _dot(vt_scr[vt_slot, :, rows], p)
                acc[i] = d if acc[i] is None else acc[i] + d
            bias = bias_ref[0, pl.ds(pl.multiple_of(r0 + c * ck, ck), ck), :]
            for i, k_scr in enumerate((k1_scr, k2_scr)):
                s = _dot_nt(k_scr[rows, :], q) + bias
                s_w[i, rows, :] = s
                cm = jnp.max(s, axis=0, keepdims=True)
                m_new[i] = cm if m_new[i] is None else jnp.maximum(m_new[i], cm)
        m_w[0:1, :] = m_new[0]
        m_w[1:2, :] = m_new[1]
        e_w[0] = acc[0]
        e_w[1] = acc[1]

    @pl.when(j % 2 == 0)
    def _():
        step(sa_scr, ma_scr, ea_scr, sb_scr, mb_scr, eb_scr)

    @pl.when(j % 2 == 1)
    def _():
        step(sb_scr, mb_scr, eb_scr, sa_scr, ma_scr, ea_scr)


def _da_seq_tile(j, n_t, n_tiles):
    jc = jnp.clip(j, 0, n_tiles - 1)
    return jc // n_t, jc % n_t


def _diff_attn(daq, dak, dav, lams, subln_g, lam_init, tq):
    B, H, S, _ = daq.shape
    n_t = S // tq
    rows = 2 * S - tq
    slopes = jnp.exp2(-8.0 * jnp.arange(1, H + 1, dtype=F32) / H)
    r = jnp.arange(rows, dtype=jnp.int32)[:, None]
    c = jnp.arange(tq, dtype=jnp.int32)[None, :]
    dist = jnp.abs(r - (S - tq) - c).astype(F32)
    bias = -((slopes * LOG2E)[:, None, None] * dist[None])
    n_tiles = B * n_t
    lam_spec = _resident((1, DA_HEAD_DIM), lambda h, j: (0, 0))

    def kv_map(h, j):
        return (_da_seq_tile(j, n_t, n_tiles)[0], h, 0, 0)

    def q_map(h, j):
        seq, tile = _da_seq_tile(j, n_t, n_tiles)
        return (seq, h, tile, 0)

    def o_map(h, j):
        seq, tile = _da_seq_tile(j - DA_DRAIN_STEPS, n_t, n_tiles)
        return (seq, tile, h)

    kv_spec = pl.BlockSpec((1, 1, S, LANES), kv_map)
    acc_shape = pltpu.VMEM((2, DA_V_DIM + DA_ONES_ROWS, tq), F32)
    return pl.pallas_call(
        functools.partial(_diff_attn_kernel, lam_init=lam_init, n_t=n_t),
        grid=(H, n_tiles + DA_DRAIN_STEPS),
        in_specs=[lam_spec, lam_spec, lam_spec, lam_spec,
                  pl.BlockSpec((1, 1, tq, LANES), q_map),
                  kv_spec, kv_spec,
                  pl.BlockSpec((1, rows, tq), lambda h, j: (h, 0, 0)),
                  _resident((DA_V_DIM, 1), lambda h, j: (0, 0))],
        out_specs=pl.BlockSpec((1, tq, LANES), o_map),
        out_shape=jax.ShapeDtypeStruct((B, S, H * LANES), BF16),
        scratch_shapes=[pltpu.VMEM((S, LANES), BF16), pltpu.VMEM((S, LANES), BF16),
                        pltpu.VMEM((2, DA_V_DIM + DA_ONES_ROWS, S), BF16),
                        pltpu.VMEM((2, S, tq), F32), pltpu.VMEM((2, S, tq), F32),
                        pltpu.VMEM((2, tq), F32), pltpu.VMEM((2, tq), F32), acc_shape, acc_shape],
        compiler_params=_cparams(("parallel", "arbitrary")),
        name="diff_attn",
    )(*lams, daq, dak, dav, bias, subln_g.reshape(DA_V_DIM, 1))


def _win_attn_kernel(sink_ref, q_ref, k_ref, vt_ref, tab_ref, o_ref, sa_scr, sb_scr, ma_scr, mb_scr, *, n_t):
    S = k_ref.shape[1]
    tq = q_ref.shape[1]
    span = tq + 2 * WINDOW
    g = pl.program_id(0)
    j = pl.program_id(1)
    n_tiles = pl.num_programs(1) - 1
    _, tile = _da_seq_tile(j, n_t, n_tiles)
    _, prev_tile = _da_seq_tile(j - 1, n_t, n_tiles)

    def window_start(t):
        return pl.multiple_of(jnp.clip(t * tq - WINDOW, 0, S - span), LANES)

    @pl.when(j == 0)
    def _():
        sb_scr[...] = jnp.zeros_like(sb_scr)
        mb_scr[...] = jnp.zeros_like(mb_scr)

    shift = jnp.where(tile == 0, 0, jnp.where(tile == n_t - 1, 2 * WINDOW, WINDOW))
    tab_row = pl.multiple_of(2 * WINDOW - shift, LANES)
    kwin = k_ref[0, pl.ds(window_start(tile), span), :]
    kb0 = window_start(prev_tile) // LANES
    heads = WA_KV_PER_STEP * WA_GROUPS

    def step(s_w, m_w, s_r, m_r):
        outs = []
        for jh in range(heads):
            kv, pair_half = divmod(jh, WA_GROUPS)
            pair, half = divmod(pair_half, 2)
            sink = sink_ref[g * heads + jh] * LOG2E
            m = m_r[jh:jh + 1, :]
            acc = None
            for kb in range(span // LANES):
                p = jnp.exp2((s_r[jh, kb * LANES:(kb + 1) * LANES, :] - m).astype(BF16))
                d = _dot(vt_ref[0, kv, kb0 + kb], p)
                acc = d if acc is None else acc + d
            l = acc[WA_HEAD_DIM:WA_HEAD_DIM + 1, :] + jnp.exp2(sink - m)
            outs.append(acc[0:WA_HEAD_DIM, :] * (1.0 / l))
            qcol = (kv * 2 + pair) * LANES
            kcol = (kv * 2 + half) * LANES
            qp = q_ref[0, :, qcol:qcol + LANES]
            s = _dot_nt(kwin[:, kcol:kcol + LANES], qp) + tab_ref[jh, pl.ds(tab_row, span), :]
            s_w[jh] = s
            m_w[jh:jh + 1, :] = jnp.maximum(jnp.max(s, axis=0, keepdims=True), sink)
        o_ref[0] = jnp.concatenate(outs, axis=0).T.astype(BF16)

    @pl.when(j % 2 == 0)
    def _():
        step(sa_scr, ma_scr, sb_scr, mb_scr)

    @pl.when(j % 2 == 1)
    def _():
        step(sb_scr, mb_scr, sa_scr, ma_scr)


def _win_attn(waq, wak, wavt, sink, tq):
    B, S, D = waq.shape
    n_t = S // tq
    span = tq + 2 * WINDOW
    assert n_t >= 2 and S >= span, (S, tq)
    heads = WA_KV_PER_STEP * WA_GROUPS
    gw = heads * WA_HEAD_DIM
    slopes = jnp.exp2(-8.0 * jnp.arange(1, WA_HEADS + 1, dtype=F32) / WA_HEADS) * LOG2E
    r = jnp.arange(span + 2 * WINDOW, dtype=jnp.int32)[:, None]
    ir = jnp.arange(tq, dtype=jnp.int32)[None, :]
    dist = jnp.abs(2 * WINDOW + ir - r)
    tab = jnp.where((dist <= WINDOW)[None], -(slopes[:, None, None] * dist.astype(F32)[None]), NEG_INF)
    vt_rows = wavt.shape[3]
    n_tiles = B * n_t

    def q_map(g, j):
        seq, tile = _da_seq_tile(j, n_t, n_tiles)
        return (seq, tile, g)

    def o_map(g, j):
        seq, tile = _da_seq_tile(j - 1, n_t, n_tiles)
        return (seq, tile, g)

    return pl.pallas_call(
        functools.partial(_win_attn_kernel, n_t=n_t),
        grid=(WA_KV_HEADS // WA_KV_PER_STEP, n_tiles + 1),
        in_specs=[pl.BlockSpec(memory_space=pltpu.SMEM),
                  pl.BlockSpec((1, tq, gw), q_map),
                  pl.BlockSpec((1, S, gw), lambda g, j: (_da_seq_tile(j, n_t, n_tiles)[0], 0, g)),
                  pl.BlockSpec((1, WA_KV_PER_STEP, S // LANES, vt_rows, LANES),
                               lambda g, j: (_da_seq_tile(j - 1, n_t, n_tiles)[0], g, 0, 0, 0)),
                  pl.BlockSpec((heads, span + 2 * WINDOW, tq), lambda g, j: (g, 0, 0),
                               pipeline_mode=pl.Buffered(1))],
        out_specs=pl.BlockSpec((1, tq, gw), o_map),
        out_shape=jax.ShapeDtypeStruct((B, S, D), BF16),
        scratch_shapes=[pltpu.VMEM((heads, span, tq), F32), pltpu.VMEM((heads, span, tq), F32),
                        pltpu.VMEM((heads, tq), F32), pltpu.VMEM((heads, tq), F32)],
        compiler_params=_cparams(("parallel", "arbitrary")),
        name="win_attn",
    )(sink, waq, wak, wavt, tab)


def _mem_attn_proj(q_ref, kv_ref, w_ref):
    hd = XA_HEAD_DIM
    acc = None
    for hh in range(XA_HEADS):
        q = q_ref[0, :, hh * hd:(hh + 1) * hd]
        k = kv_ref[0, :, hh * hd:(hh + 1) * hd]
        v = kv_ref[0, :, (XA_HEADS + hh) * hd:(XA_HEADS + hh + 1) * hd]
        s = _dot_nt(q, k)
        p = jnp.exp2(s - jnp.max(s, axis=1, keepdims=True))
        pn = (p * (1.0 / jnp.sum(p, axis=1, keepdims=True))).astype(BF16)
        o = _dot(pn, v).astype(BF16)
        d = _dot(o, w_ref[hh * hd:(hh + 1) * hd, :])
        acc = d if acc is None else acc + d
    return acc


def _merge_kernel(x_ref, oda_ref, owa_ref, xaq_ref, memkv_ref, g1_ref, wg_ref, wda_ref, wwa_ref, wxa_ref,
                  wout_ref, g2_ref, wr_ref, x1_ref, h2_ref, aff_ref):
    d = x_ref.shape[-1]
    x = x_ref[0]
    h = _rms(x, g1_ref[...]).astype(BF16)
    branches = (_dot(oda_ref[0], wda_ref[...]), _dot(owa_ref[0], wwa_ref[...]),
                _mem_attn_proj(xaq_ref, memkv_ref, wxa_ref))
    merged = None
    for i, branch in enumerate(branches):
        term = jax.nn.sigmoid(_dot(h, wg_ref[:, i * d:(i + 1) * d])) * branch
        merged = term if merged is None else merged + term
    x1 = x + _dot(merged.astype(BF16), wout_ref[...])
    x1_ref[0] = x1
    h2 = _rms(x1, g2_ref[...]).astype(BF16)
    h2_ref[0] = h2
    logits = _dot(h2, wr_ref[...])
    lt = logits.T[:N_EXPERTS, :]
    m = jnp.max(lt, axis=0, keepdims=True)
    e = jnp.exp(lt - m)
    aff_ref[0] = e / jnp.sum(e, axis=0, keepdims=True)


def _merge(x, oda, owa, xaq, memkv, g1, wg, wda, wwa, wxa, wout, g2, wr, tm):
    B, S, D = x.shape
    tok = pl.BlockSpec((1, tm, D), lambda b, i: (b, i, 0))
    full = lambda a: _resident(a.shape, lambda b, i: (0,) * a.ndim)
    return pl.pallas_call(
        _merge_kernel,
        grid=(B, S // tm),
        in_specs=[tok, tok, tok, tok, pl.BlockSpec((1,) + memkv.shape[1:], lambda b, i: (b, 0, 0)),
                  full(g1), full(wg), full(wda), full(wwa), full(wxa), full(wout), full(g2), full(wr)],
        out_specs=[tok, tok, pl.BlockSpec((1, N_EXPERTS, tm), lambda b, i: (b, 0, i))],
        out_shape=[jax.ShapeDtypeStruct((B, S, D), F32),
                   jax.ShapeDtypeStruct((B, S, D), BF16),
                   jax.ShapeDtypeStruct((B, N_EXPERTS, S), F32)],
        compiler_params=_cparams(("parallel", "parallel")),
        name="merge",
    )(x, oda, owa, xaq, memkv, g1, wg, wda, wwa, wxa, wout, g2, wr)


def _route_kernel(aff_ref, pos_ref, *, cap):
    aff = aff_ref[0]
    E, S = aff.shape
    bits = pltpu.bitcast(aff, jnp.int32)
    prefix = jnp.zeros((E, 1), jnp.int32)
    for bit in range(30, -1, -1):
        cand = prefix | (1 << bit)
        cnt = jnp.sum(jnp.where(bits >= cand, 1.0, 0.0), axis=1, keepdims=True)
        prefix = jnp.where(cnt >= cap, cand, prefix)
    gt = bits > prefix
    eq = bits == prefix
    blk = 2 * LANES
    tri = (lax.broadcasted_iota(jnp.int32, (blk, blk), 0) <= lax.broadcasted_iota(jnp.int32, (blk, blk), 1))
    tri = jnp.where(tri, 1.0, 0.0).astype(BF16)

    def cumsum(mask):
        mb = jnp.where(mask, 1.0, 0.0).astype(BF16)
        carry = jnp.zeros((E, 1), F32)
        outs = []
        for i in range(S // blk):
            cs = _dot(mb[:, i * blk:(i + 1) * blk], tri) + carry
            outs.append(cs)
            carry = cs[:, blk - 1:blk]
        return jnp.concatenate(outs, axis=1), carry

    cs_gt, n_gt = cumsum(gt)
    cs_eq, _ = cumsum(eq)
    need = cap - n_gt
    sel = gt | (eq & (cs_eq <= need))
    slot = cs_gt + jnp.minimum(cs_eq, need) - 1.0
    pos_ref[0] = jnp.where(sel, slot, -1.0).astype(jnp.int32)


def _route(aff, cap):
    B, E, S = aff.shape
    return pl.pallas_call(
        functools.partial(_route_kernel, cap=cap),
        grid=(B,),
        in_specs=[pl.BlockSpec((1, E, S), lambda b: (b, 0, 0))],
        out_specs=pl.BlockSpec((1, E, S), lambda b: (b, 0, 0)),
        out_shape=jax.ShapeDtypeStruct((B, E, S), jnp.int32),
        compiler_params=_cparams(("parallel",)),
        name="route",
    )(aff)


def _moe_kernel(pos_ref, aff_ref, h2_ref, wg_ref, wu_ref, wd_ref, o_ref, p_scr, g_scr, xe_scr, y_scr, *, cap):
    e = pl.program_id(1)
    f = pl.program_id(2)
    nf = pl.num_programs(2)
    S = h2_ref.shape[1]

    @pl.when((e == 0) & (f == 0))
    def _():
        o_ref[...] = jnp.zeros_like(o_ref)

    @pl.when(f == 0)
    def _():
        slot = lax.broadcasted_iota(jnp.int32, (cap, S), 0)
        hit = pos_ref[0, 0] == slot
        onehot = jnp.where(hit, 1.0, 0.0).astype(BF16)
        p_scr[...] = onehot
        g_scr[...] = jnp.sum(jnp.where(hit, aff_ref[0, 0], 0.0), axis=1, keepdims=True)
        xe_scr[...] = _dot(onehot, h2_ref[0]).astype(BF16)
        y_scr[...] = jnp.zeros_like(y_scr)

    xe = xe_scr[...]
    a = _dot(xe, wg_ref[0])
    u = _dot(xe, wu_ref[0])
    act = (a * jax.nn.sigmoid(a) * u).astype(BF16)
    y_scr[...] += _dot(act, wd_ref[0])

    @pl.when(f == nf - 1)
    def _():
        ye = (y_scr[...] * g_scr[...]).astype(BF16)
        o_ref[0] += _dot_tn(p_scr[...], ye)


def _moe(pos, aff, h2, wg, wu, wd, cap, tf):
    B, S, D = h2.shape
    E, _, F = wg.shape
    pos4 = pos.reshape(B, E, 1, S)
    aff4 = aff.reshape(B, E, 1, S)
    row_spec = pl.BlockSpec((1, 1, 1, S), lambda b, e, f: (b, e, 0, 0))
    return pl.pallas_call(
        functools.partial(_moe_kernel, cap=cap),
        grid=(B, E, F // tf),
        in_specs=[row_spec, row_spec,
                  pl.BlockSpec((1, S, D), lambda b, e, f: (b, 0, 0)),
                  pl.BlockSpec((1, D, tf), lambda b, e, f: (e, 0, f)),
                  pl.BlockSpec((1, D, tf), lambda b, e, f: (e, 0, f)),
                  pl.BlockSpec((1, tf, D), lambda b, e, f: (e, f, 0))],
        out_specs=pl.BlockSpec((1, S, D), lambda b, e, f: (b, 0, 0)),
        out_shape=jax.ShapeDtypeStruct((B, S, D), F32),
        scratch_shapes=[pltpu.VMEM((cap, S), BF16), pltpu.VMEM((cap, 1), F32),
                        pltpu.VMEM((cap, D), BF16), pltpu.VMEM((cap, D), F32)],
        compiler_params=_cparams(("parallel", "arbitrary", "arbitrary")),
        name="moe",
    )(pos4, aff4, h2, wg, wu, wd)


def _residual_kernel(x_ref, d_ref, o_ref):
    o_ref[0] = x_ref[0] + d_ref[0]


def _residual_norm_kernel(x_ref, d_ref, g_ref, o_ref):
    o_ref[0] = _rms(x_ref[0] + d_ref[0], g_ref[...])


def _residual(x1, delta, g, tm):
    B, S, D = x1.shape
    tok = pl.BlockSpec((1, tm, D), lambda b, i: (b, i, 0))
    if g is None:
        body, extra, extra_specs = _residual_kernel, (), []
    else:
        body, extra, extra_specs = _residual_norm_kernel, (g,), [_resident((1, D), lambda b, i: (0, 0))]
    return pl.pallas_call(
        body,
        grid=(B, S // tm),
        in_specs=[tok, tok] + extra_specs,
        out_specs=tok,
        out_shape=jax.ShapeDtypeStruct((B, S, D), F32),
        compiler_params=_cparams(("parallel", "parallel")),
        name="residual",
    )(x1, delta, *extra)


def _tile(n, pref):
    t = min(n, pref)
    assert n % t == 0, (n, t)
    return t


def _lambda_init(layer):
    return 0.8 - 0.6 * float(np.exp(-0.3 * layer))


def _lane_half_forms(w, n_heads, width):
    d = w.shape[0]
    w = w.reshape(d, n_heads, width)
    z = jnp.zeros_like(w)
    return jnp.concatenate([w, z, z, w], axis=-1).reshape(d, n_heads * 4 * width)


def kernel(x, mem, attn_norm_g, mem_norm_g, w_in, w_mem_kv, da_lambda_q1, da_lambda_k1, da_lambda_q2,
           da_lambda_k2, da_subln_g, wa_sink, w_da_o, w_wa_o, w_xa_o, w_out, ffn_norm_g, w_router,
           w_exp_gate, w_exp_up, w_exp_down, final_norm_g):
    B, S, D = x.shape
    depth = w_in.shape[0]
    cap = max(1, EC_FACTOR * S // N_EXPERTS)
    row = lambda v: v.reshape(1, -1).astype(F32)
    da_w = DA_HEADS * 2 * DA_HEAD_DIM
    wa_kv_w = WA_KV_HEADS * WA_HEAD_DIM
    o_waq = 3 * da_w
    o_wak = o_waq + WA_HEADS * WA_HEAD_DIM
    o_wav = o_wak + wa_kv_w
    o_xaq = o_wav + wa_kv_w
    o_gate = o_xaq + XA_HEADS * XA_HEAD_DIM
    for l in range(depth):
        lam_init = _lambda_init(l)
        wl = w_in[l]
        w1 = jnp.concatenate([wl[:, :o_wak],
                              _lane_half_forms(wl[:, o_wak:o_wav], WA_KV_HEADS, WA_HEAD_DIM),
                              wl[:, o_wav:o_gate]], axis=1).astype(BF16)
        wg = wl[:, o_gate:].astype(BF16)
        wr = jnp.pad(w_router[l], ((0, 0), (0, LANES - N_EXPERTS))).astype(BF16)

        daq, dak, dav, waq, wak, wavt, xaq = _in_proj(x, row(attn_norm_g[l]), w1, _tile(S, 512))
        memkv = _mem_proj(mem, row(mem_norm_g[l]), w_mem_kv[l].astype(BF16))
        lams = [row(v[l]) for v in (da_lambda_q1, da_lambda_k1, da_lambda_q2, da_lambda_k2)]
        oda = _diff_attn(daq, dak, dav, lams, da_subln_g[l].astype(F32), lam_init, _tile(S, 512))
        owa = _win_attn(waq, wak, wavt, wa_sink[l].astype(F32), _tile(S, 256))
        x1, h2, aff = _merge(x, oda, owa, xaq, memkv, row(attn_norm_g[l]), wg, w_da_o[l].astype(BF16),
                             w_wa_o[l].astype(BF16), w_xa_o[l].astype(BF16), w_out[l].astype(BF16),
                             row(ffn_norm_g[l]), wr, _tile(S, 512))
        pos = _route(aff, cap)
        delta = _moe(pos, aff, h2, w_exp_gate[l].astype(BF16), w_exp_up[l].astype(BF16),
                     w_exp_down[l].astype(BF16), cap, _tile(w_exp_gate.shape[-1], 2048))
        last = l == depth - 1
        x = _residual(x1, delta, row(final_norm_g) if last else None, _tile(S, 512))
    return x
```

```python
import functools

import numpy as np
import jax
import jax.numpy as jnp
from jax import lax
from jax.experimental import pallas as pl
from jax.experimental.pallas import tpu as pltpu

EPS = 1e-6
BLOCK = 128
WINDOW = 128
DA_HEADS = 8
DA_HEAD_DIM = 64
DA_V_DIM = 2 * DA_HEAD_DIM
WA_HEADS = 16
WA_KV_HEADS = 4
WA_GROUPS = WA_HEADS // WA_KV_HEADS
WA_HEAD_DIM = 64
XA_HEADS = 4
XA_HEAD_DIM = 256
N_BRANCH = 3
N_EXPERTS = 16
EC_FACTOR = 2

LOG2E = 1.4426950408889634
DA_Q_SCALE = DA_HEAD_DIM ** -0.5 * LOG2E
XA_Q_SCALE = XA_HEAD_DIM ** -0.5 * LOG2E
WA_Q_SCALE = WA_HEAD_DIM ** -0.5 * LOG2E
WA_ONES_ROWS = 16
WA_KV_PER_STEP = 4
DA_KEY_CHUNK = 256
DA_ONES_ROWS = 16
DA_DRAIN_STEPS = 2

LANES = 128
VMEM_LIMIT = 56 * 1024 * 1024

F32 = jnp.float32
BF16 = jnp.bfloat16
NEG_INF = float("-inf")


def _cparams(sem, flags=None):
    return pltpu.CompilerParams(dimension_semantics=sem, vmem_limit_bytes=VMEM_LIMIT, flags=flags)


def _rms(xf, g_row):
    ms = jnp.mean(xf * xf, axis=-1, keepdims=True)
    return xf * lax.rsqrt(ms + EPS) * g_row


def _dot(a, b):
    return jnp.dot(a, b, preferred_element_type=F32)


def _dot_nt(a, b):
    return lax.dot_general(a, b, (((1,), (1,)), ((), ())), preferred_element_type=F32)


def _dot_tn(a, b):
    return lax.dot_general(a, b, (((0,), (0,)), ((), ())), preferred_element_type=F32)


def _resident(shape, index_map):
    return pl.BlockSpec(shape, index_map, pipeline_mode=pl.Buffered(1))


def _in_proj_kernel(x_ref, g_ref, w_ref, daq_ref, dak_ref, dav_ref, waq_ref, wak_ref, wavt_ref, xaq_ref):
    d = x_ref.shape[-1]
    tm = x_ref.shape[1]
    h = _rms(x_ref[0], g_ref[...]).astype(BF16)
    col = [0]

    def sec(width):
        r = _dot(h, w_ref[:, col[0]:col[0] + width])
        col[0] += width
        return r

    for ref, scale in ((daq_ref, DA_Q_SCALE), (dak_ref, None), (dav_ref, None)):
        r = sec(d)
        r = (r if scale is None else r * scale).astype(BF16)
        for hh in range(DA_HEADS):
            ref[0, hh] = r[:, hh * LANES:(hh + 1) * LANES]
    waq_ref[0] = (sec(d) * WA_Q_SCALE).astype(BF16)
    wak_ref[0] = sec(d).astype(BF16)
    vt = sec(WA_KV_HEADS * WA_HEAD_DIM).T
    ones = jnp.ones((WA_ONES_ROWS, LANES), BF16)
    for hh in range(WA_KV_HEADS):
        for kb in range(tm // LANES):
            wavt_ref[0, hh, kb, 0:WA_HEAD_DIM, :] = vt[hh * WA_HEAD_DIM:(hh + 1) * WA_HEAD_DIM,
                                                       kb * LANES:(kb + 1) * LANES].astype(BF16)
            wavt_ref[0, hh, kb, WA_HEAD_DIM:, :] = ones
    xaq_ref[0] = (sec(d) * XA_Q_SCALE).astype(BF16)


def _in_proj(x, g, w1, tm):
    B, S, D = x.shape
    n_t = S // tm
    head_shape = jax.ShapeDtypeStruct((B, DA_HEADS, S, LANES), BF16)
    head_spec = pl.BlockSpec((1, DA_HEADS, tm, LANES), lambda b, i: (b, 0, i, 0))
    tok = pl.BlockSpec((1, tm, D), lambda b, i: (b, i, 0))
    tok_shape = jax.ShapeDtypeStruct((B, S, D), BF16)
    vt_rows = WA_HEAD_DIM + WA_ONES_ROWS
    return pl.pallas_call(
        _in_proj_kernel,
        grid=(B, n_t),
        in_specs=[tok, _resident((1, D), lambda b, i: (0, 0)), _resident(w1.shape, lambda b, i: (0, 0))],
        out_specs=[head_spec, head_spec, head_spec, tok, tok,
                   pl.BlockSpec((1, WA_KV_HEADS, tm // LANES, vt_rows, LANES), lambda b, i: (b, 0, i, 0, 0)),
                   tok],
        out_shape=[head_shape, head_shape, head_shape, tok_shape, tok_shape,
                   jax.ShapeDtypeStruct((B, WA_KV_HEADS, S // LANES, vt_rows, LANES), BF16),
                   tok_shape],
        compiler_params=_cparams(("parallel", "parallel")),
        name="in_proj",
    )(x, g, w1)


def _mem_proj_kernel(m_ref, g_ref, w_ref, o_ref):
    h = _rms(m_ref[0], g_ref[...]).astype(BF16)
    o_ref[0] = _dot(h, w_ref[...]).astype(BF16)


def _mem_proj(mem, g, w):
    B, M, D = mem.shape
    N = w.shape[1]
    return pl.pallas_call(
        _mem_proj_kernel,
        grid=(B,),
        in_specs=[pl.BlockSpec((1, M, D), lambda b: (b, 0, 0)),
                  _resident((1, D), lambda b: (0, 0)),
                  _resident((D, N), lambda b: (0, 0))],
        out_specs=pl.BlockSpec((1, M, N), lambda b: (b, 0, 0)),
        out_shape=jax.ShapeDtypeStruct((B, M, N), BF16),
        compiler_params=_cparams(("parallel",)),
        name="mem_proj",
    )(mem, g, w)


def _diff_attn_kernel(lq1_ref, lk1_ref, lq2_ref, lk2_ref, q_ref, k_ref, v_ref, bias_ref, g_ref, o_ref,
                      k1_scr, k2_scr, vt_scr, s_scr, m_scr, e_scr, *, lam_init, n_t):
    S = k_ref.shape[2]
    tq = q_ref.shape[2]
    j = pl.program_id(1)
    n_tiles = pl.num_programs(1) - DA_DRAIN_STEPS
    ck = DA_KEY_CHUNK
    seq, tile = _da_seq_tile(j, n_t, n_tiles)
    prev_seq, _ = _da_seq_tile(j - 1, n_t, n_tiles)

    @pl.when((tile == 0) & (j < n_tiles))
    def _():
        kk = k_ref[0, 0]
        lane = lax.broadcasted_iota(jnp.int32, kk.shape, 1)
        zero = jnp.zeros_like(kk)
        k1_scr[...] = jnp.where(lane < DA_HEAD_DIM, kk, zero)
        k2_scr[...] = jnp.where(lane >= DA_HEAD_DIM, kk, zero)
        vt_scr[seq % 2, 0:DA_V_DIM, :] = v_ref[0, 0].astype(F32).T.astype(BF16)
        vt_scr[seq % 2, DA_V_DIM:, :] = jnp.ones((DA_ONES_ROWS, S), BF16)

    @pl.when(j == 0)
    def _():
        s_scr[...] = jnp.zeros_like(s_scr)
        m_scr[...] = jnp.zeros_like(m_scr)
        e_scr[...] = jnp.ones_like(e_scr)

    lam = (jnp.exp(jnp.sum(lq1_ref[...] * lk1_ref[...], keepdims=True))
           - jnp.exp(jnp.sum(lq2_ref[...] * lk2_ref[...], keepdims=True)) + lam_init)

    q = q_ref[0, 0]
    r0 = (S - tq) - tile * tq
    vt_slot = prev_seq % 2

    o1, l1 = e_scr[0, 0:DA_V_DIM, :], e_scr[0, DA_V_DIM:DA_V_DIM + 1, :]
    o2, l2 = e_scr[1, 0:DA_V_DIM, :], e_scr[1, DA_V_DIM:DA_V_DIM + 1, :]
    ot = o1 * (1.0 / l1) - o2 * (lam / l2)
    ms = jnp.mean(ot * ot, axis=0, keepdims=True)
    y = ot * lax.rsqrt(ms + EPS) * g_ref[...] * (1.0 - lam_init)
    o_ref[0] = y.T.astype(BF16)
    m_prev = (m_scr[0:1, :], m_scr[1:2, :])
    m_new = [None, None]
    acc = [None, None]
    for c in range(S // ck):
        rows = slice(c * ck, (c + 1) * ck)
        for i in range(2):
            p = jnp.exp2((s_scr[i, rows, :] - m_prev[i]).astype(BF16))
            d = _dot(vt_scr[vt_slot, :, rows], p)
            acc[i] = d if acc[i] is None else acc[i] + d
        bias = bias_ref[0, pl.ds(pl.multiple_of(r0 + c * ck, ck), ck), :]
        for i, k_scr in enumerate((k1_scr, k2_scr)):
            s = _dot_nt(k_scr[rows, :], q) + bias
            s_scr[i, rows, :] = s
            cm = jnp.max(s, axis=0, keepdims=True)
            m_new[i] = cm if m_new[i] is None else jnp.maximum(m_new[i], cm)
    m_scr[0:1, :] = m_new[0]
    m_scr[1:2, :] = m_new[1]
    e_scr[0] = acc[0]
    e_scr[1] = acc[1]


def _da_seq_tile(j, n_t, n_tiles):
    jc = jnp.clip(j, 0, n_tiles - 1)
    return jc // n_t, jc % n_t


def _diff_attn(daq, dak, dav, lams, subln_g, lam_init, tq):
    B, H, S, _ = daq.shape
    n_t = S // tq
    rows = 2 * S - tq
    slopes = jnp.exp2(-8.0 * jnp.arange(1, H + 1, dtype=F32) / H)
    r = jnp.arange(rows, dtype=jnp.int32)[:, None]
    c = jnp.arange(tq, dtype=jnp.int32)[None, :]
    dist = jnp.abs(r - (S - tq) - c).astype(F32)
    bias = -((slopes * LOG2E)[:, None, None] * dist[None])
    n_tiles = B * n_t
    lam_spec = _resident((1, DA_HEAD_DIM), lambda h, j: (0, 0))

    def kv_map(h, j):
        return (_da_seq_tile(j, n_t, n_tiles)[0], h, 0, 0)

    def q_map(h, j):
        seq, tile = _da_seq_tile(j, n_t, n_tiles)
        return (seq, h, tile, 0)

    def o_map(h, j):
        seq, tile = _da_seq_tile(j - DA_DRAIN_STEPS, n_t, n_tiles)
        return (seq, tile, h)

    kv_spec = pl.BlockSpec((1, 1, S, LANES), kv_map)
    acc_shape = pltpu.VMEM((2, DA_V_DIM + DA_ONES_ROWS, tq), F32)
    return pl.pallas_call(
        functools.partial(_diff_attn_kernel, lam_init=lam_init, n_t=n_t),
        grid=(H, n_tiles + DA_DRAIN_STEPS),
        in_specs=[lam_spec, lam_spec, lam_spec, lam_spec,
                  pl.BlockSpec((1, 1, tq, LANES), q_map),
                  kv_spec, kv_spec,
                  pl.BlockSpec((1, rows, tq), lambda h, j: (h, 0, 0), pipeline_mode=pl.Buffered(1)),
                  _resident((DA_V_DIM, 1), lambda h, j: (0, 0))],
        out_specs=pl.BlockSpec((1, tq, LANES), o_map),
        out_shape=jax.ShapeDtypeStruct((B, S, H * LANES), BF16),
        scratch_shapes=[pltpu.VMEM((S, LANES), BF16), pltpu.VMEM((S, LANES), BF16),
                        pltpu.VMEM((2, DA_V_DIM + DA_ONES_ROWS, S), BF16),
                        pltpu.VMEM((2, S, tq), F32), pltpu.VMEM((2, tq), F32), acc_shape],
        compiler_params=_cparams(("parallel", "arbitrary")),
        name="diff_attn",
    )(*lams, daq, dak, dav, bias, subln_g.reshape(DA_V_DIM, 1))


def _win_attn_kernel(sink_ref, q_ref, k_ref, vt_ref, tab_ref, o_ref, sa_scr, sb_scr, ma_scr, mb_scr, *, n_t):
    S = k_ref.shape[1]
    tq = q_ref.shape[1]
    span = tq + 2 * WINDOW
    g = pl.program_id(0)
    j = pl.program_id(1)
    n_tiles = pl.num_programs(1) - 1
    _, tile = _da_seq_tile(j, n_t, n_tiles)
    _, prev_tile = _da_seq_tile(j - 1, n_t, n_tiles)

    def window_start(t):
        return pl.multiple_of(jnp.clip(t * tq - WINDOW, 0, S - span), LANES)

    @pl.when(j == 0)
    def _():
        sb_scr[...] = jnp.zeros_like(sb_scr)
        mb_scr[...] = jnp.zeros_like(mb_scr)

    shift = jnp.where(tile == 0, 0, jnp.where(tile == n_t - 1, 2 * WINDOW, WINDOW))
    tab_row = pl.multiple_of(2 * WINDOW - shift, LANES)
    kwin = k_ref[0, pl.ds(window_start(tile), span), :]
    kb0 = window_start(prev_tile) // LANES
    heads = WA_KV_PER_STEP * WA_GROUPS

    def step(s_w, m_w, s_r, m_r):
        outs = []
        for jh in range(heads):
            kv, pair_half = divmod(jh, WA_GROUPS)
            pair, half = divmod(pair_half, 2)
            sink = sink_ref[g * heads + jh] * LOG2E
            m = m_r[jh:jh + 1, :]
            acc = None
            for kb in range(span // LANES):
                p = jnp.exp2((s_r[jh, kb * LANES:(kb + 1) * LANES, :] - m).astype(BF16))
                d = _dot(vt_ref[0, kv, kb0 + kb], p)
                acc = d if acc is None else acc + d
            l = acc[WA_HEAD_DIM:WA_HEAD_DIM + 1, :] + jnp.exp2(sink - m)
            outs.append(acc[0:WA_HEAD_DIM, :] * (1.0 / l))
            qcol = (kv * 2 + pair) * LANES
            kcol = (kv * 2 + half) * LANES
            qp = q_ref[0, :, qcol:qcol + LANES]
            s = _dot_nt(kwin[:, kcol:kcol + LANES], qp) + tab_ref[jh, pl.ds(tab_row, span), :]
            s_w[jh] = s
            m_w[jh:jh + 1, :] = jnp.maximum(jnp.max(s, axis=0, keepdims=True), sink)
        o_ref[0] = jnp.concatenate(outs, axis=0).T.astype(BF16)

    @pl.when(j % 2 == 0)
    def _():
        step(sa_scr, ma_scr, sb_scr, mb_scr)

    @pl.when(j % 2 == 1)
    def _():
        step(sb_scr, mb_scr, sa_scr, ma_scr)


def _win_attn(waq, wak, wavt, sink, tq):
    B, S, D = waq.shape
    n_t = S // tq
    span = tq + 2 * WINDOW
    assert n_t >= 2 and S >= span, (S, tq)
    heads = WA_KV_PER_STEP * WA_GROUPS
    gw = heads * WA_HEAD_DIM
    slopes = jnp.exp2(-8.0 * jnp.arange(1, WA_HEADS + 1, dtype=F32) / WA_HEADS) * LOG2E
    r = jnp.arange(span + 2 * WINDOW, dtype=jnp.int32)[:, None]
    ir = jnp.arange(tq, dtype=jnp.int32)[None, :]
    dist = jnp.abs(2 * WINDOW + ir - r)
    tab = jnp.where((dist <= WINDOW)[None], -(slopes[:, None, None] * dist.astype(F32)[None]), NEG_INF)
    vt_rows = wavt.shape[3]
    n_tiles = B * n_t

    def q_map(g, j):
        seq, tile = _da_seq_tile(j, n_t, n_tiles)
        return (seq, tile, g)

    def o_map(g, j):
        seq, tile = _da_seq_tile(j - 1, n_t, n_tiles)
        return (seq, tile, g)

    return pl.pallas_call(
        functools.partial(_win_attn_kernel, n_t=n_t),
        grid=(WA_KV_HEADS // WA_KV_PER_STEP, n_tiles + 1),
        in_specs=[pl.BlockSpec(memory_space=pltpu.SMEM),
                  pl.BlockSpec((1, tq, gw), q_map),
                  pl.BlockSpec((1, S, gw), lambda g, j: (_da_seq_tile(j, n_t, n_tiles)[0], 0, g)),
                  pl.BlockSpec((1, WA_KV_PER_STEP, S // LANES, vt_rows, LANES),
                               lambda g, j: (_da_seq_tile(j - 1, n_t, n_tiles)[0], g, 0, 0, 0)),
                  pl.BlockSpec((heads, span + 2 * WINDOW, tq), lambda g, j: (g, 0, 0),
                               pipeline_mode=pl.Buffered(1))],
        out_specs=pl.BlockSpec((1, tq, gw), o_map),
        out_shape=jax.ShapeDtypeStruct((B, S, D), BF16),
        scratch_shapes=[pltpu.VMEM((heads, span, tq), F32), pltpu.VMEM((heads, span, tq), F32),
                        pltpu.VMEM((heads, tq), F32), pltpu.VMEM((heads, tq), F32)],
        compiler_params=_cparams(("parallel", "arbitrary")),
        name="win_attn",
    )(sink, waq, wak, wavt, tab)


def _mem_attn_proj(q_ref, kv_ref, w_ref):
    hd = XA_HEAD_DIM
    acc = None
    for hh in range(XA_HEADS):
        q = q_ref[0, :, hh * hd:(hh + 1) * hd]
        k = kv_ref[0, :, hh * hd:(hh + 1) * hd]
        v = kv_ref[0, :, (XA_HEADS + hh) * hd:(XA_HEADS + hh + 1) * hd]
        s = _dot_nt(q, k)
        p = jnp.exp2(s - jnp.max(s, axis=1, keepdims=True))
        pn = (p * (1.0 / jnp.sum(p, axis=1, keepdims=True))).astype(BF16)
        o = _dot(pn, v).astype(BF16)
        d = _dot(o, w_ref[hh * hd:(hh + 1) * hd, :])
        acc = d if acc is None else acc + d
    return acc


def _merge_kernel(x_ref, oda_ref, owa_ref, xaq_ref, memkv_ref, g1_ref, wg_ref, wda_ref, wwa_ref, wxa_ref,
                  wout_ref, g2_ref, wr_ref, x1_ref, h2_ref, aff_ref):
    d = x_ref.shape[-1]
    x = x_ref[0]
    h = _rms(x, g1_ref[...]).astype(BF16)
    branches = (_dot(oda_ref[0], wda_ref[...]), _dot(owa_ref[0], wwa_ref[...]),
                _mem_attn_proj(xaq_ref, memkv_ref, wxa_ref))
    merged = None
    for i, branch in enumerate(branches):
        term = jax.nn.sigmoid(_dot(h, wg_ref[:, i * d:(i + 1) * d])) * branch
        merged = term if merged is None else merged + term
    x1 = x + _dot(merged.astype(BF16), wout_ref[...])
    x1_ref[0] = x1
    h2 = _rms(x1, g2_ref[...]).astype(BF16)
    h2_ref[0] = h2
    logits = _dot(h2, wr_ref[...])
    lt = logits.T[:N_EXPERTS, :]
    m = jnp.max(lt, axis=0, keepdims=True)
    e = jnp.exp(lt - m)
    aff_ref[0] = e / jnp.sum(e, axis=0, keepdims=True)


def _merge(x, oda, owa, xaq, memkv, g1, wg, wda, wwa, wxa, wout, g2, wr, tm):
    B, S, D = x.shape
    tok = pl.BlockSpec((1, tm, D), lambda b, i: (b, i, 0))
    full = lambda a: _resident(a.shape, lambda b, i: (0,) * a.ndim)
    return pl.pallas_call(
        _merge_kernel,
        grid=(B, S // tm),
        in_specs=[tok, tok, tok, tok, pl.BlockSpec((1,) + memkv.shape[1:], lambda b, i: (b, 0, 0)),
                  full(g1), full(wg), full(wda), full(wwa), full(wxa), full(wout), full(g2), full(wr)],
        out_specs=[tok, tok, pl.BlockSpec((1, N_EXPERTS, tm), lambda b, i: (b, 0, i))],
        out_shape=[jax.ShapeDtypeStruct((B, S, D), F32),
                   jax.ShapeDtypeStruct((B, S, D), BF16),
                   jax.ShapeDtypeStruct((B, N_EXPERTS, S), F32)],
        compiler_params=_cparams(("parallel", "parallel")),
        name="merge",
    )(x, oda, owa, xaq, memkv, g1, wg, wda, wwa, wxa, wout, g2, wr)


def _route_kernel(aff_ref, pos_ref, *, cap):
    aff = aff_ref[0]
    E, S = aff.shape
    bits = pltpu.bitcast(aff, jnp.int32)
    prefix = jnp.zeros((E, 1), jnp.int32)
    for bit in range(30, -1, -1):
        cand = prefix | (1 << bit)
        cnt = jnp.sum(jnp.where(bits >= cand, 1.0, 0.0), axis=1, keepdims=True)
        prefix = jnp.where(cnt >= cap, cand, prefix)
    gt = bits > prefix
    eq = bits == prefix
    blk = 2 * LANES
    tri = (lax.broadcasted_iota(jnp.int32, (blk, blk), 0) <= lax.broadcasted_iota(jnp.int32, (blk, blk), 1))
    tri = jnp.where(tri, 1.0, 0.0).astype(BF16)

    def cumsum(mask):
        mb = jnp.where(mask, 1.0, 0.0).astype(BF16)
        carry = jnp.zeros((E, 1), F32)
        outs = []
        for i in range(S // blk):
            cs = _dot(mb[:, i * blk:(i + 1) * blk], tri) + carry
            outs.append(cs)
            carry = cs[:, blk - 1:blk]
        return jnp.concatenate(outs, axis=1), carry

    cs_gt, n_gt = cumsum(gt)
    cs_eq, _ = cumsum(eq)
    need = cap - n_gt
    sel = gt | (eq & (cs_eq <= need))
    slot = cs_gt + jnp.minimum(cs_eq, need) - 1.0
    pos_ref[0] = jnp.where(sel, slot, -1.0).astype(jnp.int32)


def _route(aff, cap):
    B, E, S = aff.shape
    return pl.pallas_call(
        functools.partial(_route_kernel, cap=cap),
        grid=(B,),
        in_specs=[pl.BlockSpec((1, E, S), lambda b: (b, 0, 0))],
        out_specs=pl.BlockSpec((1, E, S), lambda b: (b, 0, 0)),
        out_shape=jax.ShapeDtypeStruct((B, E, S), jnp.int32),
        compiler_params=_cparams(("parallel",)),
        name="route",
    )(aff)


def _moe_kernel(pos_ref, aff_ref, h2_ref, wg_ref, wu_ref, wd_ref, o_ref, p_scr, g_scr, xe_scr, y_scr, *, cap):
    e = pl.program_id(1)
    f = pl.program_id(2)
    nf = pl.num_programs(2)
    S = h2_ref.shape[1]

    @pl.when((e == 0) & (f == 0))
    def _():
        o_ref[...] = jnp.zeros_like(o_ref)

    @pl.when(f == 0)
    def _():
        slot = lax.broadcasted_iota(jnp.int32, (cap, S), 0)
        hit = pos_ref[0, 0] == slot
        onehot = jnp.where(hit, 1.0, 0.0).astype(BF16)
        p_scr[...] = onehot
        g_scr[...] = jnp.sum(jnp.where(hit, aff_ref[0, 0], 0.0), axis=1, keepdims=True)
        xe_scr[...] = _dot(onehot, h2_ref[0]).astype(BF16)
        y_scr[...] = jnp.zeros_like(y_scr)

    xe = xe_scr[...]
    a = _dot(xe, wg_ref[0])
    u = _dot(xe, wu_ref[0])
    act = (a * jax.nn.sigmoid(a) * u).astype(BF16)
    y_scr[...] += _dot(act, wd_ref[0])

    @pl.when(f == nf - 1)
    def _():
        ye = (y_scr[...] * g_scr[...]).astype(BF16)
        o_ref[0] += _dot_tn(p_scr[...], ye)


def _moe(pos, aff, h2, wg, wu, wd, cap, tf):
    B, S, D = h2.shape
    E, _, F = wg.shape
    pos4 = pos.reshape(B, E, 1, S)
    aff4 = aff.reshape(B, E, 1, S)
    row_spec = pl.BlockSpec((1, 1, 1, S), lambda b, e, f: (b, e, 0, 0))
    return pl.pallas_call(
        functools.partial(_moe_kernel, cap=cap),
        grid=(B, E, F // tf),
        in_specs=[row_spec, row_spec,
                  pl.BlockSpec((1, S, D), lambda b, e, f: (b, 0, 0)),
                  pl.BlockSpec((1, D, tf), lambda b, e, f: (e, 0, f)),
                  pl.BlockSpec((1, D, tf), lambda b, e, f: (e, 0, f)),
                  pl.BlockSpec((1, tf, D), lambda b, e, f: (e, f, 0))],
        out_specs=pl.BlockSpec((1, S, D), lambda b, e, f: (b, 0, 0)),
        out_shape=jax.ShapeDtypeStruct((B, S, D), F32),
        scratch_shapes=[pltpu.VMEM((cap, S), BF16), pltpu.VMEM((cap, 1), F32),
                        pltpu.VMEM((cap, D), BF16), pltpu.VMEM((cap, D), F32)],
        compiler_params=_cparams(("parallel", "arbitrary", "arbitrary")),
        name="moe",
    )(pos4, aff4, h2, wg, wu, wd)


def _residual_kernel(x_ref, d_ref, o_ref):
    o_ref[0] = x_ref[0] + d_ref[0]


def _residual_norm_kernel(x_ref, d_ref, g_ref, o_ref):
    o_ref[0] = _rms(x_ref[0] + d_ref[0], g_ref[...])


def _residual(x1, delta, g, tm):
    B, S, D = x1.shape
    tok = pl.BlockSpec((1, tm, D), lambda b, i: (b, i, 0))
    if g is None:
        body, extra, extra_specs = _residual_kernel, (), []
    else:
        body, extra, extra_specs = _residual_norm_kernel, (g,), [_resident((1, D), lambda b, i: (0, 0))]
    return pl.pallas_call(
        body,
        grid=(B, S // tm),
        in_specs=[tok, tok] + extra_specs,
        out_specs=tok,
        out_shape=jax.ShapeDtypeStruct((B, S, D), F32),
        compiler_params=_cparams(("parallel", "parallel")),
        name="residual",
    )(x1, delta, *extra)


def _tile(n, pref):
    t = min(n, pref)
    assert n % t == 0, (n, t)
    return t


def _lambda_init(layer):
    return 0.8 - 0.6 * float(np.exp(-0.3 * layer))


def _lane_half_forms(w, n_heads, width):
    d = w.shape[0]
    w = w.reshape(d, n_heads, width)
    z = jnp.zeros_like(w)
    return jnp.concatenate([w, z, z, w], axis=-1).reshape(d, n_heads * 4 * width)


def kernel(x, mem, attn_norm_g, mem_norm_g, w_in, w_mem_kv, da_lambda_q1, da_lambda_k1, da_lambda_q2,
           da_lambda_k2, da_subln_g, wa_sink, w_da_o, w_wa_o, w_xa_o, w_out, ffn_norm_g, w_router,
           w_exp_gate, w_exp_up, w_exp_down, final_norm_g):
    B, S, D = x.shape
    depth = w_in.shape[0]
    cap = max(1, EC_FACTOR * S // N_EXPERTS)
    row = lambda v: v.reshape(1, -1).astype(F32)
    da_w = DA_HEADS * 2 * DA_HEAD_DIM
    wa_kv_w = WA_KV_HEADS * WA_HEAD_DIM
    o_waq = 3 * da_w
    o_wak = o_waq + WA_HEADS * WA_HEAD_DIM
    o_wav = o_wak + wa_kv_w
    o_xaq = o_wav + wa_kv_w
    o_gate = o_xaq + XA_HEADS * XA_HEAD_DIM
    for l in range(depth):
        lam_init = _lambda_init(l)
        wl = w_in[l]
        w1 = jnp.concatenate([wl[:, :o_wak],
                              _lane_half_forms(wl[:, o_wak:o_wav], WA_KV_HEADS, WA_HEAD_DIM),
                              wl[:, o_wav:o_gate]], axis=1).astype(BF16)
        wg = wl[:, o_gate:].astype(BF16)
        wr = jnp.pad(w_router[l], ((0, 0), (0, LANES - N_EXPERTS))).astype(BF16)

        daq, dak, dav, waq, wak, wavt, xaq = _in_proj(x, row(attn_norm_g[l]), w1, _tile(S, 512))
        memkv = _mem_proj(mem, row(mem_norm_g[l]), w_mem_kv[l].astype(BF16))
        lams = [row(v[l]) for v in (da_lambda_q1, da_lambda_k1, da_lambda_q2, da_lambda_k2)]
        oda = _diff_attn(daq, dak, dav, lams, da_subln_g[l].astype(F32), lam_init, _tile(S, 1024))
        owa = _win_attn(waq, wak, wavt, wa_sink[l].astype(F32), _tile(S, 256))
        x1, h2, aff = _merge(x, oda, owa, xaq, memkv, row(attn_norm_g[l]), wg, w_da_o[l].astype(BF16),
                             w_wa_o[l].astype(BF16), w_xa_o[l].astype(BF16), w_out[l].astype(BF16),
                             row(ffn_norm_g[l]), wr, _tile(S, 512))
        pos = _route(aff, cap)
        delta = _moe(pos, aff, h2, w_exp_gate[l].astype(BF16), w_exp_up[l].astype(BF16),
                     w_exp_down[l].astype(BF16), cap, _tile(w_exp_gate.shape[-1], 2048))
        last = l == depth - 1
        x = _residual(x1, delta, row(final_norm_g) if last else None, _tile(S, 512))
    return x
```

```python
import functools

import numpy as np
import jax
import jax.numpy as jnp
from jax import lax
from jax.experimental import pallas as pl
from jax.experimental.pallas import tpu as pltpu

EPS = 1e-6
BLOCK = 128
WINDOW = 128
DA_HEADS = 8
DA_HEAD_DIM = 64
DA_V_DIM = 2 * DA_HEAD_DIM
WA_HEADS = 16
WA_KV_HEADS = 4
WA_GROUPS = WA_HEADS // WA_KV_HEADS
WA_HEAD_DIM = 64
XA_HEADS = 4
XA_HEAD_DIM = 256
N_BRANCH = 3
N_EXPERTS = 16
EC_FACTOR = 2

LOG2E = 1.4426950408889634
DA_Q_SCALE = DA_HEAD_DIM ** -0.5 * LOG2E
XA_Q_SCALE = XA_HEAD_DIM ** -0.5 * LOG2E
WA_Q_SCALE = WA_HEAD_DIM ** -0.5 * LOG2E
WA_ONES_ROWS = 16
WA_KV_PER_STEP = 4
DA_KEY_CHUNK = 256
DA_ONES_ROWS = 16
DA_DRAIN_STEPS = 2
ROUTE_SEQS_PER_STEP = 4

LANES = 128
VMEM_LIMIT = 56 * 1024 * 1024

F32 = jnp.float32
BF16 = jnp.bfloat16
NEG_INF = float("-inf")


def _cparams(sem, flags=None):
    return pltpu.CompilerParams(dimension_semantics=sem, vmem_limit_bytes=VMEM_LIMIT, flags=flags)


def _rms(xf, g_row):
    ms = jnp.mean(xf * xf, axis=-1, keepdims=True)
    return xf * lax.rsqrt(ms + EPS) * g_row


def _dot(a, b):
    return jnp.dot(a, b, preferred_element_type=F32)


def _dot_nt(a, b):
    return lax.dot_general(a, b, (((1,), (1,)), ((), ())), preferred_element_type=F32)


def _dot_tn(a, b):
    return lax.dot_general(a, b, (((0,), (0,)), ((), ())), preferred_element_type=F32)


def _resident(shape, index_map):
    return pl.BlockSpec(shape, index_map, pipeline_mode=pl.Buffered(1))


def _in_proj_kernel(x_ref, g_ref, w_ref, daq_ref, dak_ref, dav_ref, waq_ref, wak_ref, wavt_ref, xaq_ref):
    d = x_ref.shape[-1]
    tm = x_ref.shape[1]
    h = _rms(x_ref[0], g_ref[...]).astype(BF16)
    col = [0]

    def sec(width):
        r = _dot(h, w_ref[:, col[0]:col[0] + width])
        col[0] += width
        return r

    for ref, scale in ((daq_ref, DA_Q_SCALE), (dak_ref, None), (dav_ref, None)):
        r = sec(d)
        r = (r if scale is None else r * scale).astype(BF16)
        for hh in range(DA_HEADS):
            ref[0, hh] = r[:, hh * LANES:(hh + 1) * LANES]
    waq_ref[0] = (sec(d) * WA_Q_SCALE).astype(BF16)
    wak_ref[0] = sec(d).astype(BF16)
    vt = sec(WA_KV_HEADS * WA_HEAD_DIM).T
    ones = jnp.ones((WA_ONES_ROWS, LANES), BF16)
    for hh in range(WA_KV_HEADS):
        for kb in range(tm // LANES):
            wavt_ref[0, hh, kb, 0:WA_HEAD_DIM, :] = vt[hh * WA_HEAD_DIM:(hh + 1) * WA_HEAD_DIM,
                                                       kb * LANES:(kb + 1) * LANES].astype(BF16)
            wavt_ref[0, hh, kb, WA_HEAD_DIM:, :] = ones
    xaq_ref[0] = (sec(d) * XA_Q_SCALE).astype(BF16)


def _in_proj(x, g, w1, tm):
    B, S, D = x.shape
    n_t = S // tm
    head_shape = jax.ShapeDtypeStruct((B, DA_HEADS, S, LANES), BF16)
    head_spec = pl.BlockSpec((1, DA_HEADS, tm, LANES), lambda b, i: (b, 0, i, 0))
    tok = pl.BlockSpec((1, tm, D), lambda b, i: (b, i, 0))
    tok_shape = jax.ShapeDtypeStruct((B, S, D), BF16)
    vt_rows = WA_HEAD_DIM + WA_ONES_ROWS
    return pl.pallas_call(
        _in_proj_kernel,
        grid=(B, n_t),
        in_specs=[tok, _resident((1, D), lambda b, i: (0, 0)), _resident(w1.shape, lambda b, i: (0, 0))],
        out_specs=[head_spec, head_spec, head_spec, tok, tok,
                   pl.BlockSpec((1, WA_KV_HEADS, tm // LANES, vt_rows, LANES), lambda b, i: (b, 0, i, 0, 0)),
                   tok],
        out_shape=[head_shape, head_shape, head_shape, tok_shape, tok_shape,
                   jax.ShapeDtypeStruct((B, WA_KV_HEADS, S // LANES, vt_rows, LANES), BF16),
                   tok_shape],
        compiler_params=_cparams(("parallel", "parallel")),
        name="in_proj",
    )(x, g, w1)


def _mem_proj_kernel(m_ref, g_ref, w_ref, o_ref):
    h = _rms(m_ref[0], g_ref[...]).astype(BF16)
    o_ref[0] = _dot(h, w_ref[...]).astype(BF16)


def _mem_proj(mem, g, w):
    B, M, D = mem.shape
    N = w.shape[1]
    return pl.pallas_call(
        _mem_proj_kernel,
        grid=(B,),
        in_specs=[pl.BlockSpec((1, M, D), lambda b: (b, 0, 0)),
                  _resident((1, D), lambda b: (0, 0)),
                  _resident((D, N), lambda b: (0, 0))],
        out_specs=pl.BlockSpec((1, M, N), lambda b: (b, 0, 0)),
        out_shape=jax.ShapeDtypeStruct((B, M, N), BF16),
        compiler_params=_cparams(("parallel",)),
        name="mem_proj",
    )(mem, g, w)


def _diff_attn_kernel(lq1_ref, lk1_ref, lq2_ref, lk2_ref, q_ref, k_ref, v_ref, bias_ref, g_ref, o_ref,
                      k1_scr, k2_scr, vt_scr, s_scr, m_scr, e_scr, *, lam_init, n_t):
    S = k_ref.shape[2]
    tq = q_ref.shape[2]
    j = pl.program_id(1)
    n_tiles = pl.num_programs(1) - DA_DRAIN_STEPS
    ck = DA_KEY_CHUNK
    seq, tile = _da_seq_tile(j, n_t, n_tiles)
    prev_seq, _ = _da_seq_tile(j - 1, n_t, n_tiles)

    @pl.when((tile == 0) & (j < n_tiles))
    def _():
        kk = k_ref[0, 0]
        lane = lax.broadcasted_iota(jnp.int32, kk.shape, 1)
        zero = jnp.zeros_like(kk)
        k1_scr[...] = jnp.where(lane < DA_HEAD_DIM, kk, zero)
        k2_scr[...] = jnp.where(lane >= DA_HEAD_DIM, kk, zero)
        vt_scr[seq % 2, 0:DA_V_DIM, :] = v_ref[0, 0].astype(F32).T.astype(BF16)
        vt_scr[seq % 2, DA_V_DIM:, :] = jnp.ones((DA_ONES_ROWS, S), BF16)

    @pl.when(j == 0)
    def _():
        s_scr[...] = jnp.zeros_like(s_scr)
        m_scr[...] = jnp.zeros_like(m_scr)
        e_scr[...] = jnp.ones_like(e_scr)

    lam = (jnp.exp(jnp.sum(lq1_ref[...] * lk1_ref[...], keepdims=True))
           - jnp.exp(jnp.sum(lq2_ref[...] * lk2_ref[...], keepdims=True)) + lam_init)

    q = q_ref[0, 0]
    r0 = (S - tq) - tile * tq
    vt_slot = prev_seq % 2

    o1, l1 = e_scr[0, 0:DA_V_DIM, :], e_scr[0, DA_V_DIM:DA_V_DIM + 1, :]
    o2, l2 = e_scr[1, 0:DA_V_DIM, :], e_scr[1, DA_V_DIM:DA_V_DIM + 1, :]
    ot = o1 * (1.0 / l1) - o2 * (lam / l2)
    ms = jnp.mean(ot * ot, axis=0, keepdims=True)
    y = ot * lax.rsqrt(ms + EPS) * g_ref[...] * (1.0 - lam_init)
    o_ref[0] = y.T.astype(BF16)
    m_prev = (m_scr[0:1, :], m_scr[1:2, :])
    m_new = [None, None]
    acc = [None, None]
    for c in range(S // ck):
        rows = slice(c * ck, (c + 1) * ck)
        for i in range(2):
            p = jnp.exp2((s_scr[i, rows, :] - m_prev[i]).astype(BF16))
            d = _dot(vt_scr[vt_slot, :, rows], p)
            acc[i] = d if acc[i] is None else acc[i] + d
        bias = bias_ref[0, pl.ds(pl.multiple_of(r0 + c * ck, ck), ck), :]
        for i, k_scr in enumerate((k1_scr, k2_scr)):
            s = _dot_nt(k_scr[rows, :], q) + bias
            s_scr[i, rows, :] = s
            cm = jnp.max(s, axis=0, keepdims=True)
            m_new[i] = cm if m_new[i] is None else jnp.maximum(m_new[i], cm)
    m_scr[0:1, :] = m_new[0]
    m_scr[1:2, :] = m_new[1]
    e_scr[0] = acc[0]
    e_scr[1] = acc[1]


def _da_seq_tile(j, n_t, n_tiles):
    jc = jnp.clip(j, 0, n_tiles - 1)
    return jc // n_t, jc % n_t


def _diff_attn(daq, dak, dav, lams, subln_g, lam_init, tq):
    B, H, S, _ = daq.shape
    n_t = S // tq
    rows = 2 * S - tq
    slopes = jnp.exp2(-8.0 * jnp.arange(1, H + 1, dtype=F32) / H)
    r = jnp.arange(rows, dtype=jnp.int32)[:, None]
    c = jnp.arange(tq, dtype=jnp.int32)[None, :]
    dist = jnp.abs(r - (S - tq) - c).astype(F32)
    bias = -((slopes * LOG2E)[:, None, None] * dist[None])
    n_tiles = B * n_t
    lam_spec = _resident((1, DA_HEAD_DIM), lambda h, j: (0, 0))

    def kv_map(h, j):
        return (_da_seq_tile(j, n_t, n_tiles)[0], h, 0, 0)

    def q_map(h, j):
        seq, tile = _da_seq_tile(j, n_t, n_tiles)
        return (seq, h, tile, 0)

    def o_map(h, j):
        seq, tile = _da_seq_tile(j - DA_DRAIN_STEPS, n_t, n_tiles)
        return (seq, tile, h)

    kv_spec = pl.BlockSpec((1, 1, S, LANES), kv_map)
    acc_shape = pltpu.VMEM((2, DA_V_DIM + DA_ONES_ROWS, tq), F32)
    return pl.pallas_call(
        functools.partial(_diff_attn_kernel, lam_init=lam_init, n_t=n_t),
        grid=(H, n_tiles + DA_DRAIN_STEPS),
        in_specs=[lam_spec, lam_spec, lam_spec, lam_spec,
                  pl.BlockSpec((1, 1, tq, LANES), q_map),
                  kv_spec, kv_spec,
                  pl.BlockSpec((1, rows, tq), lambda h, j: (h, 0, 0), pipeline_mode=pl.Buffered(1)),
                  _resident((DA_V_DIM, 1), lambda h, j: (0, 0))],
        out_specs=pl.BlockSpec((1, tq, LANES), o_map),
        out_shape=jax.ShapeDtypeStruct((B, S, H * LANES), BF16),
        scratch_shapes=[pltpu.VMEM((S, LANES), BF16), pltpu.VMEM((S, LANES), BF16),
                        pltpu.VMEM((2, DA_V_DIM + DA_ONES_ROWS, S), BF16),
                        pltpu.VMEM((2, S, tq), F32), pltpu.VMEM((2, tq), F32), acc_shape],
        compiler_params=_cparams(("parallel", "arbitrary")),
        name="diff_attn",
    )(*lams, daq, dak, dav, bias, subln_g.reshape(DA_V_DIM, 1))


def _win_attn_kernel(sink_ref, q_ref, k_ref, vt_ref, tab_ref, o_ref, sa_scr, sb_scr, ma_scr, mb_scr, *, n_t):
    S = k_ref.shape[1]
    tq = q_ref.shape[1]
    span = tq + 2 * WINDOW
    g = pl.program_id(0)
    j = pl.program_id(1)
    n_tiles = pl.num_programs(1) - 1
    _, tile = _da_seq_tile(j, n_t, n_tiles)
    _, prev_tile = _da_seq_tile(j - 1, n_t, n_tiles)

    def window_start(t):
        return pl.multiple_of(jnp.clip(t * tq - WINDOW, 0, S - span), LANES)

    @pl.when(j == 0)
    def _():
        sb_scr[...] = jnp.zeros_like(sb_scr)
        mb_scr[...] = jnp.zeros_like(mb_scr)

    shift = jnp.where(tile == 0, 0, jnp.where(tile == n_t - 1, 2 * WINDOW, WINDOW))
    tab_row = pl.multiple_of(2 * WINDOW - shift, LANES)
    kwin = k_ref[0, pl.ds(window_start(tile), span), :]
    kb0 = window_start(prev_tile) // LANES
    heads = WA_KV_PER_STEP * WA_GROUPS

    def step(s_w, m_w, s_r, m_r):
        outs = []
        for jh in range(heads):
            kv, pair_half = divmod(jh, WA_GROUPS)
            pair, half = divmod(pair_half, 2)
            sink = sink_ref[g * heads + jh] * LOG2E
            m = m_r[jh:jh + 1, :]
            acc = None
            for kb in range(span // LANES):
                p = jnp.exp2((s_r[jh, kb * LANES:(kb + 1) * LANES, :] - m).astype(BF16))
                d = _dot(vt_ref[0, kv, kb0 + kb], p)
                acc = d if acc is None else acc + d
            l = acc[WA_HEAD_DIM:WA_HEAD_DIM + 1, :] + jnp.exp2(sink - m)
            outs.append(acc[0:WA_HEAD_DIM, :] * (1.0 / l))
            qcol = (kv * 2 + pair) * LANES
            kcol = (kv * 2 + half) * LANES
            qp = q_ref[0, :, qcol:qcol + LANES]
            s = _dot_nt(kwin[:, kcol:kcol + LANES], qp) + tab_ref[jh, pl.ds(tab_row, span), :]
            s_w[jh] = s
            m_w[jh:jh + 1, :] = jnp.maximum(jnp.max(s, axis=0, keepdims=True), sink)
        o_ref[0] = jnp.concatenate(outs, axis=0).T.astype(BF16)

    @pl.when(j % 2 == 0)
    def _():
        step(sa_scr, ma_scr, sb_scr, mb_scr)

    @pl.when(j % 2 == 1)
    def _():
        step(sb_scr, mb_scr, sa_scr, ma_scr)


def _win_attn(waq, wak, wavt, sink, tq):
    B, S, D = waq.shape
    n_t = S // tq
    span = tq + 2 * WINDOW
    assert n_t >= 2 and S >= span, (S, tq)
    heads = WA_KV_PER_STEP * WA_GROUPS
    gw = heads * WA_HEAD_DIM
    slopes = jnp.exp2(-8.0 * jnp.arange(1, WA_HEADS + 1, dtype=F32) / WA_HEADS) * LOG2E
    r = jnp.arange(span + 2 * WINDOW, dtype=jnp.int32)[:, None]
    ir = jnp.arange(tq, dtype=jnp.int32)[None, :]
    dist = jnp.abs(2 * WINDOW + ir - r)
    tab = jnp.where((dist <= WINDOW)[None], -(slopes[:, None, None] * dist.astype(F32)[None]), NEG_INF)
    vt_rows = wavt.shape[3]
    n_tiles = B * n_t

    def q_map(g, j):
        seq, tile = _da_seq_tile(j, n_t, n_tiles)
        return (seq, tile, g)

    def o_map(g, j):
        seq, tile = _da_seq_tile(j - 1, n_t, n_tiles)
        return (seq, tile, g)

    return pl.pallas_call(
        functools.partial(_win_attn_kernel, n_t=n_t),
        grid=(WA_KV_HEADS // WA_KV_PER_STEP, n_tiles + 1),
        in_specs=[pl.BlockSpec(memory_space=pltpu.SMEM),
                  pl.BlockSpec((1, tq, gw), q_map),
                  pl.BlockSpec((1, S, gw), lambda g, j: (_da_seq_tile(j, n_t, n_tiles)[0], 0, g)),
                  pl.BlockSpec((1, WA_KV_PER_STEP, S // LANES, vt_rows, LANES),
                               lambda g, j: (_da_seq_tile(j - 1, n_t, n_tiles)[0], g, 0, 0, 0)),
                  pl.BlockSpec((heads, span + 2 * WINDOW, tq), lambda g, j: (g, 0, 0),
                               pipeline_mode=pl.Buffered(1))],
        out_specs=pl.BlockSpec((1, tq, gw), o_map),
        out_shape=jax.ShapeDtypeStruct((B, S, D), BF16),
        scratch_shapes=[pltpu.VMEM((heads, span, tq), F32), pltpu.VMEM((heads, span, tq), F32),
                        pltpu.VMEM((heads, tq), F32), pltpu.VMEM((heads, tq), F32)],
        compiler_params=_cparams(("parallel", "arbitrary")),
        name="win_attn",
    )(sink, waq, wak, wavt, tab)


def _mem_attn_head(q_ref, kv_ref, hh):
    hd = XA_HEAD_DIM
    q = q_ref[0, :, hh * hd:(hh + 1) * hd]
    k = kv_ref[0, :, hh * hd:(hh + 1) * hd]
    v = kv_ref[0, :, (XA_HEADS + hh) * hd:(XA_HEADS + hh + 1) * hd]
    s = _dot_nt(q, k)
    p = jnp.exp2(s - jnp.max(s, axis=1, keepdims=True))
    pn = (p * (1.0 / jnp.sum(p, axis=1, keepdims=True))).astype(BF16)
    return _dot(pn, v).astype(BF16)


def _merge_kernel(x_ref, oda_ref, owa_ref, xaq_ref, memkv_ref, g1_ref, wg_ref, wda_ref, wwa_ref, wxa_ref,
                  wout_ref, g2_ref, wr_ref, x1_ref, h2_ref, aff_ref):
    d = x_ref.shape[-1]
    hd = XA_HEAD_DIM
    x = x_ref[0]
    h = _rms(x, g1_ref[...]).astype(BF16)

    def gated(i, branch):
        return jax.nn.sigmoid(_dot(h, wg_ref[:, i * d:(i + 1) * d])) * branch

    xa = [_mem_attn_head(xaq_ref, memkv_ref, 0)]
    merged = gated(0, _dot(oda_ref[0], wda_ref[...]))
    xa.append(_mem_attn_head(xaq_ref, memkv_ref, 1))
    merged = merged + gated(1, _dot(owa_ref[0], wwa_ref[...]))
    xa.extend(_mem_attn_head(xaq_ref, memkv_ref, hh) for hh in range(2, XA_HEADS))
    xa_proj = None
    for hh, o in enumerate(xa):
        part = _dot(o, wxa_ref[hh * hd:(hh + 1) * hd, :])
        xa_proj = part if xa_proj is None else xa_proj + part
    merged = merged + gated(2, xa_proj)
    x1 = x + _dot(merged.astype(BF16), wout_ref[...])
    x1_ref[0] = x1
    h2 = _rms(x1, g2_ref[...]).astype(BF16)
    h2_ref[0] = h2
    logits = _dot(h2, wr_ref[...])
    lt = logits.T[:N_EXPERTS, :]
    m = jnp.max(lt, axis=0, keepdims=True)
    e = jnp.exp(lt - m)
    aff_ref[0] = e / jnp.sum(e, axis=0, keepdims=True)


def _merge(x, oda, owa, xaq, memkv, g1, wg, wda, wwa, wxa, wout, g2, wr, tm):
    B, S, D = x.shape
    tok = pl.BlockSpec((1, tm, D), lambda b, i: (b, i, 0))
    full = lambda a: _resident(a.shape, lambda b, i: (0,) * a.ndim)
    return pl.pallas_call(
        _merge_kernel,
        grid=(B, S // tm),
        in_specs=[tok, tok, tok, tok, pl.BlockSpec((1,) + memkv.shape[1:], lambda b, i: (b, 0, 0)),
                  full(g1), full(wg), full(wda), full(wwa), full(wxa), full(wout), full(g2), full(wr)],
        out_specs=[tok, tok, pl.BlockSpec((1, N_EXPERTS, tm), lambda b, i: (b, 0, i))],
        out_shape=[jax.ShapeDtypeStruct((B, S, D), F32),
                   jax.ShapeDtypeStruct((B, S, D), BF16),
                   jax.ShapeDtypeStruct((B, N_EXPERTS, S), F32)],
        compiler_params=_cparams(("parallel", "parallel")),
        name="merge",
    )(x, oda, owa, xaq, memkv, g1, wg, wda, wwa, wxa, wout, g2, wr)


def _route_kernel(aff_ref, pos_ref, *, cap):
    nb, ne, S = aff_ref.shape
    E = nb * ne
    aff = aff_ref[...].reshape(E, S)
    bits = pltpu.bitcast(aff, jnp.int32)
    prefix = jnp.zeros((E, 1), jnp.int32)
    for bit in range(30, -1, -1):
        cand = prefix | (1 << bit)
        cnt = jnp.sum(jnp.where(bits >= cand, 1.0, 0.0), axis=1, keepdims=True)
        prefix = jnp.where(cnt >= cap, cand, prefix)
    gt = bits > prefix
    eq = bits == prefix
    blk = 2 * LANES
    tri = (lax.broadcasted_iota(jnp.int32, (blk, blk), 0) <= lax.broadcasted_iota(jnp.int32, (blk, blk), 1))
    tri = jnp.where(tri, 1.0, 0.0).astype(BF16)

    def cumsum(mask):
        mb = jnp.where(mask, 1.0, 0.0).astype(BF16)
        carry = jnp.zeros((E, 1), F32)
        outs = []
        for i in range(S // blk):
            cs = _dot(mb[:, i * blk:(i + 1) * blk], tri) + carry
            outs.append(cs)
            carry = cs[:, blk - 1:blk]
        return jnp.concatenate(outs, axis=1), carry

    cs_gt, n_gt = cumsum(gt)
    cs_eq, _ = cumsum(eq)
    need = cap - n_gt
    sel = gt | (eq & (cs_eq <= need))
    slot = cs_gt + jnp.minimum(cs_eq, need) - 1.0
    pos_ref[...] = jnp.where(sel, slot, -1.0).astype(jnp.int32).reshape(nb, ne, S)


def _route(aff, cap):
    B, E, S = aff.shape
    nb = ROUTE_SEQS_PER_STEP if B % ROUTE_SEQS_PER_STEP == 0 else 1
    return pl.pallas_call(
        functools.partial(_route_kernel, cap=cap),
        grid=(B // nb,),
        in_specs=[pl.BlockSpec((nb, E, S), lambda b: (b, 0, 0))],
        out_specs=pl.BlockSpec((nb, E, S), lambda b: (b, 0, 0)),
        out_shape=jax.ShapeDtypeStruct((B, E, S), jnp.int32),
        compiler_params=_cparams(("parallel",)),
        name="route",
    )(aff)


def _moe_kernel(pos_ref, aff_ref, h2_ref, wg_ref, wu_ref, wd_ref, o_ref, p_scr, g_scr, xe_scr, y_scr, *, cap):
    e = pl.program_id(1)
    f = pl.program_id(2)
    nf = pl.num_programs(2)
    S = h2_ref.shape[1]

    @pl.when((e == 0) & (f == 0))
    def _():
        o_ref[...] = jnp.zeros_like(o_ref)

    @pl.when(f == 0)
    def _():
        slot = lax.broadcasted_iota(jnp.int32, (cap, S), 0)
        hit = pos_ref[0, 0] == slot
        onehot = jnp.where(hit, 1.0, 0.0).astype(BF16)
        p_scr[...] = onehot
        g_scr[...] = jnp.sum(jnp.where(hit, aff_ref[0, 0], 0.0), axis=1, keepdims=True)
        xe_scr[...] = _dot(onehot, h2_ref[0]).astype(BF16)
        y_scr[...] = jnp.zeros_like(y_scr)

    xe = xe_scr[...]
    a = _dot(xe, wg_ref[0])
    u = _dot(xe, wu_ref[0])
    act = (a * jax.nn.sigmoid(a) * u).astype(BF16)
    y_scr[...] += _dot(act, wd_ref[0])

    @pl.when(f == nf - 1)
    def _():
        ye = (y_scr[...] * g_scr[...]).astype(BF16)
        o_ref[0] += _dot_tn(p_scr[...], ye)


def _moe(pos, aff, h2, wg, wu, wd, cap, tf):
    B, S, D = h2.shape
    E, _, F = wg.shape
    pos4 = pos.reshape(B, E, 1, S)
    aff4 = aff.reshape(B, E, 1, S)
    row_spec = pl.BlockSpec((1, 1, 1, S), lambda b, e, f: (b, e, 0, 0))
    return pl.pallas_call(
        functools.partial(_moe_kernel, cap=cap),
        grid=(B, E, F // tf),
        in_specs=[row_spec, row_spec,
                  pl.BlockSpec((1, S, D), lambda b, e, f: (b, 0, 0)),
                  pl.BlockSpec((1, D, tf), lambda b, e, f: (e, 0, f)),
                  pl.BlockSpec((1, D, tf), lambda b, e, f: (e, 0, f)),
                  pl.BlockSpec((1, tf, D), lambda b, e, f: (e, f, 0))],
        out_specs=pl.BlockSpec((1, S, D), lambda b, e, f: (b, 0, 0)),
        out_shape=jax.ShapeDtypeStruct((B, S, D), F32),
        scratch_shapes=[pltpu.VMEM((cap, S), BF16), pltpu.VMEM((cap, 1), F32),
                        pltpu.VMEM((cap, D), BF16), pltpu.VMEM((cap, D), F32)],
        compiler_params=_cparams(("parallel", "arbitrary", "arbitrary")),
        name="moe",
    )(pos4, aff4, h2, wg, wu, wd)


def _residual_kernel(x_ref, d_ref, o_ref):
    o_ref[0] = x_ref[0] + d_ref[0]


def _residual_norm_kernel(x_ref, d_ref, g_ref, o_ref):
    o_ref[0] = _rms(x_ref[0] + d_ref[0], g_ref[...])


def _residual(x1, delta, g, tm):
    B, S, D = x1.shape
    tok = pl.BlockSpec((1, tm, D), lambda b, i: (b, i, 0))
    if g is None:
        body, extra, extra_specs = _residual_kernel, (), []
    else:
        body, extra, extra_specs = _residual_norm_kernel, (g,), [_resident((1, D), lambda b, i: (0, 0))]
    return pl.pallas_call(
        body,
        grid=(B, S // tm),
        in_specs=[tok, tok] + extra_specs,
        out_specs=tok,
        out_shape=jax.ShapeDtypeStruct((B, S, D), F32),
        compiler_params=_cparams(("parallel", "parallel")),
        name="residual",
    )(x1, delta, *extra)


def _tile(n, pref):
    t = min(n, pref)
    assert n % t == 0, (n, t)
    return t


def _lambda_init(layer):
    return 0.8 - 0.6 * float(np.exp(-0.3 * layer))


def _lane_half_forms(w, n_heads, width):
    d = w.shape[0]
    w = w.reshape(d, n_heads, width)
    z = jnp.zeros_like(w)
    return jnp.concatenate([w, z, z, w], axis=-1).reshape(d, n_heads * 4 * width)


def kernel(x, mem, attn_norm_g, mem_norm_g, w_in, w_mem_kv, da_lambda_q1, da_lambda_k1, da_lambda_q2,
           da_lambda_k2, da_subln_g, wa_sink, w_da_o, w_wa_o, w_xa_o, w_out, ffn_norm_g, w_router,
           w_exp_gate, w_exp_up, w_exp_down, final_norm_g):
    B, S, D = x.shape
    depth = w_in.shape[0]
    cap = max(1, EC_FACTOR * S // N_EXPERTS)
    row = lambda v: v.reshape(1, -1).astype(F32)
    da_w = DA_HEADS * 2 * DA_HEAD_DIM
    wa_kv_w = WA_KV_HEADS * WA_HEAD_DIM
    o_waq = 3 * da_w
    o_wak = o_waq + WA_HEADS * WA_HEAD_DIM
    o_wav = o_wak + wa_kv_w
    o_xaq = o_wav + wa_kv_w
    o_gate = o_xaq + XA_HEADS * XA_HEAD_DIM
    for l in range(depth):
        lam_init = _lambda_init(l)
        wl = w_in[l]
        w1 = jnp.concatenate([wl[:, :o_wak],
                              _lane_half_forms(wl[:, o_wak:o_wav], WA_KV_HEADS, WA_HEAD_DIM),
                              wl[:, o_wav:o_gate]], axis=1).astype(BF16)
        wg = wl[:, o_gate:].astype(BF16)
        wr = jnp.pad(w_router[l], ((0, 0), (0, LANES - N_EXPERTS))).astype(BF16)

        daq, dak, dav, waq, wak, wavt, xaq = _in_proj(x, row(attn_norm_g[l]), w1, _tile(S, 512))
        memkv = _mem_proj(mem, row(mem_norm_g[l]), w_mem_kv[l].astype(BF16))
        lams = [row(v[l]) for v in (da_lambda_q1, da_lambda_k1, da_lambda_q2, da_lambda_k2)]
        oda = _diff_attn(daq, dak, dav, lams, da_subln_g[l].astype(F32), lam_init, _tile(S, 1024))
        owa = _win_attn(waq, wak, wavt, wa_sink[l].astype(F32), _tile(S, 256))
        x1, h2, aff = _merge(x, oda, owa, xaq, memkv, row(attn_norm_g[l]), wg, w_da_o[l].astype(BF16),
                             w_wa_o[l].astype(BF16), w_xa_o[l].astype(BF16), w_out[l].astype(BF16),
                             row(ffn_norm_g[l]), wr, _tile(S, 512))
        pos = _route(aff, cap)
        delta = _moe(pos, aff, h2, w_exp_gate[l].astype(BF16), w_exp_up[l].astype(BF16),
                     w_exp_down[l].astype(BF16), cap, _tile(w_exp_gate.shape[-1], 2048))
        last = l == depth - 1
        x = _residual(x1, delta, row(final_norm_g) if last else None, _tile(S, 512))
    return x
```

```python
import functools

import numpy as np
import jax
import jax.numpy as jnp
from jax import lax
from jax.experimental import pallas as pl
from jax.experimental.pallas import tpu as pltpu

EPS = 1e-6
BLOCK = 128
WINDOW = 128
DA_HEADS = 8
DA_HEAD_DIM = 64
DA_V_DIM = 2 * DA_HEAD_DIM
WA_HEADS = 16
WA_KV_HEADS = 4
WA_GROUPS = WA_HEADS // WA_KV_HEADS
WA_HEAD_DIM = 64
XA_HEADS = 4
XA_HEAD_DIM = 256
N_BRANCH = 3
N_EXPERTS = 16
EC_FACTOR = 2

LOG2E = 1.4426950408889634
DA_Q_SCALE = DA_HEAD_DIM ** -0.5 * LOG2E
XA_Q_SCALE = XA_HEAD_DIM ** -0.5 * LOG2E
WA_Q_SCALE = WA_HEAD_DIM ** -0.5 * LOG2E
WA_ONES_ROWS = 16
WA_KV_PER_STEP = 4
DA_KEY_CHUNK = 256
DA_ONES_ROWS = 16
DA_DRAIN_STEPS = 2
ROUTE_SEQS_PER_STEP = 4

LANES = 128
VMEM_LIMIT = 56 * 1024 * 1024

F32 = jnp.float32
BF16 = jnp.bfloat16
NEG_INF = float("-inf")


def _cparams(sem, flags=None):
    return pltpu.CompilerParams(dimension_semantics=sem, vmem_limit_bytes=VMEM_LIMIT, flags=flags)


def _rms(xf, g_row):
    ms = jnp.mean(xf * xf, axis=-1, keepdims=True)
    return xf * lax.rsqrt(ms + EPS) * g_row


def _dot(a, b):
    return jnp.dot(a, b, preferred_element_type=F32)


def _dot_nt(a, b):
    return lax.dot_general(a, b, (((1,), (1,)), ((), ())), preferred_element_type=F32)


def _dot_tn(a, b):
    return lax.dot_general(a, b, (((0,), (0,)), ((), ())), preferred_element_type=F32)


def _resident(shape, index_map):
    return pl.BlockSpec(shape, index_map, pipeline_mode=pl.Buffered(1))


def _in_proj_kernel(x_ref, g_ref, w_ref, daq_ref, dak_ref, dav_ref, waq_ref, wak_ref, wavt_ref, xaq_ref):
    d = x_ref.shape[-1]
    tm = x_ref.shape[1]
    h = _rms(x_ref[0], g_ref[...]).astype(BF16)
    col = [0]

    def sec(width):
        r = _dot(h, w_ref[:, col[0]:col[0] + width])
        col[0] += width
        return r

    for ref, scale in ((daq_ref, DA_Q_SCALE), (dak_ref, None), (dav_ref, None)):
        r = sec(d)
        r = (r if scale is None else r * scale).astype(BF16)
        for hh in range(DA_HEADS):
            ref[0, hh] = r[:, hh * LANES:(hh + 1) * LANES]
    waq_ref[0] = (sec(d) * WA_Q_SCALE).astype(BF16)
    wak_ref[0] = sec(d).astype(BF16)
    vt = sec(WA_KV_HEADS * WA_HEAD_DIM).T
    ones = jnp.ones((WA_ONES_ROWS, LANES), BF16)
    for hh in range(WA_KV_HEADS):
        for kb in range(tm // LANES):
            wavt_ref[0, hh, kb, 0:WA_HEAD_DIM, :] = vt[hh * WA_HEAD_DIM:(hh + 1) * WA_HEAD_DIM,
                                                       kb * LANES:(kb + 1) * LANES].astype(BF16)
            wavt_ref[0, hh, kb, WA_HEAD_DIM:, :] = ones
    xaq_ref[0] = (sec(d) * XA_Q_SCALE).astype(BF16)


def _in_proj(x, g, w1, tm):
    B, S, D = x.shape
    n_t = S // tm
    head_shape = jax.ShapeDtypeStruct((B, DA_HEADS, S, LANES), BF16)
    head_spec = pl.BlockSpec((1, DA_HEADS, tm, LANES), lambda b, i: (b, 0, i, 0))
    tok = pl.BlockSpec((1, tm, D), lambda b, i: (b, i, 0))
    tok_shape = jax.ShapeDtypeStruct((B, S, D), BF16)
    vt_rows = WA_HEAD_DIM + WA_ONES_ROWS
    return pl.pallas_call(
        _in_proj_kernel,
        grid=(B, n_t),
        in_specs=[tok, _resident((1, D), lambda b, i: (0, 0)), _resident(w1.shape, lambda b, i: (0, 0))],
        out_specs=[head_spec, head_spec, head_spec, tok, tok,
                   pl.BlockSpec((1, WA_KV_HEADS, tm // LANES, vt_rows, LANES), lambda b, i: (b, 0, i, 0, 0)),
                   tok],
        out_shape=[head_shape, head_shape, head_shape, tok_shape, tok_shape,
                   jax.ShapeDtypeStruct((B, WA_KV_HEADS, S // LANES, vt_rows, LANES), BF16),
                   tok_shape],
        compiler_params=_cparams(("parallel", "parallel")),
        name="in_proj",
    )(x, g, w1)


def _mem_proj_kernel(m_ref, g_ref, w_ref, o_ref):
    h = _rms(m_ref[0], g_ref[...]).astype(BF16)
    o_ref[0] = _dot(h, w_ref[...]).astype(BF16)


def _mem_proj(mem, g, w):
    B, M, D = mem.shape
    N = w.shape[1]
    return pl.pallas_call(
        _mem_proj_kernel,
        grid=(B,),
        in_specs=[pl.BlockSpec((1, M, D), lambda b: (b, 0, 0)),
                  _resident((1, D), lambda b: (0, 0)),
                  _resident((D, N), lambda b: (0, 0))],
        out_specs=pl.BlockSpec((1, M, N), lambda b: (b, 0, 0)),
        out_shape=jax.ShapeDtypeStruct((B, M, N), BF16),
        compiler_params=_cparams(("parallel",)),
        name="mem_proj",
    )(mem, g, w)


def _diff_attn_kernel(lq1_ref, lk1_ref, lq2_ref, lk2_ref, q_ref, k_ref, v_ref, bias_ref, g_ref, o_ref,
                      k1_scr, k2_scr, vt_scr, s_scr, m_scr, e_scr, *, lam_init, n_t):
    S = k_ref.shape[2]
    tq = q_ref.shape[2]
    j = pl.program_id(1)
    n_tiles = pl.num_programs(1) - DA_DRAIN_STEPS
    ck = DA_KEY_CHUNK
    seq, tile = _da_seq_tile(j, n_t, n_tiles)
    prev_seq, _ = _da_seq_tile(j - 1, n_t, n_tiles)

    @pl.when((tile == 0) & (j < n_tiles))
    def _():
        kk = k_ref[0, 0]
        lane = lax.broadcasted_iota(jnp.int32, kk.shape, 1)
        zero = jnp.zeros_like(kk)
        k1_scr[...] = jnp.where(lane < DA_HEAD_DIM, kk, zero)
        k2_scr[...] = jnp.where(lane >= DA_HEAD_DIM, kk, zero)
        vt_scr[seq % 2, 0:DA_V_DIM, :] = v_ref[0, 0].astype(F32).T.astype(BF16)
        vt_scr[seq % 2, DA_V_DIM:, :] = jnp.ones((DA_ONES_ROWS, S), BF16)

    @pl.when(j == 0)
    def _():
        s_scr[...] = jnp.zeros_like(s_scr)
        m_scr[...] = jnp.zeros_like(m_scr)
        e_scr[...] = jnp.ones_like(e_scr)

    lam = (jnp.exp(jnp.sum(lq1_ref[...] * lk1_ref[...], keepdims=True))
           - jnp.exp(jnp.sum(lq2_ref[...] * lk2_ref[...], keepdims=True)) + lam_init)

    q = q_ref[0, 0]
    r0 = (S - tq) - tile * tq
    vt_slot = prev_seq % 2

    def stage3():
        o1, l1 = e_scr[0, 0:DA_V_DIM, :], e_scr[0, DA_V_DIM:DA_V_DIM + 1, :]
        o2, l2 = e_scr[1, 0:DA_V_DIM, :], e_scr[1, DA_V_DIM:DA_V_DIM + 1, :]
        ot = o1 * (1.0 / l1) - o2 * (lam / l2)
        ms = jnp.mean(ot * ot, axis=0, keepdims=True)
        y = ot * lax.rsqrt(ms + EPS) * g_ref[...] * (1.0 - lam_init)
        o_ref[0] = y.T.astype(BF16)

    m_prev = (m_scr[0:1, :], m_scr[1:2, :])
    m_new = [None, None]
    acc = [None, None]
    n_chunks = S // ck
    for c in range(n_chunks):
        if c == n_chunks // 2:
            stage3()
        rows = slice(c * ck, (c + 1) * ck)
        bias = bias_ref[0, pl.ds(pl.multiple_of(r0 + c * ck, ck), ck), :]
        for i, k_scr in enumerate((k1_scr, k2_scr)):
            p = jnp.exp2((s_scr[i, rows, :] - m_prev[i]).astype(BF16))
            d = _dot(vt_scr[vt_slot, :, rows], p)
            acc[i] = d if acc[i] is None else acc[i] + d
            s = _dot_nt(k_scr[rows, :], q) + bias
            s_scr[i, rows, :] = s
            cm = jnp.max(s, axis=0, keepdims=True)
            m_new[i] = cm if m_new[i] is None else jnp.maximum(m_new[i], cm)
    m_scr[0:1, :] = m_new[0]
    m_scr[1:2, :] = m_new[1]
    e_scr[0] = acc[0]
    e_scr[1] = acc[1]


def _da_seq_tile(j, n_t, n_tiles):
    jc = jnp.clip(j, 0, n_tiles - 1)
    return jc // n_t, jc % n_t


def _diff_attn(daq, dak, dav, lams, subln_g, lam_init, tq):
    B, H, S, _ = daq.shape
    n_t = S // tq
    rows = 2 * S - tq
    slopes = jnp.exp2(-8.0 * jnp.arange(1, H + 1, dtype=F32) / H)
    r = jnp.arange(rows, dtype=jnp.int32)[:, None]
    c = jnp.arange(tq, dtype=jnp.int32)[None, :]
    dist = jnp.abs(r - (S - tq) - c).astype(F32)
    bias = -((slopes * LOG2E)[:, None, None] * dist[None])
    n_tiles = B * n_t
    lam_spec = _resident((1, DA_HEAD_DIM), lambda h, j: (0, 0))

    def kv_map(h, j):
        return (_da_seq_tile(j, n_t, n_tiles)[0], h, 0, 0)

    def q_map(h, j):
        seq, tile = _da_seq_tile(j, n_t, n_tiles)
        return (seq, h, tile, 0)

    def o_map(h, j):
        seq, tile = _da_seq_tile(j - DA_DRAIN_STEPS, n_t, n_tiles)
        return (seq, tile, h)

    kv_spec = pl.BlockSpec((1, 1, S, LANES), kv_map)
    acc_shape = pltpu.VMEM((2, DA_V_DIM + DA_ONES_ROWS, tq), F32)
    return pl.pallas_call(
        functools.partial(_diff_attn_kernel, lam_init=lam_init, n_t=n_t),
        grid=(H, n_tiles + DA_DRAIN_STEPS),
        in_specs=[lam_spec, lam_spec, lam_spec, lam_spec,
                  pl.BlockSpec((1, 1, tq, LANES), q_map),
                  kv_spec, kv_spec,
                  pl.BlockSpec((1, rows, tq), lambda h, j: (h, 0, 0), pipeline_mode=pl.Buffered(1)),
                  _resident((DA_V_DIM, 1), lambda h, j: (0, 0))],
        out_specs=pl.BlockSpec((1, tq, LANES), o_map),
        out_shape=jax.ShapeDtypeStruct((B, S, H * LANES), BF16),
        scratch_shapes=[pltpu.VMEM((S, LANES), BF16), pltpu.VMEM((S, LANES), BF16),
                        pltpu.VMEM((2, DA_V_DIM + DA_ONES_ROWS, S), BF16),
                        pltpu.VMEM((2, S, tq), F32), pltpu.VMEM((2, tq), F32), acc_shape],
        compiler_params=_cparams(("parallel", "arbitrary")),
        name="diff_attn",
    )(*lams, daq, dak, dav, bias, subln_g.reshape(DA_V_DIM, 1))


def _win_attn_kernel(sink_ref, q_ref, k_ref, vt_ref, tab_ref, o_ref, sa_scr, sb_scr, ma_scr, mb_scr, *, n_t):
    S = k_ref.shape[1]
    tq = q_ref.shape[1]
    span = tq + 2 * WINDOW
    g = pl.program_id(0)
    j = pl.program_id(1)
    n_tiles = pl.num_programs(1) - 1
    _, tile = _da_seq_tile(j, n_t, n_tiles)
    _, prev_tile = _da_seq_tile(j - 1, n_t, n_tiles)

    def window_start(t):
        return pl.multiple_of(jnp.clip(t * tq - WINDOW, 0, S - span), LANES)

    @pl.when(j == 0)
    def _():
        sb_scr[...] = jnp.zeros_like(sb_scr)
        mb_scr[...] = jnp.zeros_like(mb_scr)

    shift = jnp.where(tile == 0, 0, jnp.where(tile == n_t - 1, 2 * WINDOW, WINDOW))
    tab_row = pl.multiple_of(2 * WINDOW - shift, LANES)
    kwin = k_ref[0, pl.ds(window_start(tile), span), :]
    kb0 = window_start(prev_tile) // LANES
    heads = WA_KV_PER_STEP * WA_GROUPS

    def step(s_w, m_w, s_r, m_r):
        outs = []
        for jh in range(heads):
            kv, pair_half = divmod(jh, WA_GROUPS)
            pair, half = divmod(pair_half, 2)
            sink = sink_ref[g * heads + jh] * LOG2E
            m = m_r[jh:jh + 1, :]
            acc = None
            for kb in range(span // LANES):
                p = jnp.exp2((s_r[jh, kb * LANES:(kb + 1) * LANES, :] - m).astype(BF16))
                d = _dot(vt_ref[0, kv, kb0 + kb], p)
                acc = d if acc is None else acc + d
            l = acc[WA_HEAD_DIM:WA_HEAD_DIM + 1, :] + jnp.exp2(sink - m)
            outs.append(acc[0:WA_HEAD_DIM, :] * (1.0 / l))
            qcol = (kv * 2 + pair) * LANES
            kcol = (kv * 2 + half) * LANES
            qp = q_ref[0, :, qcol:qcol + LANES]
            s = _dot_nt(kwin[:, kcol:kcol + LANES], qp) + tab_ref[jh, pl.ds(tab_row, span), :]
            s_w[jh] = s
            m_w[jh:jh + 1, :] = jnp.maximum(jnp.max(s, axis=0, keepdims=True), sink)
        o_ref[0] = jnp.concatenate(outs, axis=0).T.astype(BF16)

    @pl.when(j % 2 == 0)
    def _():
        step(sa_scr, ma_scr, sb_scr, mb_scr)

    @pl.when(j % 2 == 1)
    def _():
        step(sb_scr, mb_scr, sa_scr, ma_scr)


def _win_attn(waq, wak, wavt, sink, tq):
    B, S, D = waq.shape
    n_t = S // tq
    span = tq + 2 * WINDOW
    assert n_t >= 2 and S >= span, (S, tq)
    heads = WA_KV_PER_STEP * WA_GROUPS
    gw = heads * WA_HEAD_DIM
    slopes = jnp.exp2(-8.0 * jnp.arange(1, WA_HEADS + 1, dtype=F32) / WA_HEADS) * LOG2E
    r = jnp.arange(span + 2 * WINDOW, dtype=jnp.int32)[:, None]
    ir = jnp.arange(tq, dtype=jnp.int32)[None, :]
    dist = jnp.abs(2 * WINDOW + ir - r)
    tab = jnp.where((dist <= WINDOW)[None], -(slopes[:, None, None] * dist.astype(F32)[None]), NEG_INF)
    vt_rows = wavt.shape[3]
    n_tiles = B * n_t

    def q_map(g, j):
        seq, tile = _da_seq_tile(j, n_t, n_tiles)
        return (seq, tile, g)

    def o_map(g, j):
        seq, tile = _da_seq_tile(j - 1, n_t, n_tiles)
        return (seq, tile, g)

    return pl.pallas_call(
        functools.partial(_win_attn_kernel, n_t=n_t),
        grid=(WA_KV_HEADS // WA_KV_PER_STEP, n_tiles + 1),
        in_specs=[pl.BlockSpec(memory_space=pltpu.SMEM),
                  pl.BlockSpec((1, tq, gw), q_map),
                  pl.BlockSpec((1, S, gw), lambda g, j: (_da_seq_tile(j, n_t, n_tiles)[0], 0, g)),
                  pl.BlockSpec((1, WA_KV_PER_STEP, S // LANES, vt_rows, LANES),
                               lambda g, j: (_da_seq_tile(j - 1, n_t, n_tiles)[0], g, 0, 0, 0)),
                  pl.BlockSpec((heads, span + 2 * WINDOW, tq), lambda g, j: (g, 0, 0),
                               pipeline_mode=pl.Buffered(1))],
        out_specs=pl.BlockSpec((1, tq, gw), o_map),
        out_shape=jax.ShapeDtypeStruct((B, S, D), BF16),
        scratch_shapes=[pltpu.VMEM((heads, span, tq), F32), pltpu.VMEM((heads, span, tq), F32),
                        pltpu.VMEM((heads, tq), F32), pltpu.VMEM((heads, tq), F32)],
        compiler_params=_cparams(("parallel", "arbitrary")),
        name="win_attn",
    )(sink, waq, wak, wavt, tab)


def _mem_attn_head(q_ref, kv_ref, hh):
    hd = XA_HEAD_DIM
    q = q_ref[0, :, hh * hd:(hh + 1) * hd]
    k = kv_ref[0, :, hh * hd:(hh + 1) * hd]
    v = kv_ref[0, :, (XA_HEADS + hh) * hd:(XA_HEADS + hh + 1) * hd]
    s = _dot_nt(q, k)
    p = jnp.exp2(s - jnp.max(s, axis=1, keepdims=True))
    pn = (p * (1.0 / jnp.sum(p, axis=1, keepdims=True))).astype(BF16)
    return _dot(pn, v).astype(BF16)


def _merge_kernel(x_ref, oda_ref, owa_ref, xaq_ref, memkv_ref, g1_ref, wg_ref, wda_ref, wwa_ref, wxa_ref,
                  wout_ref, g2_ref, wr_ref, x1_ref, h2_ref, aff_ref):
    d = x_ref.shape[-1]
    hd = XA_HEAD_DIM
    x = x_ref[0]
    h = _rms(x, g1_ref[...]).astype(BF16)

    def gated(i, branch):
        return jax.nn.sigmoid(_dot(h, wg_ref[:, i * d:(i + 1) * d])) * branch

    xa = [_mem_attn_head(xaq_ref, memkv_ref, 0)]
    merged = gated(0, _dot(oda_ref[0], wda_ref[...]))
    xa.append(_mem_attn_head(xaq_ref, memkv_ref, 1))
    merged = merged + gated(1, _dot(owa_ref[0], wwa_ref[...]))
    xa.extend(_mem_attn_head(xaq_ref, memkv_ref, hh) for hh in range(2, XA_HEADS))
    xa_proj = None
    for hh, o in enumerate(xa):
        part = _dot(o, wxa_ref[hh * hd:(hh + 1) * hd, :])
        xa_proj = part if xa_proj is None else xa_proj + part
    merged = merged + gated(2, xa_proj)
    x1 = x + _dot(merged.astype(BF16), wout_ref[...])
    x1_ref[0] = x1
    h2 = _rms(x1, g2_ref[...]).astype(BF16)
    h2_ref[0] = h2
    logits = _dot(h2, wr_ref[...])
    lt = logits.T[:N_EXPERTS, :]
    m = jnp.max(lt, axis=0, keepdims=True)
    e = jnp.exp(lt - m)
    aff_ref[0] = e / jnp.sum(e, axis=0, keepdims=True)


def _merge(x, oda, owa, xaq, memkv, g1, wg, wda, wwa, wxa, wout, g2, wr, tm):
    B, S, D = x.shape
    tok = pl.BlockSpec((1, tm, D), lambda b, i: (b, i, 0))
    full = lambda a: _resident(a.shape, lambda b, i: (0,) * a.ndim)
    return pl.pallas_call(
        _merge_kernel,
        grid=(B, S // tm),
        in_specs=[tok, tok, tok, tok, pl.BlockSpec((1,) + memkv.shape[1:], lambda b, i: (b, 0, 0)),
                  full(g1), full(wg), full(wda), full(wwa), full(wxa), full(wout), full(g2), full(wr)],
        out_specs=[tok, tok, pl.BlockSpec((1, N_EXPERTS, tm), lambda b, i: (b, 0, i))],
        out_shape=[jax.ShapeDtypeStruct((B, S, D), F32),
                   jax.ShapeDtypeStruct((B, S, D), BF16),
                   jax.ShapeDtypeStruct((B, N_EXPERTS, S), F32)],
        compiler_params=_cparams(("parallel", "parallel")),
        name="merge",
    )(x, oda, owa, xaq, memkv, g1, wg, wda, wwa, wxa, wout, g2, wr)


def _route_kernel(aff_ref, pos_ref, *, cap):
    nb, ne, S = aff_ref.shape
    E = nb * ne
    aff = aff_ref[...].reshape(E, S)
    bits = pltpu.bitcast(aff, jnp.int32)
    prefix = jnp.zeros((E, 1), jnp.int32)
    for bit in range(30, -1, -1):
        cand = prefix | (1 << bit)
        cnt = jnp.sum(jnp.where(bits >= cand, 1.0, 0.0), axis=1, keepdims=True)
        prefix = jnp.where(cnt >= cap, cand, prefix)
    gt = bits > prefix
    eq = bits == prefix
    blk = 2 * LANES
    tri = (lax.broadcasted_iota(jnp.int32, (blk, blk), 0) <= lax.broadcasted_iota(jnp.int32, (blk, blk), 1))
    tri = jnp.where(tri, 1.0, 0.0).astype(BF16)

    def cumsum(mask):
        mb = jnp.where(mask, 1.0, 0.0).astype(BF16)
        carry = jnp.zeros((E, 1), F32)
        outs = []
        for i in range(S // blk):
            cs = _dot(mb[:, i * blk:(i + 1) * blk], tri) + carry
            outs.append(cs)
            carry = cs[:, blk - 1:blk]
        return jnp.concatenate(outs, axis=1), carry

    cs_gt, n_gt = cumsum(gt)
    cs_eq, _ = cumsum(eq)
    need = cap - n_gt
    sel = gt | (eq & (cs_eq <= need))
    slot = cs_gt + jnp.minimum(cs_eq, need) - 1.0
    pos_ref[...] = jnp.where(sel, slot, -1.0).astype(jnp.int32).reshape(nb, ne, S)


def _route(aff, cap):
    B, E, S = aff.shape
    nb = ROUTE_SEQS_PER_STEP if B % ROUTE_SEQS_PER_STEP == 0 else 1
    return pl.pallas_call(
        functools.partial(_route_kernel, cap=cap),
        grid=(B // nb,),
        in_specs=[pl.BlockSpec((nb, E, S), lambda b: (b, 0, 0))],
        out_specs=pl.BlockSpec((nb, E, S), lambda b: (b, 0, 0)),
        out_shape=jax.ShapeDtypeStruct((B, E, S), jnp.int32),
        compiler_params=_cparams(("parallel",)),
        name="route",
    )(aff)


def _moe_kernel(pos_ref, aff_ref, h2_ref, wg_ref, wu_ref, wd_ref, o_ref, p_scr, g_scr, xe_scr, y_scr, *, cap):
    e = pl.program_id(1)
    f = pl.program_id(2)
    nf = pl.num_programs(2)
    S = h2_ref.shape[1]

    @pl.when((e == 0) & (f == 0))
    def _():
        o_ref[...] = jnp.zeros_like(o_ref)

    @pl.when(f == 0)
    def _():
        slot = lax.broadcasted_iota(jnp.int32, (cap, S), 0)
        hit = pos_ref[0, 0] == slot
        onehot = jnp.where(hit, 1.0, 0.0).astype(BF16)
        p_scr[...] = onehot
        g_scr[...] = jnp.sum(jnp.where(hit, aff_ref[0, 0], 0.0), axis=1, keepdims=True)
        xe_scr[...] = _dot(onehot, h2_ref[0]).astype(BF16)
        y_scr[...] = jnp.zeros_like(y_scr)

    xe = xe_scr[...]
    a = _dot(xe, wg_ref[0])
    u = _dot(xe, wu_ref[0])
    act = (a * jax.nn.sigmoid(a) * u).astype(BF16)
    y_scr[...] += _dot(act, wd_ref[0])

    @pl.when(f == nf - 1)
    def _():
        ye = (y_scr[...] * g_scr[...]).astype(BF16)
        o_ref[0] += _dot_tn(p_scr[...], ye)


def _moe(pos, aff, h2, wg, wu, wd, cap, tf):
    B, S, D = h2.shape
    E, _, F = wg.shape
    pos4 = pos.reshape(B, E, 1, S)
    aff4 = aff.reshape(B, E, 1, S)
    row_spec = pl.BlockSpec((1, 1, 1, S), lambda b, e, f: (b, e, 0, 0))
    return pl.pallas_call(
        functools.partial(_moe_kernel, cap=cap),
        grid=(B, E, F // tf),
        in_specs=[row_spec, row_spec,
                  pl.BlockSpec((1, S, D), lambda b, e, f: (b, 0, 0)),
                  pl.BlockSpec((1, D, tf), lambda b, e, f: (e, 0, f)),
                  pl.BlockSpec((1, D, tf), lambda b, e, f: (e, 0, f)),
                  pl.BlockSpec((1, tf, D), lambda b, e, f: (e, f, 0))],
        out_specs=pl.BlockSpec((1, S, D), lambda b, e, f: (b, 0, 0)),
        out_shape=jax.ShapeDtypeStruct((B, S, D), F32),
        scratch_shapes=[pltpu.VMEM((cap, S), BF16), pltpu.VMEM((cap, 1), F32),
                        pltpu.VMEM((cap, D), BF16), pltpu.VMEM((cap, D), F32)],
        compiler_params=_cparams(("parallel", "arbitrary", "arbitrary")),
        name="moe",
    )(pos4, aff4, h2, wg, wu, wd)


def _residual_kernel(x_ref, d_ref, o_ref):
    o_ref[0] = x_ref[0] + d_ref[0]


def _residual_norm_kernel(x_ref, d_ref, g_ref, o_ref):
    o_ref[0] = _rms(x_ref[0] + d_ref[0], g_ref[...])


def _residual(x1, delta, g, tm):
    B, S, D = x1.shape
    tok = pl.BlockSpec((1, tm, D), lambda b, i: (b, i, 0))
    if g is None:
        body, extra, extra_specs = _residual_kernel, (), []
    else:
        body, extra, extra_specs = _residual_norm_kernel, (g,), [_resident((1, D), lambda b, i: (0, 0))]
    return pl.pallas_call(
        body,
        grid=(B, S // tm),
        in_specs=[tok, tok] + extra_specs,
        out_specs=tok,
        out_shape=jax.ShapeDtypeStruct((B, S, D), F32),
        compiler_params=_cparams(("parallel", "parallel")),
        name="residual",
    )(x1, delta, *extra)


def _tile(n, pref):
    t = min(n, pref)
    assert n % t == 0, (n, t)
    return t


def _lambda_init(layer):
    return 0.8 - 0.6 * float(np.exp(-0.3 * layer))


def _lane_half_forms(w, n_heads, width):
    d = w.shape[0]
    w = w.reshape(d, n_heads, width)
    z = jnp.zeros_like(w)
    return jnp.concatenate([w, z, z, w], axis=-1).reshape(d, n_heads * 4 * width)


def kernel(x, mem, attn_norm_g, mem_norm_g, w_in, w_mem_kv, da_lambda_q1, da_lambda_k1, da_lambda_q2,
           da_lambda_k2, da_subln_g, wa_sink, w_da_o, w_wa_o, w_xa_o, w_out, ffn_norm_g, w_router,
           w_exp_gate, w_exp_up, w_exp_down, final_norm_g):
    B, S, D = x.shape
    depth = w_in.shape[0]
    cap = max(1, EC_FACTOR * S // N_EXPERTS)
    row = lambda v: v.reshape(1, -1).astype(F32)
    da_w = DA_HEADS * 2 * DA_HEAD_DIM
    wa_kv_w = WA_KV_HEADS * WA_HEAD_DIM
    o_waq = 3 * da_w
    o_wak = o_waq + WA_HEADS * WA_HEAD_DIM
    o_wav = o_wak + wa_kv_w
    o_xaq = o_wav + wa_kv_w
    o_gate = o_xaq + XA_HEADS * XA_HEAD_DIM
    for l in range(depth):
        lam_init = _lambda_init(l)
        wl = w_in[l]
        w1 = jnp.concatenate([wl[:, :o_wak],
                              _lane_half_forms(wl[:, o_wak:o_wav], WA_KV_HEADS, WA_HEAD_DIM),
                              wl[:, o_wav:o_gate]], axis=1).astype(BF16)
        wg = wl[:, o_gate:].astype(BF16)
        wr = jnp.pad(w_router[l], ((0, 0), (0, LANES - N_EXPERTS))).astype(BF16)

        daq, dak, dav, waq, wak, wavt, xaq = _in_proj(x, row(attn_norm_g[l]), w1, _tile(S, 512))
        memkv = _mem_proj(mem, row(mem_norm_g[l]), w_mem_kv[l].astype(BF16))
        lams = [row(v[l]) for v in (da_lambda_q1, da_lambda_k1, da_lambda_q2, da_lambda_k2)]
        oda = _diff_attn(daq, dak, dav, lams, da_subln_g[l].astype(F32), lam_init, _tile(S, 1024))
        owa = _win_attn(waq, wak, wavt, wa_sink[l].astype(F32), _tile(S, 256))
        x1, h2, aff = _merge(x, oda, owa, xaq, memkv, row(attn_norm_g[l]), wg, w_da_o[l].astype(BF16),
                             w_wa_o[l].astype(BF16), w_xa_o[l].astype(BF16), w_out[l].astype(BF16),
                             row(ffn_norm_g[l]), wr, _tile(S, 512))
        pos = _route(aff, cap)
        delta = _moe(pos, aff, h2, w_exp_gate[l].astype(BF16), w_exp_up[l].astype(BF16),
                     w_exp_down[l].astype(BF16), cap, _tile(w_exp_gate.shape[-1], 2048))
        last = l == depth - 1
        x = _residual(x1, delta, row(final_norm_g) if last else None, _tile(S, 512))
    return x
```

```python
import functools

import numpy as np
import jax
import jax.numpy as jnp
from jax import lax
from jax.experimental import pallas as pl
from jax.experimental.pallas import tpu as pltpu

EPS = 1e-6
BLOCK = 128
WINDOW = 128
DA_HEADS = 8
DA_HEAD_DIM = 64
DA_V_DIM = 2 * DA_HEAD_DIM
WA_HEADS = 16
WA_KV_HEADS = 4
WA_GROUPS = WA_HEADS // WA_KV_HEADS
WA_HEAD_DIM = 64
XA_HEADS = 4
XA_HEAD_DIM = 256
N_BRANCH = 3
N_EXPERTS = 16
EC_FACTOR = 2

LOG2E = 1.4426950408889634
DA_Q_SCALE = DA_HEAD_DIM ** -0.5 * LOG2E
XA_Q_SCALE = XA_HEAD_DIM ** -0.5 * LOG2E
WA_Q_SCALE = WA_HEAD_DIM ** -0.5 * LOG2E
WA_ONES_ROWS = 16
WA_KV_PER_STEP = 4
DA_KEY_CHUNK = 256
DA_ONES_ROWS = 16
DA_DRAIN_STEPS = 2
ROUTE_SEQS_PER_STEP = 4

LANES = 128
VMEM_LIMIT = 56 * 1024 * 1024

F32 = jnp.float32
BF16 = jnp.bfloat16
NEG_INF = float("-inf")


def _cparams(sem, flags=None):
    return pltpu.CompilerParams(dimension_semantics=sem, vmem_limit_bytes=VMEM_LIMIT, flags=flags)


def _rms(xf, g_row):
    ms = jnp.mean(xf * xf, axis=-1, keepdims=True)
    return xf * lax.rsqrt(ms + EPS) * g_row


def _dot(a, b):
    return jnp.dot(a, b, preferred_element_type=F32)


def _dot_nt(a, b):
    return lax.dot_general(a, b, (((1,), (1,)), ((), ())), preferred_element_type=F32)


def _dot_tn(a, b):
    return lax.dot_general(a, b, (((0,), (0,)), ((), ())), preferred_element_type=F32)


def _resident(shape, index_map):
    return pl.BlockSpec(shape, index_map, pipeline_mode=pl.Buffered(1))


def _in_proj_kernel(x_ref, g_ref, w_ref, daq_ref, dak_ref, dav_ref, waq_ref, wak_ref, wavt_ref, xaq_ref):
    d = x_ref.shape[-1]
    tm = x_ref.shape[1]
    h = _rms(x_ref[0], g_ref[...]).astype(BF16)
    col = [0]

    def sec(width):
        r = _dot(h, w_ref[:, col[0]:col[0] + width])
        col[0] += width
        return r

    for ref, scale in ((daq_ref, DA_Q_SCALE), (dak_ref, None), (dav_ref, None)):
        r = sec(d)
        r = (r if scale is None else r * scale).astype(BF16)
        for hh in range(DA_HEADS):
            ref[0, hh] = r[:, hh * LANES:(hh + 1) * LANES]
    waq_ref[0] = (sec(d) * WA_Q_SCALE).astype(BF16)
    wak_ref[0] = sec(d).astype(BF16)
    vt = sec(WA_KV_HEADS * WA_HEAD_DIM).T
    ones = jnp.ones((WA_ONES_ROWS, LANES), BF16)
    for hh in range(WA_KV_HEADS):
        for kb in range(tm // LANES):
            wavt_ref[0, hh, kb, 0:WA_HEAD_DIM, :] = vt[hh * WA_HEAD_DIM:(hh + 1) * WA_HEAD_DIM,
                                                       kb * LANES:(kb + 1) * LANES].astype(BF16)
            wavt_ref[0, hh, kb, WA_HEAD_DIM:, :] = ones
    xaq_ref[0] = (sec(d) * XA_Q_SCALE).astype(BF16)


def _in_proj(x, g, w1, tm):
    B, S, D = x.shape
    n_t = S // tm
    head_shape = jax.ShapeDtypeStruct((B, DA_HEADS, S, LANES), BF16)
    head_spec = pl.BlockSpec((1, DA_HEADS, tm, LANES), lambda b, i: (b, 0, i, 0))
    tok = pl.BlockSpec((1, tm, D), lambda b, i: (b, i, 0))
    tok_shape = jax.ShapeDtypeStruct((B, S, D), BF16)
    vt_rows = WA_HEAD_DIM + WA_ONES_ROWS
    return pl.pallas_call(
        _in_proj_kernel,
        grid=(B, n_t),
        in_specs=[tok, _resident((1, D), lambda b, i: (0, 0)), _resident(w1.shape, lambda b, i: (0, 0))],
        out_specs=[head_spec, head_spec, head_spec, tok, tok,
                   pl.BlockSpec((1, WA_KV_HEADS, tm // LANES, vt_rows, LANES), lambda b, i: (b, 0, i, 0, 0)),
                   tok],
        out_shape=[head_shape, head_shape, head_shape, tok_shape, tok_shape,
                   jax.ShapeDtypeStruct((B, WA_KV_HEADS, S // LANES, vt_rows, LANES), BF16),
                   tok_shape],
        compiler_params=_cparams(("parallel", "parallel")),
        name="in_proj",
    )(x, g, w1)


def _mem_proj_kernel(m_ref, g_ref, w_ref, o_ref):
    h = _rms(m_ref[0], g_ref[...]).astype(BF16)
    o_ref[0] = _dot(h, w_ref[...]).astype(BF16)


def _mem_proj(mem, g, w):
    B, M, D = mem.shape
    N = w.shape[1]
    return pl.pallas_call(
        _mem_proj_kernel,
        grid=(B,),
        in_specs=[pl.BlockSpec((1, M, D), lambda b: (b, 0, 0)),
                  _resident((1, D), lambda b: (0, 0)),
                  _resident((D, N), lambda b: (0, 0))],
        out_specs=pl.BlockSpec((1, M, N), lambda b: (b, 0, 0)),
        out_shape=jax.ShapeDtypeStruct((B, M, N), BF16),
        compiler_params=_cparams(("parallel",)),
        name="mem_proj",
    )(mem, g, w)


def _diff_attn_kernel(lq1_ref, lk1_ref, lq2_ref, lk2_ref, q_ref, k_ref, v_ref, bias_ref, g_ref, o_ref,
                      k1_scr, k2_scr, vt_scr, s_scr, m_scr, e_scr, *, lam_init, n_t):
    S = k_ref.shape[2]
    tq = q_ref.shape[2]
    j = pl.program_id(1)
    n_tiles = pl.num_programs(1) - DA_DRAIN_STEPS
    ck = DA_KEY_CHUNK
    seq, tile = _da_seq_tile(j, n_t, n_tiles)
    prev_seq, _ = _da_seq_tile(j - 1, n_t, n_tiles)

    @pl.when((tile == 0) & (j < n_tiles))
    def _():
        kk = k_ref[0, 0]
        lane = lax.broadcasted_iota(jnp.int32, kk.shape, 1)
        zero = jnp.zeros_like(kk)
        k1_scr[...] = jnp.where(lane < DA_HEAD_DIM, kk, zero)
        k2_scr[...] = jnp.where(lane >= DA_HEAD_DIM, kk, zero)
        vt_scr[seq % 2, 0:DA_V_DIM, :] = v_ref[0, 0].astype(F32).T.astype(BF16)
        vt_scr[seq % 2, DA_V_DIM:, :] = jnp.ones((DA_ONES_ROWS, S), BF16)

    @pl.when(j == 0)
    def _():
        s_scr[...] = jnp.zeros_like(s_scr)
        m_scr[...] = jnp.zeros_like(m_scr)
        e_scr[...] = jnp.ones_like(e_scr)

    lam = (jnp.exp(jnp.sum(lq1_ref[...] * lk1_ref[...], keepdims=True))
           - jnp.exp(jnp.sum(lq2_ref[...] * lk2_ref[...], keepdims=True)) + lam_init)

    q = q_ref[0, 0]
    r0 = (S - tq) - tile * tq
    vt_slot = prev_seq % 2

    def stage3():
        o1, l1 = e_scr[0, 0:DA_V_DIM, :], e_scr[0, DA_V_DIM:DA_V_DIM + 1, :]
        o2, l2 = e_scr[1, 0:DA_V_DIM, :], e_scr[1, DA_V_DIM:DA_V_DIM + 1, :]
        ot = o1 * (1.0 / l1) - o2 * (lam / l2)
        ms = jnp.mean(ot * ot, axis=0, keepdims=True)
        y = ot * lax.rsqrt(ms + EPS) * g_ref[...] * (1.0 - lam_init)
        o_ref[0] = y.T.astype(BF16)

    m_prev = (m_scr[0:1, :], m_scr[1:2, :])
    m_new = [None, None]
    acc = [None, None]
    n_chunks = S // ck
    for c in range(n_chunks):
        if c == n_chunks // 2:
            stage3()
        rows = slice(c * ck, (c + 1) * ck)
        bias = bias_ref[0, pl.ds(pl.multiple_of(r0 + c * ck, ck), ck), :]
        half = ck // 2
        for i, k_scr in enumerate((k1_scr, k2_scr)):
            p_lo = jnp.exp2((s_scr[i, c * ck:c * ck + half, :] - m_prev[i]).astype(BF16))
            s = _dot_nt(k_scr[rows, :], q) + bias
            p_hi = jnp.exp2((s_scr[i, c * ck + half:(c + 1) * ck, :] - m_prev[i]).astype(BF16))
            d = _dot(vt_scr[vt_slot, :, rows], jnp.concatenate([p_lo, p_hi], axis=0))
            acc[i] = d if acc[i] is None else acc[i] + d
            s_scr[i, rows, :] = s
            cm = jnp.max(s, axis=0, keepdims=True)
            m_new[i] = cm if m_new[i] is None else jnp.maximum(m_new[i], cm)
    m_scr[0:1, :] = m_new[0]
    m_scr[1:2, :] = m_new[1]
    e_scr[0] = acc[0]
    e_scr[1] = acc[1]


def _da_seq_tile(j, n_t, n_tiles):
    jc = jnp.clip(j, 0, n_tiles - 1)
    return jc // n_t, jc % n_t


def _diff_attn(daq, dak, dav, lams, subln_g, lam_init, tq):
    B, H, S, _ = daq.shape
    n_t = S // tq
    rows = 2 * S - tq
    slopes = jnp.exp2(-8.0 * jnp.arange(1, H + 1, dtype=F32) / H)
    r = jnp.arange(rows, dtype=jnp.int32)[:, None]
    c = jnp.arange(tq, dtype=jnp.int32)[None, :]
    dist = jnp.abs(r - (S - tq) - c).astype(F32)
    bias = -((slopes * LOG2E)[:, None, None] * dist[None])
    n_tiles = B * n_t
    lam_spec = _resident((1, DA_HEAD_DIM), lambda h, j: (0, 0))

    def kv_map(h, j):
        return (_da_seq_tile(j, n_t, n_tiles)[0], h, 0, 0)

    def q_map(h, j):
        seq, tile = _da_seq_tile(j, n_t, n_tiles)
        return (seq, h, tile, 0)

    def o_map(h, j):
        seq, tile = _da_seq_tile(j - DA_DRAIN_STEPS, n_t, n_tiles)
        return (seq, tile, h)

    kv_spec = pl.BlockSpec((1, 1, S, LANES), kv_map)
    acc_shape = pltpu.VMEM((2, DA_V_DIM + DA_ONES_ROWS, tq), F32)
    return pl.pallas_call(
        functools.partial(_diff_attn_kernel, lam_init=lam_init, n_t=n_t),
        grid=(H, n_tiles + DA_DRAIN_STEPS),
        in_specs=[lam_spec, lam_spec, lam_spec, lam_spec,
                  pl.BlockSpec((1, 1, tq, LANES), q_map),
                  kv_spec, kv_spec,
                  pl.BlockSpec((1, rows, tq), lambda h, j: (h, 0, 0), pipeline_mode=pl.Buffered(1)),
                  _resident((DA_V_DIM, 1), lambda h, j: (0, 0))],
        out_specs=pl.BlockSpec((1, tq, LANES), o_map),
        out_shape=jax.ShapeDtypeStruct((B, S, H * LANES), BF16),
        scratch_shapes=[pltpu.VMEM((S, LANES), BF16), pltpu.VMEM((S, LANES), BF16),
                        pltpu.VMEM((2, DA_V_DIM + DA_ONES_ROWS, S), BF16),
                        pltpu.VMEM((2, S, tq), F32), pltpu.VMEM((2, tq), F32), acc_shape],
        compiler_params=_cparams(("parallel", "arbitrary")),
        name="diff_attn",
    )(*lams, daq, dak, dav, bias, subln_g.reshape(DA_V_DIM, 1))


def _win_attn_kernel(sink_ref, q_ref, k_ref, vt_ref, tab_ref, o_ref, sa_scr, sb_scr, ma_scr, mb_scr, *, n_t):
    S = k_ref.shape[1]
    tq = q_ref.shape[1]
    span = tq + 2 * WINDOW
    g = pl.program_id(0)
    j = pl.program_id(1)
    n_tiles = pl.num_programs(1) - 1
    _, tile = _da_seq_tile(j, n_t, n_tiles)
    _, prev_tile = _da_seq_tile(j - 1, n_t, n_tiles)

    def window_start(t):
        return pl.multiple_of(jnp.clip(t * tq - WINDOW, 0, S - span), LANES)

    @pl.when(j == 0)
    def _():
        sb_scr[...] = jnp.zeros_like(sb_scr)
        mb_scr[...] = jnp.zeros_like(mb_scr)

    shift = jnp.where(tile == 0, 0, jnp.where(tile == n_t - 1, 2 * WINDOW, WINDOW))
    tab_row = pl.multiple_of(2 * WINDOW - shift, LANES)
    kwin = k_ref[0, pl.ds(window_start(tile), span), :]
    kb0 = window_start(prev_tile) // LANES
    heads = WA_KV_PER_STEP * WA_GROUPS

    def step(s_w, m_w, s_r, m_r):
        outs = []
        for jh in range(heads):
            kv, pair_half = divmod(jh, WA_GROUPS)
            pair, half = divmod(pair_half, 2)
            sink = sink_ref[g * heads + jh] * LOG2E
            m = m_r[jh:jh + 1, :]
            acc = None
            for kb in range(span // LANES):
                p = jnp.exp2((s_r[jh, kb * LANES:(kb + 1) * LANES, :] - m).astype(BF16))
                d = _dot(vt_ref[0, kv, kb0 + kb], p)
                acc = d if acc is None else acc + d
            l = acc[WA_HEAD_DIM:WA_HEAD_DIM + 1, :] + jnp.exp2(sink - m)
            outs.append(acc[0:WA_HEAD_DIM, :] * (1.0 / l))
            qcol = (kv * 2 + pair) * LANES
            kcol = (kv * 2 + half) * LANES
            qp = q_ref[0, :, qcol:qcol + LANES]
            s = _dot_nt(kwin[:, kcol:kcol + LANES], qp) + tab_ref[jh, pl.ds(tab_row, span), :]
            s_w[jh] = s
            m_w[jh:jh + 1, :] = jnp.maximum(jnp.max(s, axis=0, keepdims=True), sink)
        o_ref[0] = jnp.concatenate(outs, axis=0).T.astype(BF16)

    @pl.when(j % 2 == 0)
    def _():
        step(sa_scr, ma_scr, sb_scr, mb_scr)

    @pl.when(j % 2 == 1)
    def _():
        step(sb_scr, mb_scr, sa_scr, ma_scr)


def _win_attn(waq, wak, wavt, sink, tq):
    B, S, D = waq.shape
    n_t = S // tq
    span = tq + 2 * WINDOW
    assert n_t >= 2 and S >= span, (S, tq)
    heads = WA_KV_PER_STEP * WA_GROUPS
    gw = heads * WA_HEAD_DIM
    slopes = jnp.exp2(-8.0 * jnp.arange(1, WA_HEADS + 1, dtype=F32) / WA_HEADS) * LOG2E
    r = jnp.arange(span + 2 * WINDOW, dtype=jnp.int32)[:, None]
    ir = jnp.arange(tq, dtype=jnp.int32)[None, :]
    dist = jnp.abs(2 * WINDOW + ir - r)
    tab = jnp.where((dist <= WINDOW)[None], -(slopes[:, None, None] * dist.astype(F32)[None]), NEG_INF)
    vt_rows = wavt.shape[3]
    n_tiles = B * n_t

    def q_map(g, j):
        seq, tile = _da_seq_tile(j, n_t, n_tiles)
        return (seq, tile, g)

    def o_map(g, j):
        seq, tile = _da_seq_tile(j - 1, n_t, n_tiles)
        return (seq, tile, g)

    return pl.pallas_call(
        functools.partial(_win_attn_kernel, n_t=n_t),
        grid=(WA_KV_HEADS // WA_KV_PER_STEP, n_tiles + 1),
        in_specs=[pl.BlockSpec(memory_space=pltpu.SMEM),
                  pl.BlockSpec((1, tq, gw), q_map),
                  pl.BlockSpec((1, S, gw), lambda g, j: (_da_seq_tile(j, n_t, n_tiles)[0], 0, g)),
                  pl.BlockSpec((1, WA_KV_PER_STEP, S // LANES, vt_rows, LANES),
                               lambda g, j: (_da_seq_tile(j - 1, n_t, n_tiles)[0], g, 0, 0, 0)),
                  pl.BlockSpec((heads, span + 2 * WINDOW, tq), lambda g, j: (g, 0, 0),
                               pipeline_mode=pl.Buffered(1))],
        out_specs=pl.BlockSpec((1, tq, gw), o_map),
        out_shape=jax.ShapeDtypeStruct((B, S, D), BF16),
        scratch_shapes=[pltpu.VMEM((heads, span, tq), F32), pltpu.VMEM((heads, span, tq), F32),
                        pltpu.VMEM((heads, tq), F32), pltpu.VMEM((heads, tq), F32)],
        compiler_params=_cparams(("parallel", "arbitrary")),
        name="win_attn",
    )(sink, waq, wak, wavt, tab)


def _mem_attn_head(q_ref, kv_ref, hh):
    hd = XA_HEAD_DIM
    q = q_ref[0, :, hh * hd:(hh + 1) * hd]
    k = kv_ref[0, :, hh * hd:(hh + 1) * hd]
    v = kv_ref[0, :, (XA_HEADS + hh) * hd:(XA_HEADS + hh + 1) * hd]
    s = _dot_nt(q, k)
    p = jnp.exp2(s - jnp.max(s, axis=1, keepdims=True))
    pn = (p * (1.0 / jnp.sum(p, axis=1, keepdims=True))).astype(BF16)
    return _dot(pn, v).astype(BF16)


def _merge_kernel(x_ref, oda_ref, owa_ref, xaq_ref, memkv_ref, g1_ref, wg_ref, wda_ref, wwa_ref, wxa_ref,
                  wout_ref, g2_ref, wr_ref, x1_ref, h2_ref, aff_ref):
    d = x_ref.shape[-1]
    hd = XA_HEAD_DIM
    x = x_ref[0]
    h = _rms(x, g1_ref[...]).astype(BF16)

    def gated(i, branch):
        return jax.nn.sigmoid(_dot(h, wg_ref[:, i * d:(i + 1) * d])) * branch

    xa = [_mem_attn_head(xaq_ref, memkv_ref, 0)]
    merged = gated(0, _dot(oda_ref[0], wda_ref[...]))
    xa.append(_mem_attn_head(xaq_ref, memkv_ref, 1))
    merged = merged + gated(1, _dot(owa_ref[0], wwa_ref[...]))
    xa.extend(_mem_attn_head(xaq_ref, memkv_ref, hh) for hh in range(2, XA_HEADS))
    xa_proj = None
    for hh, o in enumerate(xa):
        part = _dot(o, wxa_ref[hh * hd:(hh + 1) * hd, :])
        xa_proj = part if xa_proj is None else xa_proj + part
    merged = merged + gated(2, xa_proj)
    x1 = x + _dot(merged.astype(BF16), wout_ref[...])
    x1_ref[0] = x1
    h2 = _rms(x1, g2_ref[...]).astype(BF16)
    h2_ref[0] = h2
    logits = _dot(h2, wr_ref[...])
    lt = logits.T[:N_EXPERTS, :]
    m = jnp.max(lt, axis=0, keepdims=True)
    e = jnp.exp(lt - m)
    aff_ref[0] = e / jnp.sum(e, axis=0, keepdims=True)


def _merge(x, oda, owa, xaq, memkv, g1, wg, wda, wwa, wxa, wout, g2, wr, tm):
    B, S, D = x.shape
    tok = pl.BlockSpec((1, tm, D), lambda b, i: (b, i, 0))
    full = lambda a: _resident(a.shape, lambda b, i: (0,) * a.ndim)
    return pl.pallas_call(
        _merge_kernel,
        grid=(B, S // tm),
        in_specs=[tok, tok, tok, tok, pl.BlockSpec((1,) + memkv.shape[1:], lambda b, i: (b, 0, 0)),
                  full(g1), full(wg), full(wda), full(wwa), full(wxa), full(wout), full(g2), full(wr)],
        out_specs=[tok, tok, pl.BlockSpec((1, N_EXPERTS, tm), lambda b, i: (b, 0, i))],
        out_shape=[jax.ShapeDtypeStruct((B, S, D), F32),
                   jax.ShapeDtypeStruct((B, S, D), BF16),
                   jax.ShapeDtypeStruct((B, N_EXPERTS, S), F32)],
        compiler_params=_cparams(("parallel", "parallel")),
        name="merge",
    )(x, oda, owa, xaq, memkv, g1, wg, wda, wwa, wxa, wout, g2, wr)


def _route_kernel(aff_ref, pos_ref, *, cap):
    nb, ne, S = aff_ref.shape
    E = nb * ne
    aff = aff_ref[...].reshape(E, S)
    bits = pltpu.bitcast(aff, jnp.int32)
    prefix = jnp.zeros((E, 1), jnp.int32)
    for bit in range(30, -1, -1):
        cand = prefix | (1 << bit)
        cnt = jnp.sum(jnp.where(bits >= cand, 1.0, 0.0), axis=1, keepdims=True)
        prefix = jnp.where(cnt >= cap, cand, prefix)
    gt = bits > prefix
    eq = bits == prefix
    blk = 2 * LANES
    tri = (lax.broadcasted_iota(jnp.int32, (blk, blk), 0) <= lax.broadcasted_iota(jnp.int32, (blk, blk), 1))
    tri = jnp.where(tri, 1.0, 0.0).astype(BF16)

    def cumsum(mask):
        mb = jnp.where(mask, 1.0, 0.0).astype(BF16)
        carry = jnp.zeros((E, 1), F32)
        outs = []
        for i in range(S // blk):
            cs = _dot(mb[:, i * blk:(i + 1) * blk], tri) + carry
            outs.append(cs)
            carry = cs[:, blk - 1:blk]
        return jnp.concatenate(outs, axis=1), carry

    cs_gt, n_gt = cumsum(gt)
    cs_eq, _ = cumsum(eq)
    need = cap - n_gt
    sel = gt | (eq & (cs_eq <= need))
    slot = cs_gt + jnp.minimum(cs_eq, need) - 1.0
    pos_ref[...] = jnp.where(sel, slot, -1.0).astype(jnp.int32).reshape(nb, ne, S)


def _route(aff, cap):
    B, E, S = aff.shape
    nb = ROUTE_SEQS_PER_STEP if B % ROUTE_SEQS_PER_STEP == 0 else 1
    return pl.pallas_call(
        functools.partial(_route_kernel, cap=cap),
        grid=(B // nb,),
        in_specs=[pl.BlockSpec((nb, E, S), lambda b: (b, 0, 0))],
        out_specs=pl.BlockSpec((nb, E, S), lambda b: (b, 0, 0)),
        out_shape=jax.ShapeDtypeStruct((B, E, S), jnp.int32),
        compiler_params=_cparams(("parallel",)),
        name="route",
    )(aff)


def _moe_kernel(pos_ref, aff_ref, h2_ref, wg_ref, wu_ref, wd_ref, x1_hbm, gf_ref, o_ref, sem, *, cap, final_norm):
    b = pl.program_id(0)
    e = pl.program_id(1)
    S = h2_ref.shape[1]

    @pl.when(e == 0)
    def _():
        residual_copy = pltpu.make_async_copy(x1_hbm.at[b], o_ref.at[0], sem)
        residual_copy.start()
        residual_copy.wait()

    hit = pos_ref[0, 0] == lax.broadcasted_iota(jnp.int32, (cap, S), 0)
    onehot = jnp.where(hit, 1.0, 0.0).astype(BF16)
    gate = jnp.sum(jnp.where(hit, aff_ref[0, 0], 0.0), axis=1, keepdims=True)
    xe = _dot(onehot, h2_ref[0]).astype(BF16)
    a = _dot(xe, wg_ref[0])
    u = _dot(xe, wu_ref[0])
    act = (a * jax.nn.sigmoid(a) * u).astype(BF16)
    ye = (_dot(act, wd_ref[0]) * gate).astype(BF16)
    o_ref[0] += _dot_tn(onehot, ye)

    if final_norm:
        @pl.when(e == pl.num_programs(1) - 1)
        def _():
            o_ref[0] = _rms(o_ref[0], gf_ref[...])


def _moe(pos, aff, h2, wg, wu, wd, x1, final_g, cap):
    B, S, D = h2.shape
    E, _, F = wg.shape
    pos4 = pos.reshape(B, E, 1, S)
    aff4 = aff.reshape(B, E, 1, S)
    row_spec = pl.BlockSpec((1, 1, 1, S), lambda b, e: (b, e, 0, 0))
    gf = jnp.ones((1, D), F32) if final_g is None else final_g
    return pl.pallas_call(
        functools.partial(_moe_kernel, cap=cap, final_norm=final_g is not None),
        grid=(B, E),
        in_specs=[row_spec, row_spec,
                  pl.BlockSpec((1, S, D), lambda b, e: (b, 0, 0)),
                  pl.BlockSpec((1, D, F), lambda b, e: (e, 0, 0)),
                  pl.BlockSpec((1, D, F), lambda b, e: (e, 0, 0)),
                  pl.BlockSpec((1, F, D), lambda b, e: (e, 0, 0)),
                  pl.BlockSpec(memory_space=pl.ANY),
                  _resident((1, D), lambda b, e: (0, 0))],
        out_specs=pl.BlockSpec((1, S, D), lambda b, e: (b, 0, 0)),
        out_shape=jax.ShapeDtypeStruct((B, S, D), F32),
        scratch_shapes=[pltpu.SemaphoreType.DMA(())],
        compiler_params=_cparams(("arbitrary", "arbitrary")),
        name="moe",
    )(pos4, aff4, h2, wg, wu, wd, x1, gf)


def _tile(n, pref):
    t = min(n, pref)
    assert n % t == 0, (n, t)
    return t


def _lambda_init(layer):
    return 0.8 - 0.6 * float(np.exp(-0.3 * layer))


def _lane_half_forms(w, n_heads, width):
    d = w.shape[0]
    w = w.reshape(d, n_heads, width)
    z = jnp.zeros_like(w)
    return jnp.concatenate([w, z, z, w], axis=-1).reshape(d, n_heads * 4 * width)


def kernel(x, mem, attn_norm_g, mem_norm_g, w_in, w_mem_kv, da_lambda_q1, da_lambda_k1, da_lambda_q2,
           da_lambda_k2, da_subln_g, wa_sink, w_da_o, w_wa_o, w_xa_o, w_out, ffn_norm_g, w_router,
           w_exp_gate, w_exp_up, w_exp_down, final_norm_g):
    B, S, D = x.shape
    depth = w_in.shape[0]
    cap = max(1, EC_FACTOR * S // N_EXPERTS)
    row = lambda v: v.reshape(1, -1).astype(F32)
    da_w = DA_HEADS * 2 * DA_HEAD_DIM
    wa_kv_w = WA_KV_HEADS * WA_HEAD_DIM
    o_waq = 3 * da_w
    o_wak = o_waq + WA_HEADS * WA_HEAD_DIM
    o_wav = o_wak + wa_kv_w
    o_xaq = o_wav + wa_kv_w
    o_gate = o_xaq + XA_HEADS * XA_HEAD_DIM
    for l in range(depth):
        lam_init = _lambda_init(l)
        wl = w_in[l]
        w1 = jnp.concatenate([wl[:, :o_wak],
                              _lane_half_forms(wl[:, o_wak:o_wav], WA_KV_HEADS, WA_HEAD_DIM),
                              wl[:, o_wav:o_gate]], axis=1).astype(BF16)
        wg = wl[:, o_gate:].astype(BF16)
        wr = jnp.pad(w_router[l], ((0, 0), (0, LANES - N_EXPERTS))).astype(BF16)

        daq, dak, dav, waq, wak, wavt, xaq = _in_proj(x, row(attn_norm_g[l]), w1, _tile(S, 512))
        memkv = _mem_proj(mem, row(mem_norm_g[l]), w_mem_kv[l].astype(BF16))
        lams = [row(v[l]) for v in (da_lambda_q1, da_lambda_k1, da_lambda_q2, da_lambda_k2)]
        oda = _diff_attn(daq, dak, dav, lams, da_subln_g[l].astype(F32), lam_init, _tile(S, 1024))
        owa = _win_attn(waq, wak, wavt, wa_sink[l].astype(F32), _tile(S, 256))
        x1, h2, aff = _merge(x, oda, owa, xaq, memkv, row(attn_norm_g[l]), wg, w_da_o[l].astype(BF16),
                             w_wa_o[l].astype(BF16), w_xa_o[l].astype(BF16), w_out[l].astype(BF16),
                             row(ffn_norm_g[l]), wr, _tile(S, 512))
        pos = _route(aff, cap)
        last = l == depth - 1
        x = _moe(pos, aff, h2, w_exp_gate[l].astype(BF16), w_exp_up[l].astype(BF16), w_exp_down[l].astype(BF16),
                 x1, row(final_norm_g) if last else None, cap)
    return x
```

```python
import functools

import numpy as np
import jax
import jax.numpy as jnp
from jax import lax
from jax.experimental import pallas as pl
from jax.experimental.pallas import tpu as pltpu

EPS = 1e-6
BLOCK = 128
WINDOW = 128
DA_HEADS = 8
DA_HEAD_DIM = 64
DA_V_DIM = 2 * DA_HEAD_DIM
WA_HEADS = 16
WA_KV_HEADS = 4
WA_GROUPS = WA_HEADS // WA_KV_HEADS
WA_HEAD_DIM = 64
XA_HEADS = 4
XA_HEAD_DIM = 256
N_BRANCH = 3
N_EXPERTS = 16
EC_FACTOR = 2

LOG2E = 1.4426950408889634
DA_Q_SCALE = DA_HEAD_DIM ** -0.5 * LOG2E
XA_Q_SCALE = XA_HEAD_DIM ** -0.5 * LOG2E
WA_Q_SCALE = WA_HEAD_DIM ** -0.5 * LOG2E
WA_ONES_ROWS = 16
WA_KV_PER_STEP = 4
DA_KEY_CHUNK = 256
DA_ONES_ROWS = 16
DA_DRAIN_STEPS = 2
ROUTE_SEQS_PER_STEP = 4

LANES = 128
VMEM_LIMIT = 56 * 1024 * 1024

F32 = jnp.float32
BF16 = jnp.bfloat16
NEG_INF = float("-inf")


def _cparams(sem, flags=None):
    return pltpu.CompilerParams(dimension_semantics=sem, vmem_limit_bytes=VMEM_LIMIT, flags=flags)


def _rms(xf, g_row):
    ms = jnp.mean(xf * xf, axis=-1, keepdims=True)
    return xf * lax.rsqrt(ms + EPS) * g_row


def _dot(a, b):
    return jnp.dot(a, b, preferred_element_type=F32)


def _dot_nt(a, b):
    return lax.dot_general(a, b, (((1,), (1,)), ((), ())), preferred_element_type=F32)


def _dot_tn(a, b):
    return lax.dot_general(a, b, (((0,), (0,)), ((), ())), preferred_element_type=F32)


def _resident(shape, index_map):
    return pl.BlockSpec(shape, index_map, pipeline_mode=pl.Buffered(1))


def _in_proj_kernel(x_ref, g_ref, w_ref, daq_ref, dak_ref, dav_ref, waq_ref, wak_ref, wavt_ref, xaq_ref):
    d = x_ref.shape[-1]
    tm = x_ref.shape[1]
    h = _rms(x_ref[0], g_ref[...]).astype(BF16)
    col = [0]

    def sec(width):
        r = _dot(h, w_ref[:, col[0]:col[0] + width])
        col[0] += width
        return r

    for ref, scale in ((daq_ref, DA_Q_SCALE), (dak_ref, None), (dav_ref, None)):
        r = sec(d)
        r = (r if scale is None else r * scale).astype(BF16)
        for hh in range(DA_HEADS):
            ref[0, hh] = r[:, hh * LANES:(hh + 1) * LANES]
    waq_ref[0] = (sec(d) * WA_Q_SCALE).astype(BF16)
    wak_ref[0] = sec(d).astype(BF16)
    vt = sec(WA_KV_HEADS * WA_HEAD_DIM).T
    ones = jnp.ones((WA_ONES_ROWS, LANES), BF16)
    for hh in range(WA_KV_HEADS):
        for kb in range(tm // LANES):
            wavt_ref[0, hh, kb, 0:WA_HEAD_DIM, :] = vt[hh * WA_HEAD_DIM:(hh + 1) * WA_HEAD_DIM,
                                                       kb * LANES:(kb + 1) * LANES].astype(BF16)
            wavt_ref[0, hh, kb, WA_HEAD_DIM:, :] = ones
    xaq_ref[0] = (sec(d) * XA_Q_SCALE).astype(BF16)


def _in_proj(x, g, w1, tm):
    B, S, D = x.shape
    n_t = S // tm
    head_shape = jax.ShapeDtypeStruct((B, DA_HEADS, S, LANES), BF16)
    head_spec = pl.BlockSpec((1, DA_HEADS, tm, LANES), lambda b, i: (b, 0, i, 0))
    tok = pl.BlockSpec((1, tm, D), lambda b, i: (b, i, 0))
    tok_shape = jax.ShapeDtypeStruct((B, S, D), BF16)
    vt_rows = WA_HEAD_DIM + WA_ONES_ROWS
    return pl.pallas_call(
        _in_proj_kernel,
        grid=(B, n_t),
        in_specs=[tok, _resident((1, D), lambda b, i: (0, 0)), _resident(w1.shape, lambda b, i: (0, 0))],
        out_specs=[head_spec, head_spec, head_spec, tok, tok,
                   pl.BlockSpec((1, WA_KV_HEADS, tm // LANES, vt_rows, LANES), lambda b, i: (b, 0, i, 0, 0)),
                   tok],
        out_shape=[head_shape, head_shape, head_shape, tok_shape, tok_shape,
                   jax.ShapeDtypeStruct((B, WA_KV_HEADS, S // LANES, vt_rows, LANES), BF16),
                   tok_shape],
        compiler_params=_cparams(("parallel", "parallel")),
        name="in_proj",
    )(x, g, w1)


def _mem_proj_kernel(m_ref, g_ref, w_ref, o_ref):
    h = _rms(m_ref[0], g_ref[...]).astype(BF16)
    o_ref[0] = _dot(h, w_ref[...]).astype(BF16)


def _mem_proj(mem, g, w):
    B, M, D = mem.shape
    N = w.shape[1]
    return pl.pallas_call(
        _mem_proj_kernel,
        grid=(B,),
        in_specs=[pl.BlockSpec((1, M, D), lambda b: (b, 0, 0)),
                  _resident((1, D), lambda b: (0, 0)),
                  _resident((D, N), lambda b: (0, 0))],
        out_specs=pl.BlockSpec((1, M, N), lambda b: (b, 0, 0)),
        out_shape=jax.ShapeDtypeStruct((B, M, N), BF16),
        compiler_params=_cparams(("parallel",)),
        name="mem_proj",
    )(mem, g, w)


def _diff_attn_kernel(lq1_ref, lk1_ref, lq2_ref, lk2_ref, q_ref, k_ref, v_ref, bias_ref, g_ref, o_ref,
                      k1_scr, k2_scr, vt_scr, s_scr, m_scr, e_scr, *, lam_init, n_t):
    S = k_ref.shape[2]
    tq = q_ref.shape[2]
    j = pl.program_id(1)
    n_tiles = pl.num_programs(1) - DA_DRAIN_STEPS
    ck = DA_KEY_CHUNK
    seq, tile = _da_seq_tile(j, n_t, n_tiles)
    prev_seq, _ = _da_seq_tile(j - 1, n_t, n_tiles)

    @pl.when((tile == 0) & (j < n_tiles))
    def _():
        kk = k_ref[0, 0]
        lane = lax.broadcasted_iota(jnp.int32, kk.shape, 1)
        zero = jnp.zeros_like(kk)
        k1_scr[...] = jnp.where(lane < DA_HEAD_DIM, kk, zero)
        k2_scr[...] = jnp.where(lane >= DA_HEAD_DIM, kk, zero)
        vt_scr[seq % 2, 0:DA_V_DIM, :] = v_ref[0, 0].astype(F32).T.astype(BF16)
        vt_scr[seq % 2, DA_V_DIM:, :] = jnp.ones((DA_ONES_ROWS, S), BF16)

    @pl.when(j == 0)
    def _():
        s_scr[...] = jnp.zeros_like(s_scr)
        m_scr[...] = jnp.zeros_like(m_scr)
        e_scr[...] = jnp.ones_like(e_scr)

    lam = (jnp.exp(jnp.sum(lq1_ref[...] * lk1_ref[...], keepdims=True))
           - jnp.exp(jnp.sum(lq2_ref[...] * lk2_ref[...], keepdims=True)) + lam_init)

    q = q_ref[0, 0]
    r0 = (S - tq) - tile * tq
    vt_slot = prev_seq % 2

    def stage3():
        o1, l1 = e_scr[0, 0:DA_V_DIM, :], e_scr[0, DA_V_DIM:DA_V_DIM + 1, :]
        o2, l2 = e_scr[1, 0:DA_V_DIM, :], e_scr[1, DA_V_DIM:DA_V_DIM + 1, :]
        ot = o1 * (1.0 / l1) - o2 * (lam / l2)
        ms = jnp.mean(ot * ot, axis=0, keepdims=True)
        y = ot * lax.rsqrt(ms + EPS) * g_ref[...] * (1.0 - lam_init)
        o_ref[0] = y.T.astype(BF16)

    m_prev = (m_scr[0:1, :], m_scr[1:2, :])
    m_new = [None, None]
    acc = [None, None]
    n_chunks = S // ck
    for c in range(n_chunks):
        if c == n_chunks // 2:
            stage3()
        rows = slice(c * ck, (c + 1) * ck)
        bias = bias_ref[0, pl.ds(pl.multiple_of(r0 + c * ck, ck), ck), :]
        half = ck // 2
        for i, k_scr in enumerate((k1_scr, k2_scr)):
            p_lo = jnp.exp2((s_scr[i, c * ck:c * ck + half, :] - m_prev[i]).astype(BF16))
            s = _dot_nt(k_scr[rows, :], q) + bias
            p_hi = jnp.exp2((s_scr[i, c * ck + half:(c + 1) * ck, :] - m_prev[i]).astype(BF16))
            d = _dot(vt_scr[vt_slot, :, rows], jnp.concatenate([p_lo, p_hi], axis=0))
            acc[i] = d if acc[i] is None else acc[i] + d
            s_scr[i, rows, :] = s
            cm = jnp.max(s, axis=0, keepdims=True)
            m_new[i] = cm if m_new[i] is None else jnp.maximum(m_new[i], cm)
    m_scr[0:1, :] = m_new[0]
    m_scr[1:2, :] = m_new[1]
    e_scr[0] = acc[0]
    e_scr[1] = acc[1]


def _da_seq_tile(j, n_t, n_tiles):
    jc = jnp.clip(j, 0, n_tiles - 1)
    return jc // n_t, jc % n_t


def _diff_attn(daq, dak, dav, lams, subln_g, lam_init, tq):
    B, H, S, _ = daq.shape
    n_t = S // tq
    rows = 2 * S - tq
    slopes = jnp.exp2(-8.0 * jnp.arange(1, H + 1, dtype=F32) / H)
    r = jnp.arange(rows, dtype=jnp.int32)[:, None]
    c = jnp.arange(tq, dtype=jnp.int32)[None, :]
    dist = jnp.abs(r - (S - tq) - c).astype(F32)
    bias = -((slopes * LOG2E)[:, None, None] * dist[None])
    n_tiles = B * n_t
    lam_spec = _resident((1, DA_HEAD_DIM), lambda h, j: (0, 0))

    def kv_map(h, j):
        return (_da_seq_tile(j, n_t, n_tiles)[0], h, 0, 0)

    def q_map(h, j):
        seq, tile = _da_seq_tile(j, n_t, n_tiles)
        return (seq, h, tile, 0)

    def o_map(h, j):
        seq, tile = _da_seq_tile(j - DA_DRAIN_STEPS, n_t, n_tiles)
        return (seq, tile, h)

    kv_spec = pl.BlockSpec((1, 1, S, LANES), kv_map)
    acc_shape = pltpu.VMEM((2, DA_V_DIM + DA_ONES_ROWS, tq), F32)
    return pl.pallas_call(
        functools.partial(_diff_attn_kernel, lam_init=lam_init, n_t=n_t),
        grid=(H, n_tiles + DA_DRAIN_STEPS),
        in_specs=[lam_spec, lam_spec, lam_spec, lam_spec,
                  pl.BlockSpec((1, 1, tq, LANES), q_map),
                  kv_spec, kv_spec,
                  pl.BlockSpec((1, rows, tq), lambda h, j: (h, 0, 0), pipeline_mode=pl.Buffered(1)),
                  _resident((DA_V_DIM, 1), lambda h, j: (0, 0))],
        out_specs=pl.BlockSpec((1, tq, LANES), o_map),
        out_shape=jax.ShapeDtypeStruct((B, S, H * LANES), BF16),
        scratch_shapes=[pltpu.VMEM((S, LANES), BF16), pltpu.VMEM((S, LANES), BF16),
                        pltpu.VMEM((2, DA_V_DIM + DA_ONES_ROWS, S), BF16),
                        pltpu.VMEM((2, S, tq), F32), pltpu.VMEM((2, tq), F32), acc_shape],
        compiler_params=_cparams(("parallel", "arbitrary")),
        name="diff_attn",
    )(*lams, daq, dak, dav, bias, subln_g.reshape(DA_V_DIM, 1))


def _win_attn_kernel(sink_ref, q_ref, k_ref, vt_ref, tab_ref, o_ref, sa_scr, sb_scr, ma_scr, mb_scr, *, n_t):
    S = k_ref.shape[1]
    tq = q_ref.shape[1]
    span = tq + 2 * WINDOW
    g = pl.program_id(0)
    j = pl.program_id(1)
    n_tiles = pl.num_programs(1) - 1
    _, tile = _da_seq_tile(j, n_t, n_tiles)
    _, prev_tile = _da_seq_tile(j - 1, n_t, n_tiles)

    def window_start(t):
        return pl.multiple_of(jnp.clip(t * tq - WINDOW, 0, S - span), LANES)

    @pl.when(j == 0)
    def _():
        sb_scr[...] = jnp.zeros_like(sb_scr)
        mb_scr[...] = jnp.zeros_like(mb_scr)

    shift = jnp.where(tile == 0, 0, jnp.where(tile == n_t - 1, 2 * WINDOW, WINDOW))
    tab_row = pl.multiple_of(2 * WINDOW - shift, LANES)
    kwin = k_ref[0, pl.ds(window_start(tile), span), :]
    kb0 = window_start(prev_tile) // LANES
    heads = WA_KV_PER_STEP * WA_GROUPS

    def step(s_w, m_w, s_r, m_r):
        outs = []
        for jh in range(heads):
            kv, pair_half = divmod(jh, WA_GROUPS)
            pair, half = divmod(pair_half, 2)
            sink = sink_ref[g * heads + jh] * LOG2E
            m = m_r[jh:jh + 1, :]
            n_kb = span // LANES
            qcol = (kv * 2 + pair) * LANES
            kcol = (kv * 2 + half) * LANES
            qp = q_ref[0, :, qcol:qcol + LANES]
            ps = [jnp.exp2((s_r[jh, kb * LANES:(kb + 1) * LANES, :] - m).astype(BF16)) for kb in range(n_kb // 2)]
            s = _dot_nt(kwin[:, kcol:kcol + LANES], qp) + tab_ref[jh, pl.ds(tab_row, span), :]
            ps += [jnp.exp2((s_r[jh, kb * LANES:(kb + 1) * LANES, :] - m).astype(BF16))
                   for kb in range(n_kb // 2, n_kb)]
            acc = None
            for kb, p in enumerate(ps):
                d = _dot(vt_ref[0, kv, kb0 + kb], p)
                acc = d if acc is None else acc + d
            l = acc[WA_HEAD_DIM:WA_HEAD_DIM + 1, :] + jnp.exp2(sink - m)
            outs.append(acc[0:WA_HEAD_DIM, :] * (1.0 / l))
            s_w[jh] = s
            m_w[jh:jh + 1, :] = jnp.maximum(jnp.max(s, axis=0, keepdims=True), sink)
        o_ref[0] = jnp.concatenate(outs, axis=0).T.astype(BF16)

    @pl.when(j % 2 == 0)
    def _():
        step(sa_scr, ma_scr, sb_scr, mb_scr)

    @pl.when(j % 2 == 1)
    def _():
        step(sb_scr, mb_scr, sa_scr, ma_scr)


def _win_attn(waq, wak, wavt, sink, tq):
    B, S, D = waq.shape
    n_t = S // tq
    span = tq + 2 * WINDOW
    assert n_t >= 2 and S >= span, (S, tq)
    heads = WA_KV_PER_STEP * WA_GROUPS
    gw = heads * WA_HEAD_DIM
    slopes = jnp.exp2(-8.0 * jnp.arange(1, WA_HEADS + 1, dtype=F32) / WA_HEADS) * LOG2E
    r = jnp.arange(span + 2 * WINDOW, dtype=jnp.int32)[:, None]
    ir = jnp.arange(tq, dtype=jnp.int32)[None, :]
    dist = jnp.abs(2 * WINDOW + ir - r)
    tab = jnp.where((dist <= WINDOW)[None], -(slopes[:, None, None] * dist.astype(F32)[None]), NEG_INF)
    vt_rows = wavt.shape[3]
    n_tiles = B * n_t

    def q_map(g, j):
        seq, tile = _da_seq_tile(j, n_t, n_tiles)
        return (seq, tile, g)

    def o_map(g, j):
        seq, tile = _da_seq_tile(j - 1, n_t, n_tiles)
        return (seq, tile, g)

    return pl.pallas_call(
        functools.partial(_win_attn_kernel, n_t=n_t),
        grid=(WA_KV_HEADS // WA_KV_PER_STEP, n_tiles + 1),
        in_specs=[pl.BlockSpec(memory_space=pltpu.SMEM),
                  pl.BlockSpec((1, tq, gw), q_map),
                  pl.BlockSpec((1, S, gw), lambda g, j: (_da_seq_tile(j, n_t, n_tiles)[0], 0, g)),
                  pl.BlockSpec((1, WA_KV_PER_STEP, S // LANES, vt_rows, LANES),
                               lambda g, j: (_da_seq_tile(j - 1, n_t, n_tiles)[0], g, 0, 0, 0)),
                  pl.BlockSpec((heads, span + 2 * WINDOW, tq), lambda g, j: (g, 0, 0),
                               pipeline_mode=pl.Buffered(1))],
        out_specs=pl.BlockSpec((1, tq, gw), o_map),
        out_shape=jax.ShapeDtypeStruct((B, S, D), BF16),
        scratch_shapes=[pltpu.VMEM((heads, span, tq), F32), pltpu.VMEM((heads, span, tq), F32),
                        pltpu.VMEM((heads, tq), F32), pltpu.VMEM((heads, tq), F32)],
        compiler_params=_cparams(("parallel", "arbitrary")),
        name="win_attn",
    )(sink, waq, wak, wavt, tab)


def _mem_attn_head(q_ref, kv_ref, hh):
    hd = XA_HEAD_DIM
    q = q_ref[0, :, hh * hd:(hh + 1) * hd]
    k = kv_ref[0, :, hh * hd:(hh + 1) * hd]
    v = kv_ref[0, :, (XA_HEADS + hh) * hd:(XA_HEADS + hh + 1) * hd]
    s = _dot_nt(q, k)
    p = jnp.exp2(s - jnp.max(s, axis=1, keepdims=True))
    pn = (p * (1.0 / jnp.sum(p, axis=1, keepdims=True))).astype(BF16)
    return _dot(pn, v).astype(BF16)


def _merge_kernel(x_ref, oda_ref, owa_ref, xaq_ref, memkv_ref, g1_ref, wg_ref, wda_ref, wwa_ref, wxa_ref,
                  wout_ref, g2_ref, wr_ref, x1_ref, h2_ref, aff_ref):
    d = x_ref.shape[-1]
    hd = XA_HEAD_DIM
    x = x_ref[0]
    h = _rms(x, g1_ref[...]).astype(BF16)

    def gated(i, branch):
        return jax.nn.sigmoid(_dot(h, wg_ref[:, i * d:(i + 1) * d])) * branch

    xa = [_mem_attn_head(xaq_ref, memkv_ref, 0)]
    merged = gated(0, _dot(oda_ref[0], wda_ref[...]))
    xa.append(_mem_attn_head(xaq_ref, memkv_ref, 1))
    merged = merged + gated(1, _dot(owa_ref[0], wwa_ref[...]))
    xa.extend(_mem_attn_head(xaq_ref, memkv_ref, hh) for hh in range(2, XA_HEADS))
    xa_proj = None
    for hh, o in enumerate(xa):
        part = _dot(o, wxa_ref[hh * hd:(hh + 1) * hd, :])
        xa_proj = part if xa_proj is None else xa_proj + part
    merged = merged + gated(2, xa_proj)
    x1 = x + _dot(merged.astype(BF16), wout_ref[...])
    x1_ref[0] = x1
    h2 = _rms(x1, g2_ref[...]).astype(BF16)
    h2_ref[0] = h2
    logits = _dot(h2, wr_ref[...])
    lt = logits.T[:N_EXPERTS, :]
    m = jnp.max(lt, axis=0, keepdims=True)
    e = jnp.exp(lt - m)
    aff_ref[0] = e / jnp.sum(e, axis=0, keepdims=True)


def _merge(x, oda, owa, xaq, memkv, g1, wg, wda, wwa, wxa, wout, g2, wr, tm):
    B, S, D = x.shape
    tok = pl.BlockSpec((1, tm, D), lambda b, i: (b, i, 0))
    full = lambda a: _resident(a.shape, lambda b, i: (0,) * a.ndim)
    return pl.pallas_call(
        _merge_kernel,
        grid=(B, S // tm),
        in_specs=[tok, tok, tok, tok, pl.BlockSpec((1,) + memkv.shape[1:], lambda b, i: (b, 0, 0)),
                  full(g1), full(wg), full(wda), full(wwa), full(wxa), full(wout), full(g2), full(wr)],
        out_specs=[tok, tok, pl.BlockSpec((1, N_EXPERTS, tm), lambda b, i: (b, 0, i))],
        out_shape=[jax.ShapeDtypeStruct((B, S, D), F32),
                   jax.ShapeDtypeStruct((B, S, D), BF16),
                   jax.ShapeDtypeStruct((B, N_EXPERTS, S), F32)],
        compiler_params=_cparams(("parallel", "parallel")),
        name="merge",
    )(x, oda, owa, xaq, memkv, g1, wg, wda, wwa, wxa, wout, g2, wr)


def _route_kernel(aff_ref, pos_ref, *, cap):
    nb, ne, S = aff_ref.shape
    E = nb * ne
    aff = aff_ref[...].reshape(E, S)
    bits = pltpu.bitcast(aff, jnp.int32)
    prefix = jnp.zeros((E, 1), jnp.int32)
    for bit in range(30, -1, -1):
        cand = prefix | (1 << bit)
        cnt = jnp.sum(jnp.where(bits >= cand, 1.0, 0.0), axis=1, keepdims=True)
        prefix = jnp.where(cnt >= cap, cand, prefix)
    gt = bits > prefix
    eq = bits == prefix
    blk = 2 * LANES
    tri = (lax.broadcasted_iota(jnp.int32, (blk, blk), 0) <= lax.broadcasted_iota(jnp.int32, (blk, blk), 1))
    tri = jnp.where(tri, 1.0, 0.0).astype(BF16)

    def cumsum(mask):
        mb = jnp.where(mask, 1.0, 0.0).astype(BF16)
        carry = jnp.zeros((E, 1), F32)
        outs = []
        for i in range(S // blk):
            cs = _dot(mb[:, i * blk:(i + 1) * blk], tri) + carry
            outs.append(cs)
            carry = cs[:, blk - 1:blk]
        return jnp.concatenate(outs, axis=1), carry

    cs_gt, n_gt = cumsum(gt)
    cs_eq, _ = cumsum(eq)
    need = cap - n_gt
    sel = gt | (eq & (cs_eq <= need))
    slot = cs_gt + jnp.minimum(cs_eq, need) - 1.0
    pos_ref[...] = jnp.where(sel, slot, -1.0).astype(jnp.int32).reshape(nb, ne, S)


def _route(aff, cap):
    B, E, S = aff.shape
    nb = ROUTE_SEQS_PER_STEP if B % ROUTE_SEQS_PER_STEP == 0 else 1
    return pl.pallas_call(
        functools.partial(_route_kernel, cap=cap),
        grid=(B // nb,),
        in_specs=[pl.BlockSpec((nb, E, S), lambda b: (b, 0, 0))],
        out_specs=pl.BlockSpec((nb, E, S), lambda b: (b, 0, 0)),
        out_shape=jax.ShapeDtypeStruct((B, E, S), jnp.int32),
        compiler_params=_cparams(("parallel",)),
        name="route",
    )(aff)


def _moe_kernel(pos_ref, aff_ref, h2_ref, wg_ref, wu_ref, wd_ref, o_ref, p_scr, g_scr, xe_scr, y_scr, *, cap):
    e = pl.program_id(1)
    f = pl.program_id(2)
    nf = pl.num_programs(2)
    S = h2_ref.shape[1]

    @pl.when((e == 0) & (f == 0))
    def _():
        o_ref[...] = jnp.zeros_like(o_ref)

    @pl.when(f == 0)
    def _():
        slot = lax.broadcasted_iota(jnp.int32, (cap, S), 0)
        hit = pos_ref[0, 0] == slot
        onehot = jnp.where(hit, 1.0, 0.0).astype(BF16)
        p_scr[...] = onehot
        g_scr[...] = jnp.sum(jnp.where(hit, aff_ref[0, 0], 0.0), axis=1, keepdims=True)
        xe_scr[...] = _dot(onehot, h2_ref[0]).astype(BF16)
        y_scr[...] = jnp.zeros_like(y_scr)

    xe = xe_scr[...]
    a = _dot(xe, wg_ref[0])
    u = _dot(xe, wu_ref[0])
    act = (a * jax.nn.sigmoid(a) * u).astype(BF16)
    y_scr[...] += _dot(act, wd_ref[0])

    @pl.when(f == nf - 1)
    def _():
        ye = (y_scr[...] * g_scr[...]).astype(BF16)
        o_ref[0] += _dot_tn(p_scr[...], ye)


def _moe(pos, aff, h2, wg, wu, wd, cap, tf):
    B, S, D = h2.shape
    E, _, F = wg.shape
    pos4 = pos.reshape(B, E, 1, S)
    aff4 = aff.reshape(B, E, 1, S)
    row_spec = pl.BlockSpec((1, 1, 1, S), lambda b, e, f: (b, e, 0, 0))
    return pl.pallas_call(
        functools.partial(_moe_kernel, cap=cap),
        grid=(B, E, F // tf),
        in_specs=[row_spec, row_spec,
                  pl.BlockSpec((1, S, D), lambda b, e, f: (b, 0, 0)),
                  pl.BlockSpec((1, D, tf), lambda b, e, f: (e, 0, f)),
                  pl.BlockSpec((1, D, tf), lambda b, e, f: (e, 0, f)),
                  pl.BlockSpec((1, tf, D), lambda b, e, f: (e, f, 0))],
        out_specs=pl.BlockSpec((1, S, D), lambda b, e, f: (b, 0, 0)),
        out_shape=jax.ShapeDtypeStruct((B, S, D), F32),
        scratch_shapes=[pltpu.VMEM((cap, S), BF16), pltpu.VMEM((cap, 1), F32),
                        pltpu.VMEM((cap, D), BF16), pltpu.VMEM((cap, D), F32)],
        compiler_params=_cparams(("parallel", "arbitrary", "arbitrary")),
        name="moe",
    )(pos4, aff4, h2, wg, wu, wd)


def _residual_kernel(x_ref, d_ref, o_ref):
    o_ref[0] = x_ref[0] + d_ref[0]


def _residual_norm_kernel(x_ref, d_ref, g_ref, o_ref):
    o_ref[0] = _rms(x_ref[0] + d_ref[0], g_ref[...])


def _residual(x1, delta, g, tm):
    B, S, D = x1.shape
    tok = pl.BlockSpec((1, tm, D), lambda b, i: (b, i, 0))
    if g is None:
        body, extra, extra_specs = _residual_kernel, (), []
    else:
        body, extra, extra_specs = _residual_norm_kernel, (g,), [_resident((1, D), lambda b, i: (0, 0))]
    return pl.pallas_call(
        body,
        grid=(B, S // tm),
        in_specs=[tok, tok] + extra_specs,
        out_specs=tok,
        out_shape=jax.ShapeDtypeStruct((B, S, D), F32),
        compiler_params=_cparams(("parallel", "parallel")),
        name="residual",
    )(x1, delta, *extra)


def _tile(n, pref):
    t = min(n, pref)
    assert n % t == 0, (n, t)
    return t


def _lambda_init(layer):
    return 0.8 - 0.6 * float(np.exp(-0.3 * layer))


def _lane_half_forms(w, n_heads, width):
    d = w.shape[0]
    w = w.reshape(d, n_heads, width)
    z = jnp.zeros_like(w)
    return jnp.concatenate([w, z, z, w], axis=-1).reshape(d, n_heads * 4 * width)


def kernel(x, mem, attn_norm_g, mem_norm_g, w_in, w_mem_kv, da_lambda_q1, da_lambda_k1, da_lambda_q2,
           da_lambda_k2, da_subln_g, wa_sink, w_da_o, w_wa_o, w_xa_o, w_out, ffn_norm_g, w_router,
           w_exp_gate, w_exp_up, w_exp_down, final_norm_g):
    B, S, D = x.shape
    depth = w_in.shape[0]
    cap = max(1, EC_FACTOR * S // N_EXPERTS)
    row = lambda v: v.reshape(1, -1).astype(F32)
    da_w = DA_HEADS * 2 * DA_HEAD_DIM
    wa_kv_w = WA_KV_HEADS * WA_HEAD_DIM
    o_waq = 3 * da_w
    o_wak = o_waq + WA_HEADS * WA_HEAD_DIM
    o_wav = o_wak + wa_kv_w
    o_xaq = o_wav + wa_kv_w
    o_gate = o_xaq + XA_HEADS * XA_HEAD_DIM
    for l in range(depth):
        lam_init = _lambda_init(l)
        wl = w_in[l]
        w1 = jnp.concatenate([wl[:, :o_wak],
                              _lane_half_forms(wl[:, o_wak:o_wav], WA_KV_HEADS, WA_HEAD_DIM),
                              wl[:, o_wav:o_gate]], axis=1).astype(BF16)
        wg = wl[:, o_gate:].astype(BF16)
        wr = jnp.pad(w_router[l], ((0, 0), (0, LANES - N_EXPERTS))).astype(BF16)

        daq, dak, dav, waq, wak, wavt, xaq = _in_proj(x, row(attn_norm_g[l]), w1, _tile(S, 512))
        memkv = _mem_proj(mem, row(mem_norm_g[l]), w_mem_kv[l].astype(BF16))
        lams = [row(v[l]) for v in (da_lambda_q1, da_lambda_k1, da_lambda_q2, da_lambda_k2)]
        oda = _diff_attn(daq, dak, dav, lams, da_subln_g[l].astype(F32), lam_init, _tile(S, 1024))
        owa = _win_attn(waq, wak, wavt, wa_sink[l].astype(F32), _tile(S, 256))
        x1, h2, aff = _merge(x, oda, owa, xaq, memkv, row(attn_norm_g[l]), wg, w_da_o[l].astype(BF16),
                             w_wa_o[l].astype(BF16), w_xa_o[l].astype(BF16), w_out[l].astype(BF16),
                             row(ffn_norm_g[l]), wr, _tile(S, 512))
        pos = _route(aff, cap)
        delta = _moe(pos, aff, h2, w_exp_gate[l].astype(BF16), w_exp_up[l].astype(BF16),
                     w_exp_down[l].astype(BF16), cap, _tile(w_exp_gate.shape[-1], 2048))
        last = l == depth - 1
        x = _residual(x1, delta, row(final_norm_g) if last else None, _tile(S, 512))
    return x
```

```python
import functools

import numpy as np
import jax
import jax.numpy as jnp
from jax import lax
from jax.experimental import pallas as pl
from jax.experimental.pallas import tpu as pltpu

EPS = 1e-6
BLOCK = 128
WINDOW = 128
DA_HEADS = 8
DA_HEAD_DIM = 64
DA_V_DIM = 2 * DA_HEAD_DIM
WA_HEADS = 16
WA_KV_HEADS = 4
WA_GROUPS = WA_HEADS // WA_KV_HEADS
WA_HEAD_DIM = 64
XA_HEADS = 4
XA_HEAD_DIM = 256
N_BRANCH = 3
N_EXPERTS = 16
EC_FACTOR = 2

LOG2E = 1.4426950408889634
DA_Q_SCALE = DA_HEAD_DIM ** -0.5 * LOG2E
XA_Q_SCALE = XA_HEAD_DIM ** -0.5 * LOG2E
WA_Q_SCALE = WA_HEAD_DIM ** -0.5 * LOG2E
WA_ONES_ROWS = 16
WA_KV_PER_STEP = 4
DA_KEY_CHUNK = 256
DA_ONES_ROWS = 16
DA_DRAIN_STEPS = 2
ROUTE_SEQS_PER_STEP = 4

LANES = 128
VMEM_LIMIT = 56 * 1024 * 1024

F32 = jnp.float32
BF16 = jnp.bfloat16
NEG_INF = float("-inf")


def _cparams(sem, flags=None):
    return pltpu.CompilerParams(dimension_semantics=sem, vmem_limit_bytes=VMEM_LIMIT, flags=flags)


def _rms(xf, g_row):
    ms = jnp.mean(xf * xf, axis=-1, keepdims=True)
    return xf * lax.rsqrt(ms + EPS) * g_row


def _dot(a, b):
    return jnp.dot(a, b, preferred_element_type=F32)


def _dot_nt(a, b):
    return lax.dot_general(a, b, (((1,), (1,)), ((), ())), preferred_element_type=F32)


def _dot_tn(a, b):
    return lax.dot_general(a, b, (((0,), (0,)), ((), ())), preferred_element_type=F32)


def _resident(shape, index_map):
    return pl.BlockSpec(shape, index_map, pipeline_mode=pl.Buffered(1))


def _in_proj_kernel(x_ref, g_ref, w_ref, daq_ref, dak_ref, dav_ref, waq_ref, wak_ref, wavt_ref, xaq_ref):
    d = x_ref.shape[-1]
    tm = x_ref.shape[1]
    h = _rms(x_ref[0], g_ref[...]).astype(BF16)
    col = [0]

    def sec(width):
        r = _dot(h, w_ref[:, col[0]:col[0] + width])
        col[0] += width
        return r

    for ref, scale in ((daq_ref, DA_Q_SCALE), (dak_ref, None), (dav_ref, None)):
        r = sec(d)
        r = (r if scale is None else r * scale).astype(BF16)
        for hh in range(DA_HEADS):
            ref[0, hh] = r[:, hh * LANES:(hh + 1) * LANES]
    waq_ref[0] = (sec(d) * WA_Q_SCALE).astype(BF16)
    wak_ref[0] = sec(d).astype(BF16)
    vt = sec(WA_KV_HEADS * WA_HEAD_DIM).T
    ones = jnp.ones((WA_ONES_ROWS, LANES), BF16)
    for hh in range(WA_KV_HEADS):
        for kb in range(tm // LANES):
            wavt_ref[0, hh, kb, 0:WA_HEAD_DIM, :] = vt[hh * WA_HEAD_DIM:(hh + 1) * WA_HEAD_DIM,
                                                       kb * LANES:(kb + 1) * LANES].astype(BF16)
            wavt_ref[0, hh, kb, WA_HEAD_DIM:, :] = ones
    xaq_ref[0] = (sec(d) * XA_Q_SCALE).astype(BF16)


def _in_proj(x, g, w1, tm):
    B, S, D = x.shape
    n_t = S // tm
    head_shape = jax.ShapeDtypeStruct((B, DA_HEADS, S, LANES), BF16)
    head_spec = pl.BlockSpec((1, DA_HEADS, tm, LANES), lambda b, i: (b, 0, i, 0))
    tok = pl.BlockSpec((1, tm, D), lambda b, i: (b, i, 0))
    tok_shape = jax.ShapeDtypeStruct((B, S, D), BF16)
    vt_rows = WA_HEAD_DIM + WA_ONES_ROWS
    return pl.pallas_call(
        _in_proj_kernel,
        grid=(B, n_t),
        in_specs=[tok, _resident((1, D), lambda b, i: (0, 0)), _resident(w1.shape, lambda b, i: (0, 0))],
        out_specs=[head_spec, head_spec, head_spec, tok, tok,
                   pl.BlockSpec((1, WA_KV_HEADS, tm // LANES, vt_rows, LANES), lambda b, i: (b, 0, i, 0, 0)),
                   tok],
        out_shape=[head_shape, head_shape, head_shape, tok_shape, tok_shape,
                   jax.ShapeDtypeStruct((B, WA_KV_HEADS, S // LANES, vt_rows, LANES), BF16),
                   tok_shape],
        compiler_params=_cparams(("parallel", "parallel")),
        name="in_proj",
    )(x, g, w1)


def _mem_proj_kernel(m_ref, g_ref, w_ref, o_ref):
    h = _rms(m_ref[0], g_ref[...]).astype(BF16)
    o_ref[0] = _dot(h, w_ref[...]).astype(BF16)


def _mem_proj(mem, g, w):
    B, M, D = mem.shape
    N = w.shape[1]
    return pl.pallas_call(
        _mem_proj_kernel,
        grid=(B,),
        in_specs=[pl.BlockSpec((1, M, D), lambda b: (b, 0, 0)),
                  _resident((1, D), lambda b: (0, 0)),
                  _resident((D, N), lambda b: (0, 0))],
        out_specs=pl.BlockSpec((1, M, N), lambda b: (b, 0, 0)),
        out_shape=jax.ShapeDtypeStruct((B, M, N), BF16),
        compiler_params=_cparams(("parallel",)),
        name="mem_proj",
    )(mem, g, w)


def _diff_attn_kernel(lq1_ref, lk1_ref, lq2_ref, lk2_ref, q_ref, k_ref, v_ref, bias_ref, g_ref, o_ref,
                      k1_scr, k2_scr, vt_scr, s_scr, m_scr, e_scr, *, lam_init, n_t):
    S = k_ref.shape[2]
    tq = q_ref.shape[2]
    j = pl.program_id(1)
    n_tiles = pl.num_programs(1) - DA_DRAIN_STEPS
    ck = DA_KEY_CHUNK
    seq, tile = _da_seq_tile(j, n_t, n_tiles)
    prev_seq, _ = _da_seq_tile(j - 1, n_t, n_tiles)

    @pl.when((tile == 0) & (j < n_tiles))
    def _():
        kk = k_ref[0, 0]
        lane = lax.broadcasted_iota(jnp.int32, kk.shape, 1)
        zero = jnp.zeros_like(kk)
        k1_scr[...] = jnp.where(lane < DA_HEAD_DIM, kk, zero)
        k2_scr[...] = jnp.where(lane >= DA_HEAD_DIM, kk, zero)
        vt_scr[seq % 2, 0:DA_V_DIM, :] = v_ref[0, 0].astype(F32).T.astype(BF16)
        vt_scr[seq % 2, DA_V_DIM:, :] = jnp.ones((DA_ONES_ROWS, S), BF16)

    @pl.when(j == 0)
    def _():
        s_scr[...] = jnp.zeros_like(s_scr)
        m_scr[...] = jnp.zeros_like(m_scr)
        e_scr[...] = jnp.ones_like(e_scr)

    lam = (jnp.exp(jnp.sum(lq1_ref[...] * lk1_ref[...], keepdims=True))
           - jnp.exp(jnp.sum(lq2_ref[...] * lk2_ref[...], keepdims=True)) + lam_init)

    q = q_ref[0, 0]
    r0 = (S - tq) - tile * tq
    vt_slot = prev_seq % 2

    def stage3():
        o1, l1 = e_scr[0, 0:DA_V_DIM, :], e_scr[0, DA_V_DIM:DA_V_DIM + 1, :]
        o2, l2 = e_scr[1, 0:DA_V_DIM, :], e_scr[1, DA_V_DIM:DA_V_DIM + 1, :]
        ot = o1 * (1.0 / l1) - o2 * (lam / l2)
        ms = jnp.mean(ot * ot, axis=0, keepdims=True)
        y = ot * lax.rsqrt(ms + EPS) * g_ref[...] * (1.0 - lam_init)
        o_ref[0] = y.T.astype(BF16)

    m_prev = (m_scr[0:1, :], m_scr[1:2, :])
    m_new = [None, None]
    acc = [None, None]
    n_chunks = S // ck
    for c in range(n_chunks):
        if c == n_chunks // 2:
            stage3()
        rows = slice(c * ck, (c + 1) * ck)
        bias = bias_ref[0, pl.ds(pl.multiple_of(r0 + c * ck, ck), ck), :]
        half = ck // 2
        for i, k_scr in enumerate((k1_scr, k2_scr)):
            p_lo = jnp.exp2((s_scr[i, c * ck:c * ck + half, :] - m_prev[i]).astype(BF16))
            s = _dot_nt(k_scr[rows, :], q) + bias
            p_hi = jnp.exp2((s_scr[i, c * ck + half:(c + 1) * ck, :] - m_prev[i]).astype(BF16))
            d = _dot(vt_scr[vt_slot, :, rows], jnp.concatenate([p_lo, p_hi], axis=0))
            acc[i] = d if acc[i] is None else acc[i] + d
            s_scr[i, rows, :] = s
            cm = jnp.max(s, axis=0, keepdims=True)
            m_new[i] = cm if m_new[i] is None else jnp.maximum(m_new[i], cm)
    m_scr[0:1, :] = m_new[0]
    m_scr[1:2, :] = m_new[1]
    e_scr[0] = acc[0]
    e_scr[1] = acc[1]


def _da_seq_tile(j, n_t, n_tiles):
    jc = jnp.clip(j, 0, n_tiles - 1)
    return jc // n_t, jc % n_t


def _diff_attn(daq, dak, dav, lams, subln_g, lam_init, tq):
    B, H, S, _ = daq.shape
    n_t = S // tq
    rows = 2 * S - tq
    slopes = jnp.exp2(-8.0 * jnp.arange(1, H + 1, dtype=F32) / H)
    r = jnp.arange(rows, dtype=jnp.int32)[:, None]
    c = jnp.arange(tq, dtype=jnp.int32)[None, :]
    dist = jnp.abs(r - (S - tq) - c).astype(F32)
    bias = -((slopes * LOG2E)[:, None, None] * dist[None])
    n_tiles = B * n_t
    lam_spec = _resident((1, DA_HEAD_DIM), lambda h, j: (0, 0))

    def kv_map(h, j):
        return (_da_seq_tile(j, n_t, n_tiles)[0], h, 0, 0)

    def q_map(h, j):
        seq, tile = _da_seq_tile(j, n_t, n_tiles)
        return (seq, h, tile, 0)

    def o_map(h, j):
        seq, tile = _da_seq_tile(j - DA_DRAIN_STEPS, n_t, n_tiles)
        return (seq, tile, h)

    kv_spec = pl.BlockSpec((1, 1, S, LANES), kv_map)
    acc_shape = pltpu.VMEM((2, DA_V_DIM + DA_ONES_ROWS, tq), F32)
    return pl.pallas_call(
        functools.partial(_diff_attn_kernel, lam_init=lam_init, n_t=n_t),
        grid=(H, n_tiles + DA_DRAIN_STEPS),
        in_specs=[lam_spec, lam_spec, lam_spec, lam_spec,
                  pl.BlockSpec((1, 1, tq, LANES), q_map),
                  kv_spec, kv_spec,
                  pl.BlockSpec((1, rows, tq), lambda h, j: (h, 0, 0), pipeline_mode=pl.Buffered(1)),
                  _resident((DA_V_DIM, 1), lambda h, j: (0, 0))],
        out_specs=pl.BlockSpec((1, tq, LANES), o_map),
        out_shape=jax.ShapeDtypeStruct((B, S, H * LANES), BF16),
        scratch_shapes=[pltpu.VMEM((S, LANES), BF16), pltpu.VMEM((S, LANES), BF16),
                        pltpu.VMEM((2, DA_V_DIM + DA_ONES_ROWS, S), BF16),
                        pltpu.VMEM((2, S, tq), F32), pltpu.VMEM((2, tq), F32), acc_shape],
        compiler_params=_cparams(("parallel", "arbitrary")),
        name="diff_attn",
    )(*lams, daq, dak, dav, bias, subln_g.reshape(DA_V_DIM, 1))


def _win_attn_kernel(sink_ref, q_ref, k_ref, vt_ref, tab_ref, o_ref, s_scr, m_scr, *, n_t):
    S = k_ref.shape[1]
    tq = q_ref.shape[1]
    span = tq + 2 * WINDOW
    g = pl.program_id(0)
    j = pl.program_id(1)
    n_tiles = pl.num_programs(1) - 1
    _, tile = _da_seq_tile(j, n_t, n_tiles)
    _, prev_tile = _da_seq_tile(j - 1, n_t, n_tiles)

    def window_start(t):
        return pl.multiple_of(jnp.clip(t * tq - WINDOW, 0, S - span), LANES)

    def variant_of(t):
        return jnp.where(t == 0, 0, jnp.where(t == n_t - 1, 2, 1))

    @pl.when(j == 0)
    def _():
        s_scr[...] = jnp.zeros_like(s_scr)
        m_scr[...] = jnp.zeros_like(m_scr)

    tab_row = pl.multiple_of(2 * WINDOW - variant_of(tile) * WINDOW, LANES)
    kwin = k_ref[0, pl.ds(window_start(tile), span), :]
    kb0 = window_start(prev_tile) // LANES
    heads = WA_KV_PER_STEP * WA_GROUPS
    n_kb = span // LANES
    n_qb = tq // LANES

    def step(prev_variant):
        prev_shift = prev_variant * WINDOW

        def probs(jh, kb, m):
            cols = []
            for qb in range(n_qb):
                if abs(prev_shift + (qb - kb) * LANES) < 2 * LANES:
                    x = s_scr[jh, kb * LANES:(kb + 1) * LANES, qb * LANES:(qb + 1) * LANES]
                    cols.append(jnp.exp2((x - m[:, qb * LANES:(qb + 1) * LANES]).astype(BF16)))
                else:
                    cols.append(jnp.zeros((LANES, LANES), BF16))
            return jnp.concatenate(cols, axis=1)

        outs = []
        for jh in range(heads):
            kv, pair_half = divmod(jh, WA_GROUPS)
            pair, half = divmod(pair_half, 2)
            sink = sink_ref[g * heads + jh] * LOG2E
            m = m_scr[jh:jh + 1, :]
            qcol = (kv * 2 + pair) * LANES
            kcol = (kv * 2 + half) * LANES
            qp = q_ref[0, :, qcol:qcol + LANES]
            ps = [probs(jh, kb, m) for kb in range(n_kb // 2)]
            s = _dot_nt(kwin[:, kcol:kcol + LANES], qp) + tab_ref[jh, pl.ds(tab_row, span), :]
            ps += [probs(jh, kb, m) for kb in range(n_kb // 2, n_kb)]
            acc = None
            for kb, p in enumerate(ps):
                d = _dot(vt_ref[0, kv, kb0 + kb], p)
                acc = d if acc is None else acc + d
            l = acc[WA_HEAD_DIM:WA_HEAD_DIM + 1, :] + jnp.exp2(sink - m)
            outs.append(acc[0:WA_HEAD_DIM, :] * (1.0 / l))
            s_scr[jh] = s
            m_scr[jh:jh + 1, :] = jnp.maximum(jnp.max(s, axis=0, keepdims=True), sink)
        o_ref[0] = jnp.concatenate(outs, axis=0).T.astype(BF16)

    prev_variant = variant_of(prev_tile)
    for v in range(3):
        pl.when(prev_variant == v)(functools.partial(step, v))


def _win_attn(waq, wak, wavt, sink, tq):
    B, S, D = waq.shape
    n_t = S // tq
    span = tq + 2 * WINDOW
    assert n_t >= 2 and S >= span, (S, tq)
    heads = WA_KV_PER_STEP * WA_GROUPS
    gw = heads * WA_HEAD_DIM
    slopes = jnp.exp2(-8.0 * jnp.arange(1, WA_HEADS + 1, dtype=F32) / WA_HEADS) * LOG2E
    r = jnp.arange(span + 2 * WINDOW, dtype=jnp.int32)[:, None]
    ir = jnp.arange(tq, dtype=jnp.int32)[None, :]
    dist = jnp.abs(2 * WINDOW + ir - r)
    tab = jnp.where((dist <= WINDOW)[None], -(slopes[:, None, None] * dist.astype(F32)[None]), NEG_INF)
    vt_rows = wavt.shape[3]
    n_tiles = B * n_t

    def q_map(g, j):
        seq, tile = _da_seq_tile(j, n_t, n_tiles)
        return (seq, tile, g)

    def o_map(g, j):
        seq, tile = _da_seq_tile(j - 1, n_t, n_tiles)
        return (seq, tile, g)

    return pl.pallas_call(
        functools.partial(_win_attn_kernel, n_t=n_t),
        grid=(WA_KV_HEADS // WA_KV_PER_STEP, n_tiles + 1),
        in_specs=[pl.BlockSpec(memory_space=pltpu.SMEM),
                  pl.BlockSpec((1, tq, gw), q_map),
                  pl.BlockSpec((1, S, gw), lambda g, j: (_da_seq_tile(j, n_t, n_tiles)[0], 0, g)),
                  pl.BlockSpec((1, WA_KV_PER_STEP, S // LANES, vt_rows, LANES),
                               lambda g, j: (_da_seq_tile(j - 1, n_t, n_tiles)[0], g, 0, 0, 0)),
                  pl.BlockSpec((heads, span + 2 * WINDOW, tq), lambda g, j: (g, 0, 0),
                               pipeline_mode=pl.Buffered(1))],
        out_specs=pl.BlockSpec((1, tq, gw), o_map),
        out_shape=jax.ShapeDtypeStruct((B, S, D), BF16),
        scratch_shapes=[pltpu.VMEM((heads, span, tq), F32), pltpu.VMEM((heads, tq), F32)],
        compiler_params=_cparams(("parallel", "arbitrary")),
        name="win_attn",
    )(sink, waq, wak, wavt, tab)


def _mem_attn_head(q_ref, kv_ref, hh):
    hd = XA_HEAD_DIM
    q = q_ref[0, :, hh * hd:(hh + 1) * hd]
    k = kv_ref[0, :, hh * hd:(hh + 1) * hd]
    v = kv_ref[0, :, (XA_HEADS + hh) * hd:(XA_HEADS + hh + 1) * hd]
    s = _dot_nt(q, k)
    p = jnp.exp2(s - jnp.max(s, axis=1, keepdims=True))
    pn = (p * (1.0 / jnp.sum(p, axis=1, keepdims=True))).astype(BF16)
    return _dot(pn, v).astype(BF16)


def _merge_kernel(x_ref, oda_ref, owa_ref, xaq_ref, memkv_ref, g1_ref, wg_ref, wda_ref, wwa_ref, wxa_ref,
                  wout_ref, g2_ref, wr_ref, x1_ref, h2_ref, aff_ref, x1_scr):
    d = x_ref.shape[-1]
    hd = XA_HEAD_DIM

    @pl.when(pl.program_id(0) == 0)
    def _():
        x1_scr[...] = jnp.zeros_like(x1_scr)

    def previous_tile_tail():
        h2 = _rms(x1_scr[...], g2_ref[...]).astype(BF16)
        h2_ref[0] = h2
        logits = _dot(h2, wr_ref[...])
        lt = logits.T[:N_EXPERTS, :]
        e = jnp.exp(lt - jnp.max(lt, axis=0, keepdims=True))
        aff_ref[0] = e / jnp.sum(e, axis=0, keepdims=True)

    x = x_ref[0]
    h = _rms(x, g1_ref[...]).astype(BF16)

    def gated(i, branch):
        return jax.nn.sigmoid(_dot(h, wg_ref[:, i * d:(i + 1) * d])) * branch

    xa = [_mem_attn_head(xaq_ref, memkv_ref, 0)]
    merged = gated(0, _dot(oda_ref[0], wda_ref[...]))
    xa.append(_mem_attn_head(xaq_ref, memkv_ref, 1))
    merged = merged + gated(1, _dot(owa_ref[0], wwa_ref[...]))
    xa.extend(_mem_attn_head(xaq_ref, memkv_ref, hh) for hh in range(2, XA_HEADS))
    xa_proj = None
    for hh, o in enumerate(xa):
        part = _dot(o, wxa_ref[hh * hd:(hh + 1) * hd, :])
        xa_proj = part if xa_proj is None else xa_proj + part
    merged = merged + gated(2, xa_proj)
    previous_tile_tail()
    x1 = x + _dot(merged.astype(BF16), wout_ref[...])
    x1_ref[0] = x1
    x1_scr[...] = x1


def _merge(x, oda, owa, xaq, memkv, g1, wg, wda, wwa, wxa, wout, g2, wr, tm):
    B, S, D = x.shape
    n_t = S // tm
    n_tiles = B * n_t

    def cur(j):
        return _da_seq_tile(j, n_t, n_tiles)

    def prev(j):
        return _da_seq_tile(j - 1, n_t, n_tiles)

    tok = pl.BlockSpec((1, tm, D), lambda j: cur(j) + (0,))
    tok_prev = pl.BlockSpec((1, tm, D), lambda j: prev(j) + (0,))
    full = lambda a: _resident(a.shape, lambda j: (0,) * a.ndim)
    return pl.pallas_call(
        _merge_kernel,
        grid=(n_tiles + 1,),
        in_specs=[tok, tok, tok, tok, pl.BlockSpec((1,) + memkv.shape[1:], lambda j: (cur(j)[0], 0, 0)),
                  full(g1), full(wg), full(wda), full(wwa), full(wxa), full(wout), full(g2), full(wr)],
        out_specs=[tok, tok_prev, pl.BlockSpec((1, N_EXPERTS, tm), lambda j: (prev(j)[0], 0, prev(j)[1]))],
        out_shape=[jax.ShapeDtypeStruct((B, S, D), F32),
                   jax.ShapeDtypeStruct((B, S, D), BF16),
                   jax.ShapeDtypeStruct((B, N_EXPERTS, S), F32)],
        scratch_shapes=[pltpu.VMEM((tm, D), F32)],
        compiler_params=_cparams(("arbitrary",)),
        name="merge",
    )(x, oda, owa, xaq, memkv, g1, wg, wda, wwa, wxa, wout, g2, wr)


def _route_kernel(aff_ref, pos_ref, *, cap):
    nb, ne, S = aff_ref.shape
    E = nb * ne
    aff = aff_ref[...].reshape(E, S)
    bits = pltpu.bitcast(aff, jnp.int32)
    prefix = jnp.zeros((E, 1), jnp.int32)
    for bit in range(30, -1, -1):
        cand = prefix | (1 << bit)
        cnt = jnp.sum(jnp.where(bits >= cand, 1.0, 0.0), axis=1, keepdims=True)
        prefix = jnp.where(cnt >= cap, cand, prefix)
    gt = bits > prefix
    eq = bits == prefix
    blk = 2 * LANES
    tri = (lax.broadcasted_iota(jnp.int32, (blk, blk), 0) <= lax.broadcasted_iota(jnp.int32, (blk, blk), 1))
    tri = jnp.where(tri, 1.0, 0.0).astype(BF16)

    def cumsum(mask):
        mb = jnp.where(mask, 1.0, 0.0).astype(BF16)
        carry = jnp.zeros((E, 1), F32)
        outs = []
        for i in range(S // blk):
            cs = _dot(mb[:, i * blk:(i + 1) * blk], tri) + carry
            outs.append(cs)
            carry = cs[:, blk - 1:blk]
        return jnp.concatenate(outs, axis=1), carry

    cs_gt, n_gt = cumsum(gt)
    cs_eq, _ = cumsum(eq)
    need = cap - n_gt
    sel = gt | (eq & (cs_eq <= need))
    slot = cs_gt + jnp.minimum(cs_eq, need) - 1.0
    pos_ref[...] = jnp.where(sel, slot, -1.0).astype(jnp.int32).reshape(nb, ne, S)


def _route(aff, cap):
    B, E, S = aff.shape
    nb = ROUTE_SEQS_PER_STEP if B % ROUTE_SEQS_PER_STEP == 0 else 1
    return pl.pallas_call(
        functools.partial(_route_kernel, cap=cap),
        grid=(B // nb,),
        in_specs=[pl.BlockSpec((nb, E, S), lambda b: (b, 0, 0))],
        out_specs=pl.BlockSpec((nb, E, S), lambda b: (b, 0, 0)),
        out_shape=jax.ShapeDtypeStruct((B, E, S), jnp.int32),
        compiler_params=_cparams(("parallel",)),
        name="route",
    )(aff)


def _moe_kernel(pos_ref, aff_ref, h2_ref, wg_ref, wu_ref, wd_ref, o_ref, p_scr, g_scr, xe_scr, y_scr, *, cap):
    e = pl.program_id(1)
    f = pl.program_id(2)
    nf = pl.num_programs(2)
    S = h2_ref.shape[1]

    @pl.when((e == 0) & (f == 0))
    def _():
        o_ref[...] = jnp.zeros_like(o_ref)

    @pl.when(f == 0)
    def _():
        slot = lax.broadcasted_iota(jnp.int32, (cap, S), 0)
        hit = pos_ref[0, 0] == slot
        onehot = jnp.where(hit, 1.0, 0.0).astype(BF16)
        p_scr[...] = onehot
        g_scr[...] = jnp.sum(jnp.where(hit, aff_ref[0, 0], 0.0), axis=1, keepdims=True)
        xe_scr[...] = _dot(onehot, h2_ref[0]).astype(BF16)
        y_scr[...] = jnp.zeros_like(y_scr)

    xe = xe_scr[...]
    a = _dot(xe, wg_ref[0])
    u = _dot(xe, wu_ref[0])
    act = (a * jax.nn.sigmoid(a) * u).astype(BF16)
    y_scr[...] += _dot(act, wd_ref[0])

    @pl.when(f == nf - 1)
    def _():
        ye = (y_scr[...] * g_scr[...]).astype(BF16)
        o_ref[0] += _dot_tn(p_scr[...], ye)


def _moe(pos, aff, h2, wg, wu, wd, cap, tf):
    B, S, D = h2.shape
    E, _, F = wg.shape
    pos4 = pos.reshape(B, E, 1, S)
    aff4 = aff.reshape(B, E, 1, S)
    row_spec = pl.BlockSpec((1, 1, 1, S), lambda b, e, f: (b, e, 0, 0))
    return pl.pallas_call(
        functools.partial(_moe_kernel, cap=cap),
        grid=(B, E, F // tf),
        in_specs=[row_spec, row_spec,
                  pl.BlockSpec((1, S, D), lambda b, e, f: (b, 0, 0)),
                  pl.BlockSpec((1, D, tf), lambda b, e, f: (e, 0, f)),
                  pl.BlockSpec((1, D, tf), lambda b, e, f: (e, 0, f)),
                  pl.BlockSpec((1, tf, D), lambda b, e, f: (e, f, 0))],
        out_specs=pl.BlockSpec((1, S, D), lambda b, e, f: (b, 0, 0)),
        out_shape=jax.ShapeDtypeStruct((B, S, D), F32),
        scratch_shapes=[pltpu.VMEM((cap, S), BF16), pltpu.VMEM((cap, 1), F32),
                        pltpu.VMEM((cap, D), BF16), pltpu.VMEM((cap, D), F32)],
        compiler_params=_cparams(("parallel", "arbitrary", "arbitrary")),
        name="moe",
    )(pos4, aff4, h2, wg, wu, wd)


def _residual_kernel(x_ref, d_ref, o_ref):
    o_ref[0] = x_ref[0] + d_ref[0]


def _residual_norm_kernel(x_ref, d_ref, g_ref, o_ref):
    o_ref[0] = _rms(x_ref[0] + d_ref[0], g_ref[...])


def _residual(x1, delta, g, tm):
    B, S, D = x1.shape
    tok = pl.BlockSpec((1, tm, D), lambda b, i: (b, i, 0))
    if g is None:
        body, extra, extra_specs = _residual_kernel, (), []
    else:
        body, extra, extra_specs = _residual_norm_kernel, (g,), [_resident((1, D), lambda b, i: (0, 0))]
    return pl.pallas_call(
        body,
        grid=(B, S // tm),
        in_specs=[tok, tok] + extra_specs,
        out_specs=tok,
        out_shape=jax.ShapeDtypeStruct((B, S, D), F32),
        compiler_params=_cparams(("parallel", "parallel")),
        name="residual",
    )(x1, delta, *extra)


def _tile(n, pref):
    t = min(n, pref)
    assert n % t == 0, (n, t)
    return t


def _lambda_init(layer):
    return 0.8 - 0.6 * float(np.exp(-0.3 * layer))


def _lane_half_forms(w, n_heads, width):
    d = w.shape[0]
    w = w.reshape(d, n_heads, width)
    z = jnp.zeros_like(w)
    return jnp.concatenate([w, z, z, w], axis=-1).reshape(d, n_heads * 4 * width)


def kernel(x, mem, attn_norm_g, mem_norm_g, w_in, w_mem_kv, da_lambda_q1, da_lambda_k1, da_lambda_q2,
           da_lambda_k2, da_subln_g, wa_sink, w_da_o, w_wa_o, w_xa_o, w_out, ffn_norm_g, w_router,
           w_exp_gate, w_exp_up, w_exp_down, final_norm_g):
    B, S, D = x.shape
    depth = w_in.shape[0]
    cap = max(1, EC_FACTOR * S // N_EXPERTS)
    row = lambda v: v.reshape(1, -1).astype(F32)
    da_w = DA_HEADS * 2 * DA_HEAD_DIM
    wa_kv_w = WA_KV_HEADS * WA_HEAD_DIM
    o_waq = 3 * da_w
    o_wak = o_waq + WA_HEADS * WA_HEAD_DIM
    o_wav = o_wak + wa_kv_w
    o_xaq = o_wav + wa_kv_w
    o_gate = o_xaq + XA_HEADS * XA_HEAD_DIM
    for l in range(depth):
        lam_init = _lambda_init(l)
        wl = w_in[l]
        w1 = jnp.concatenate([wl[:, :o_wak],
                              _lane_half_forms(wl[:, o_wak:o_wav], WA_KV_HEADS, WA_HEAD_DIM),
                              wl[:, o_wav:o_gate]], axis=1).astype(BF16)
        wg = wl[:, o_gate:].astype(BF16)
        wr = jnp.pad(w_router[l], ((0, 0), (0, LANES - N_EXPERTS))).astype(BF16)

        daq, dak, dav, waq, wak, wavt, xaq = _in_proj(x, row(attn_norm_g[l]), w1, _tile(S, 512))
        memkv = _mem_proj(mem, row(mem_norm_g[l]), w_mem_kv[l].astype(BF16))
        lams = [row(v[l]) for v in (da_lambda_q1, da_lambda_k1, da_lambda_q2, da_lambda_k2)]
        oda = _diff_attn(daq, dak, dav, lams, da_subln_g[l].astype(F32), lam_init, _tile(S, 1024))
        owa = _win_attn(waq, wak, wavt, wa_sink[l].astype(F32), _tile(S, 256))
        x1, h2, aff = _merge(x, oda, owa, xaq, memkv, row(attn_norm_g[l]), wg, w_da_o[l].astype(BF16),
                             w_wa_o[l].astype(BF16), w_xa_o[l].astype(BF16), w_out[l].astype(BF16),
                             row(ffn_norm_g[l]), wr, _tile(S, 512))
        pos = _route(aff, cap)
        delta = _moe(pos, aff, h2, w_exp_gate[l].astype(BF16), w_exp_up[l].astype(BF16),
                     w_exp_down[l].astype(BF16), cap, _tile(w_exp_gate.shape[-1], 2048))
        last = l == depth - 1
        x = _residual(x1, delta, row(final_norm_g) if last else None, _tile(S, 512))
    return x
```

```python
import functools

import numpy as np
import jax
import jax.numpy as jnp
from jax import lax
from jax.experimental import pallas as pl
from jax.experimental.pallas import tpu as pltpu

EPS = 1e-6
BLOCK = 128
WINDOW = 128
DA_HEADS = 8
DA_HEAD_DIM = 64
DA_V_DIM = 2 * DA_HEAD_DIM
WA_HEADS = 16
WA_KV_HEADS = 4
WA_GROUPS = WA_HEADS // WA_KV_HEADS
WA_HEAD_DIM = 64
XA_HEADS = 4
XA_HEAD_DIM = 256
N_BRANCH = 3
N_EXPERTS = 16
EC_FACTOR = 2

LOG2E = 1.4426950408889634
DA_Q_SCALE = DA_HEAD_DIM ** -0.5 * LOG2E
XA_Q_SCALE = XA_HEAD_DIM ** -0.5 * LOG2E
WA_Q_SCALE = WA_HEAD_DIM ** -0.5 * LOG2E
WA_ONES_ROWS = 16
WA_KV_PER_STEP = 4
DA_KEY_CHUNK = 256
DA_ONES_ROWS = 16
DA_DRAIN_STEPS = 2
ROUTE_SEQS_PER_STEP = 4

LANES = 128
VMEM_LIMIT = 56 * 1024 * 1024

F32 = jnp.float32
BF16 = jnp.bfloat16
NEG_INF = float("-inf")


def _cparams(sem, flags=None):
    return pltpu.CompilerParams(dimension_semantics=sem, vmem_limit_bytes=VMEM_LIMIT, flags=flags)


def _rms(xf, g_row):
    ms = jnp.mean(xf * xf, axis=-1, keepdims=True)
    return xf * lax.rsqrt(ms + EPS) * g_row


def _dot(a, b):
    return jnp.dot(a, b, preferred_element_type=F32)


def _dot_nt(a, b):
    return lax.dot_general(a, b, (((1,), (1,)), ((), ())), preferred_element_type=F32)


def _dot_tn(a, b):
    return lax.dot_general(a, b, (((0,), (0,)), ((), ())), preferred_element_type=F32)


def _resident(shape, index_map):
    return pl.BlockSpec(shape, index_map, pipeline_mode=pl.Buffered(1))


def _in_proj_kernel(x_ref, g_ref, w_ref, daq_ref, dak_ref, dav_ref, waq_ref, wak_ref, wavt_ref, xaq_ref):
    d = x_ref.shape[-1]
    tm = x_ref.shape[1]
    h = _rms(x_ref[0], g_ref[...]).astype(BF16)
    col = [0]

    def sec(width):
        r = _dot(h, w_ref[:, col[0]:col[0] + width])
        col[0] += width
        return r

    for ref, scale in ((daq_ref, DA_Q_SCALE), (dak_ref, None), (dav_ref, None)):
        r = sec(d)
        r = (r if scale is None else r * scale).astype(BF16)
        for hh in range(DA_HEADS):
            ref[0, hh] = r[:, hh * LANES:(hh + 1) * LANES]
    waq_ref[0] = (sec(d) * WA_Q_SCALE).astype(BF16)
    wak_ref[0] = sec(d).astype(BF16)
    vt = sec(WA_KV_HEADS * WA_HEAD_DIM).T
    ones = jnp.ones((WA_ONES_ROWS, LANES), BF16)
    for hh in range(WA_KV_HEADS):
        for kb in range(tm // LANES):
            wavt_ref[0, hh, kb, 0:WA_HEAD_DIM, :] = vt[hh * WA_HEAD_DIM:(hh + 1) * WA_HEAD_DIM,
                                                       kb * LANES:(kb + 1) * LANES].astype(BF16)
            wavt_ref[0, hh, kb, WA_HEAD_DIM:, :] = ones
    xaq_ref[0] = (sec(d) * XA_Q_SCALE).astype(BF16)


def _in_proj(x, g, w1, tm):
    B, S, D = x.shape
    n_t = S // tm
    head_shape = jax.ShapeDtypeStruct((B, DA_HEADS, S, LANES), BF16)
    head_spec = pl.BlockSpec((1, DA_HEADS, tm, LANES), lambda b, i: (b, 0, i, 0))
    tok = pl.BlockSpec((1, tm, D), lambda b, i: (b, i, 0))
    tok_shape = jax.ShapeDtypeStruct((B, S, D), BF16)
    vt_rows = WA_HEAD_DIM + WA_ONES_ROWS
    return pl.pallas_call(
        _in_proj_kernel,
        grid=(B, n_t),
        in_specs=[tok, _resident((1, D), lambda b, i: (0, 0)), _resident(w1.shape, lambda b, i: (0, 0))],
        out_specs=[head_spec, head_spec, head_spec, tok, tok,
                   pl.BlockSpec((1, WA_KV_HEADS, tm // LANES, vt_rows, LANES), lambda b, i: (b, 0, i, 0, 0)),
                   tok],
        out_shape=[head_shape, head_shape, head_shape, tok_shape, tok_shape,
                   jax.ShapeDtypeStruct((B, WA_KV_HEADS, S // LANES, vt_rows, LANES), BF16),
                   tok_shape],
        compiler_params=_cparams(("parallel", "parallel")),
        name="in_proj",
    )(x, g, w1)


def _mem_proj_kernel(m_ref, g_ref, w_ref, o_ref):
    h = _rms(m_ref[0], g_ref[...]).astype(BF16)
    o_ref[0] = _dot(h, w_ref[...]).astype(BF16)


def _mem_proj(mem, g, w):
    B, M, D = mem.shape
    N = w.shape[1]
    return pl.pallas_call(
        _mem_proj_kernel,
        grid=(B,),
        in_specs=[pl.BlockSpec((1, M, D), lambda b: (b, 0, 0)),
                  _resident((1, D), lambda b: (0, 0)),
                  _resident((D, N), lambda b: (0, 0))],
        out_specs=pl.BlockSpec((1, M, N), lambda b: (b, 0, 0)),
        out_shape=jax.ShapeDtypeStruct((B, M, N), BF16),
        compiler_params=_cparams(("parallel",)),
        name="mem_proj",
    )(mem, g, w)


def _diff_attn_kernel(lq1_ref, lk1_ref, lq2_ref, lk2_ref, q_ref, k_ref, v_ref, bias_ref, g_ref, o_ref,
                      k1_scr, k2_scr, vt_scr, s_scr, m_scr, e_scr, *, lam_init, n_t):
    S = k_ref.shape[2]
    tq = q_ref.shape[2]
    j = pl.program_id(1)
    n_tiles = pl.num_programs(1) - DA_DRAIN_STEPS
    ck = DA_KEY_CHUNK
    seq, tile = _da_seq_tile(j, n_t, n_tiles)
    prev_seq, _ = _da_seq_tile(j - 1, n_t, n_tiles)

    @pl.when((tile == 0) & (j < n_tiles))
    def _():
        kk = k_ref[0, 0]
        lane = lax.broadcasted_iota(jnp.int32, kk.shape, 1)
        zero = jnp.zeros_like(kk)
        k1_scr[...] = jnp.where(lane < DA_HEAD_DIM, kk, zero)
        k2_scr[...] = jnp.where(lane >= DA_HEAD_DIM, kk, zero)
        vt_scr[seq % 2, 0:DA_V_DIM, :] = v_ref[0, 0].astype(F32).T.astype(BF16)
        vt_scr[seq % 2, DA_V_DIM:, :] = jnp.ones((DA_ONES_ROWS, S), BF16)

    @pl.when(j == 0)
    def _():
        s_scr[...] = jnp.zeros_like(s_scr)
        m_scr[...] = jnp.zeros_like(m_scr)
        e_scr[...] = jnp.ones_like(e_scr)

    lam = (jnp.exp(jnp.sum(lq1_ref[...] * lk1_ref[...], keepdims=True))
           - jnp.exp(jnp.sum(lq2_ref[...] * lk2_ref[...], keepdims=True)) + lam_init)

    q = q_ref[0, 0]
    r0 = (S - tq) - tile * tq
    vt_slot = prev_seq % 2

    def stage3():
        o1, l1 = e_scr[0, 0:DA_V_DIM, :], e_scr[0, DA_V_DIM:DA_V_DIM + 1, :]
        o2, l2 = e_scr[1, 0:DA_V_DIM, :], e_scr[1, DA_V_DIM:DA_V_DIM + 1, :]
        ot = o1 * (1.0 / l1) - o2 * (lam / l2)
        ms = jnp.mean(ot * ot, axis=0, keepdims=True)
        y = ot * lax.rsqrt(ms + EPS) * g_ref[...] * (1.0 - lam_init)
        o_ref[0] = y.T.astype(BF16)

    m_prev = (m_scr[0:1, :], m_scr[1:2, :])
    m_new = [None, None]
    acc = [None, None]
    n_chunks = S // ck
    for c in range(n_chunks):
        if c == n_chunks // 2:
            stage3()
        rows = slice(c * ck, (c + 1) * ck)
        bias = bias_ref[0, pl.ds(pl.multiple_of(r0 + c * ck, ck), ck), :]
        half = ck // 2
        for i, k_scr in enumerate((k1_scr, k2_scr)):
            p_lo = jnp.exp2((s_scr[i, c * ck:c * ck + half, :] - m_prev[i]).astype(BF16))
            s = _dot_nt(k_scr[rows, :], q) + bias
            p_hi = jnp.exp2((s_scr[i, c * ck + half:(c + 1) * ck, :] - m_prev[i]).astype(BF16))
            d = _dot(vt_scr[vt_slot, :, rows], jnp.concatenate([p_lo, p_hi], axis=0))
            acc[i] = d if acc[i] is None else acc[i] + d
            s_scr[i, rows, :] = s
            cm = jnp.max(s, axis=0, keepdims=True)
            m_new[i] = cm if m_new[i] is None else jnp.maximum(m_new[i], cm)
    m_scr[0:1, :] = m_new[0]
    m_scr[1:2, :] = m_new[1]
    e_scr[0] = acc[0]
    e_scr[1] = acc[1]


def _da_seq_tile(j, n_t, n_tiles):
    jc = jnp.clip(j, 0, n_tiles - 1)
    return jc // n_t, jc % n_t


def _diff_attn(daq, dak, dav, lams, subln_g, lam_init, tq):
    B, H, S, _ = daq.shape
    n_t = S // tq
    rows = 2 * S - tq
    slopes = jnp.exp2(-8.0 * jnp.arange(1, H + 1, dtype=F32) / H)
    r = jnp.arange(rows, dtype=jnp.int32)[:, None]
    c = jnp.arange(tq, dtype=jnp.int32)[None, :]
    dist = jnp.abs(r - (S - tq) - c).astype(F32)
    bias = -((slopes * LOG2E)[:, None, None] * dist[None])
    n_tiles = B * n_t
    lam_spec = _resident((1, DA_HEAD_DIM), lambda h, j: (0, 0))

    def kv_map(h, j):
        return (_da_seq_tile(j, n_t, n_tiles)[0], h, 0, 0)

    def q_map(h, j):
        seq, tile = _da_seq_tile(j, n_t, n_tiles)
        return (seq, h, tile, 0)

    def o_map(h, j):
        seq, tile = _da_seq_tile(j - DA_DRAIN_STEPS, n_t, n_tiles)
        return (seq, tile, h)

    kv_spec = pl.BlockSpec((1, 1, S, LANES), kv_map)
    acc_shape = pltpu.VMEM((2, DA_V_DIM + DA_ONES_ROWS, tq), F32)
    return pl.pallas_call(
        functools.partial(_diff_attn_kernel, lam_init=lam_init, n_t=n_t),
        grid=(H, n_tiles + DA_DRAIN_STEPS),
        in_specs=[lam_spec, lam_spec, lam_spec, lam_spec,
                  pl.BlockSpec((1, 1, tq, LANES), q_map),
                  kv_spec, kv_spec,
                  pl.BlockSpec((1, rows, tq), lambda h, j: (h, 0, 0), pipeline_mode=pl.Buffered(1)),
                  _resident((DA_V_DIM, 1), lambda h, j: (0, 0))],
        out_specs=pl.BlockSpec((1, tq, LANES), o_map),
        out_shape=jax.ShapeDtypeStruct((B, S, H * LANES), BF16),
        scratch_shapes=[pltpu.VMEM((S, LANES), BF16), pltpu.VMEM((S, LANES), BF16),
                        pltpu.VMEM((2, DA_V_DIM + DA_ONES_ROWS, S), BF16),
                        pltpu.VMEM((2, S, tq), F32), pltpu.VMEM((2, tq), F32), acc_shape],
        compiler_params=_cparams(("parallel", "arbitrary")),
        name="diff_attn",
    )(*lams, daq, dak, dav, bias, subln_g.reshape(DA_V_DIM, 1))


def _win_attn_kernel(sink_ref, q_ref, k_ref, vt_ref, tab_ref, o_ref, s_scr, m_scr, *, n_t):
    S = k_ref.shape[1]
    tq = q_ref.shape[1]
    span = tq + 2 * WINDOW
    g = pl.program_id(0)
    j = pl.program_id(1)
    n_tiles = pl.num_programs(1) - 1
    _, tile = _da_seq_tile(j, n_t, n_tiles)
    _, prev_tile = _da_seq_tile(j - 1, n_t, n_tiles)

    def window_start(t):
        return pl.multiple_of(jnp.clip(t * tq - WINDOW, 0, S - span), LANES)

    def variant_of(t):
        return jnp.where(t == 0, 0, jnp.where(t == n_t - 1, 2, 1))

    @pl.when(j == 0)
    def _():
        s_scr[...] = jnp.zeros_like(s_scr)
        m_scr[...] = jnp.zeros_like(m_scr)

    tab_row = pl.multiple_of(2 * WINDOW - variant_of(tile) * WINDOW, LANES)
    kwin = k_ref[0, pl.ds(window_start(tile), span), :]
    kb0 = window_start(prev_tile) // LANES
    heads = WA_KV_PER_STEP * WA_GROUPS
    n_kb = span // LANES
    n_qb = tq // LANES

    def step(prev_variant):
        prev_shift = prev_variant * WINDOW

        def probs(jh, kb, m):
            cols = []
            for qb in range(n_qb):
                if abs(prev_shift + (qb - kb) * LANES) < 2 * LANES:
                    x = s_scr[jh, kb * LANES:(kb + 1) * LANES, qb * LANES:(qb + 1) * LANES]
                    cols.append(jnp.exp2((x - m[:, qb * LANES:(qb + 1) * LANES]).astype(BF16)))
                else:
                    cols.append(jnp.zeros((LANES, LANES), BF16))
            return jnp.concatenate(cols, axis=1)

        outs = []
        for jh in range(heads):
            kv, pair_half = divmod(jh, WA_GROUPS)
            pair, half = divmod(pair_half, 2)
            sink = sink_ref[g * heads + jh] * LOG2E
            m = m_scr[jh:jh + 1, :]
            qcol = (kv * 2 + pair) * LANES
            kcol = (kv * 2 + half) * LANES
            qp = q_ref[0, :, qcol:qcol + LANES]
            ps = [probs(jh, kb, m) for kb in range(n_kb // 2)]
            s = _dot_nt(kwin[:, kcol:kcol + LANES], qp) + tab_ref[jh, pl.ds(tab_row, span), :]
            ps += [probs(jh, kb, m) for kb in range(n_kb // 2, n_kb)]
            acc = None
            for kb, p in enumerate(ps):
                d = _dot(vt_ref[0, kv, kb0 + kb], p)
                acc = d if acc is None else acc + d
            l = acc[WA_HEAD_DIM:WA_HEAD_DIM + 1, :] + jnp.exp2(sink - m)
            outs.append(acc[0:WA_HEAD_DIM, :] * (1.0 / l))
            s_scr[jh] = s
            m_scr[jh:jh + 1, :] = jnp.maximum(jnp.max(s, axis=0, keepdims=True), sink)
        o_ref[0] = jnp.concatenate(outs, axis=0).T.astype(BF16)

    prev_variant = variant_of(prev_tile)
    for v in range(3):
        pl.when(prev_variant == v)(functools.partial(step, v))


def _win_attn(waq, wak, wavt, sink, tq):
    B, S, D = waq.shape
    n_t = S // tq
    span = tq + 2 * WINDOW
    assert n_t >= 2 and S >= span, (S, tq)
    heads = WA_KV_PER_STEP * WA_GROUPS
    gw = heads * WA_HEAD_DIM
    slopes = jnp.exp2(-8.0 * jnp.arange(1, WA_HEADS + 1, dtype=F32) / WA_HEADS) * LOG2E
    r = jnp.arange(span + 2 * WINDOW, dtype=jnp.int32)[:, None]
    ir = jnp.arange(tq, dtype=jnp.int32)[None, :]
    dist = jnp.abs(2 * WINDOW + ir - r)
    tab = jnp.where((dist <= WINDOW)[None], -(slopes[:, None, None] * dist.astype(F32)[None]), NEG_INF)
    vt_rows = wavt.shape[3]
    n_tiles = B * n_t

    def q_map(g, j):
        seq, tile = _da_seq_tile(j, n_t, n_tiles)
        return (seq, tile, g)

    def o_map(g, j):
        seq, tile = _da_seq_tile(j - 1, n_t, n_tiles)
        return (seq, tile, g)

    return pl.pallas_call(
        functools.partial(_win_attn_kernel, n_t=n_t),
        grid=(WA_KV_HEADS // WA_KV_PER_STEP, n_tiles + 1),
        in_specs=[pl.BlockSpec(memory_space=pltpu.SMEM),
                  pl.BlockSpec((1, tq, gw), q_map),
                  pl.BlockSpec((1, S, gw), lambda g, j: (_da_seq_tile(j, n_t, n_tiles)[0], 0, g)),
                  pl.BlockSpec((1, WA_KV_PER_STEP, S // LANES, vt_rows, LANES),
                               lambda g, j: (_da_seq_tile(j - 1, n_t, n_tiles)[0], g, 0, 0, 0)),
                  pl.BlockSpec((heads, span + 2 * WINDOW, tq), lambda g, j: (g, 0, 0),
                               pipeline_mode=pl.Buffered(1))],
        out_specs=pl.BlockSpec((1, tq, gw), o_map),
        out_shape=jax.ShapeDtypeStruct((B, S, D), BF16),
        scratch_shapes=[pltpu.VMEM((heads, span, tq), F32), pltpu.VMEM((heads, tq), F32)],
        compiler_params=_cparams(("parallel", "arbitrary")),
        name="win_attn",
    )(sink, waq, wak, wavt, tab)


def _mem_attn_head(q_ref, kv_ref, hh):
    hd = XA_HEAD_DIM
    q = q_ref[0, :, hh * hd:(hh + 1) * hd]
    k = kv_ref[0, :, hh * hd:(hh + 1) * hd]
    v = kv_ref[0, :, (XA_HEADS + hh) * hd:(XA_HEADS + hh + 1) * hd]
    s = _dot_nt(q, k)
    p = jnp.exp2(s - jnp.max(s, axis=1, keepdims=True))
    pn = (p * (1.0 / jnp.sum(p, axis=1, keepdims=True))).astype(BF16)
    return _dot(pn, v).astype(BF16)


def _merge_kernel(x_ref, oda_ref, owa_ref, xaq_ref, memkv_ref, g1_ref, wg_ref, wda_ref, wwa_ref, wxa_ref,
                  wout_ref, g2_ref, wr_ref, x1_ref, h2_ref, aff_ref, x1_scr):
    d = x_ref.shape[-1]
    hd = XA_HEAD_DIM

    @pl.when(pl.program_id(0) == 0)
    def _():
        x1_scr[...] = jnp.zeros_like(x1_scr)

    def previous_tile_tail():
        h2 = _rms(x1_scr[...], g2_ref[...]).astype(BF16)
        h2_ref[0] = h2
        logits = _dot(h2, wr_ref[...])
        lt = logits.T[:N_EXPERTS, :]
        e = jnp.exp(lt - jnp.max(lt, axis=0, keepdims=True))
        aff_ref[0] = e / jnp.sum(e, axis=0, keepdims=True)

    x = x_ref[0]
    h = _rms(x, g1_ref[...]).astype(BF16)

    def gated(i, branch):
        return jax.nn.sigmoid(_dot(h, wg_ref[:, i * d:(i + 1) * d])) * branch

    xa = [_mem_attn_head(xaq_ref, memkv_ref, 0)]
    merged = gated(0, _dot(oda_ref[0], wda_ref[...]))
    xa.append(_mem_attn_head(xaq_ref, memkv_ref, 1))
    merged = merged + gated(1, _dot(owa_ref[0], wwa_ref[...]))
    xa.extend(_mem_attn_head(xaq_ref, memkv_ref, hh) for hh in range(2, XA_HEADS))
    xa_proj = None
    for hh, o in enumerate(xa):
        part = _dot(o, wxa_ref[hh * hd:(hh + 1) * hd, :])
        xa_proj = part if xa_proj is None else xa_proj + part
    merged = merged + gated(2, xa_proj)
    previous_tile_tail()
    x1 = x + _dot(merged.astype(BF16), wout_ref[...])
    x1_ref[0] = x1
    x1_scr[...] = x1


def _merge(x, oda, owa, xaq, memkv, g1, wg, wda, wwa, wxa, wout, g2, wr, tm):
    B, S, D = x.shape
    n_t = S // tm
    n_tiles = B * n_t

    def cur(j):
        return _da_seq_tile(j, n_t, n_tiles)

    def prev(j):
        return _da_seq_tile(j - 1, n_t, n_tiles)

    tok = pl.BlockSpec((1, tm, D), lambda j: cur(j) + (0,))
    tok_prev = pl.BlockSpec((1, tm, D), lambda j: prev(j) + (0,))
    full = lambda a: _resident(a.shape, lambda j: (0,) * a.ndim)
    return pl.pallas_call(
        _merge_kernel,
        grid=(n_tiles + 1,),
        in_specs=[tok, tok, tok, tok, pl.BlockSpec((1,) + memkv.shape[1:], lambda j: (cur(j)[0], 0, 0)),
                  full(g1), full(wg), full(wda), full(wwa), full(wxa), full(wout), full(g2), full(wr)],
        out_specs=[tok, tok_prev, pl.BlockSpec((1, N_EXPERTS, tm), lambda j: (prev(j)[0], 0, prev(j)[1]))],
        out_shape=[jax.ShapeDtypeStruct((B, S, D), F32),
                   jax.ShapeDtypeStruct((B, S, D), BF16),
                   jax.ShapeDtypeStruct((B, N_EXPERTS, S), F32)],
        scratch_shapes=[pltpu.VMEM((tm, D), F32)],
        compiler_params=_cparams(("arbitrary",)),
        name="merge",
    )(x, oda, owa, xaq, memkv, g1, wg, wda, wwa, wxa, wout, g2, wr)


def _route_kernel(aff_ref, pos_ref, *, cap):
    nb, ne, S = aff_ref.shape
    E = nb * ne
    aff = aff_ref[...].reshape(E, S)
    bits = pltpu.bitcast(aff, jnp.int32)
    prefix = jnp.zeros((E, 1), jnp.int32)
    for bit in range(30, -1, -1):
        cand = prefix | (1 << bit)
        cnt = jnp.sum(jnp.where(bits >= cand, 1.0, 0.0), axis=1, keepdims=True)
        prefix = jnp.where(cnt >= cap, cand, prefix)
    gt = bits > prefix
    eq = bits == prefix
    blk = 2 * LANES
    tri = (lax.broadcasted_iota(jnp.int32, (blk, blk), 0) <= lax.broadcasted_iota(jnp.int32, (blk, blk), 1))
    tri = jnp.where(tri, 1.0, 0.0).astype(BF16)

    def cumsum(mask):
        mb = jnp.where(mask, 1.0, 0.0).astype(BF16)
        carry = jnp.zeros((E, 1), F32)
        outs = []
        for i in range(S // blk):
            cs = _dot(mb[:, i * blk:(i + 1) * blk], tri) + carry
            outs.append(cs)
            carry = cs[:, blk - 1:blk]
        return jnp.concatenate(outs, axis=1), carry

    cs_gt, n_gt = cumsum(gt)
    cs_eq, _ = cumsum(eq)
    need = cap - n_gt
    sel = gt | (eq & (cs_eq <= need))
    slot = cs_gt + jnp.minimum(cs_eq, need) - 1.0
    pos_ref[...] = jnp.where(sel, slot, -1.0).astype(jnp.int32).reshape(nb, ne, S)


def _route(aff, cap):
    B, E, S = aff.shape
    nb = ROUTE_SEQS_PER_STEP if B % ROUTE_SEQS_PER_STEP == 0 else 1
    return pl.pallas_call(
        functools.partial(_route_kernel, cap=cap),
        grid=(B // nb,),
        in_specs=[pl.BlockSpec((nb, E, S), lambda b: (b, 0, 0))],
        out_specs=pl.BlockSpec((nb, E, S), lambda b: (b, 0, 0)),
        out_shape=jax.ShapeDtypeStruct((B, E, S), jnp.int32),
        compiler_params=_cparams(("parallel",)),
        name="route",
    )(aff)


def _moe_kernel(pos_ref, aff_ref, h2_ref, wg_ref, wu_ref, wd_ref, o_ref, p_scr, g_scr, xe_scr, y_scr, *, cap):
    e = pl.program_id(1)
    f = pl.program_id(2)
    nf = pl.num_programs(2)
    S = h2_ref.shape[1]

    @pl.when((e == 0) & (f == 0))
    def _():
        o_ref[...] = jnp.zeros_like(o_ref)

    @pl.when(f == 0)
    def _():
        slot = lax.broadcasted_iota(jnp.int32, (cap, S), 0)
        hit = pos_ref[0, 0] == slot
        onehot = jnp.where(hit, 1.0, 0.0).astype(BF16)
        p_scr[...] = onehot
        g_scr[...] = jnp.sum(jnp.where(hit, aff_ref[0, 0], 0.0), axis=1, keepdims=True)
        xe_scr[...] = _dot(onehot, h2_ref[0]).astype(BF16)
        y_scr[...] = jnp.zeros_like(y_scr)

    xe = xe_scr[...]
    a = _dot(xe, wg_ref[0])
    u = _dot(xe, wu_ref[0])
    act = (a * jax.nn.sigmoid(a) * u).astype(BF16)
    y_scr[...] += _dot(act, wd_ref[0])

    @pl.when(f == nf - 1)
    def _():
        ye = (y_scr[...] * g_scr[...]).astype(BF16)
        o_ref[0] += _dot_tn(p_scr[...], ye)


def _moe(pos, aff, h2, wg, wu, wd, cap, tf):
    B, S, D = h2.shape
    E, _, F = wg.shape
    pos4 = pos.reshape(B, E, 1, S)
    aff4 = aff.reshape(B, E, 1, S)
    row_spec = pl.BlockSpec((1, 1, 1, S), lambda b, e, f: (b, e, 0, 0))
    return pl.pallas_call(
        functools.partial(_moe_kernel, cap=cap),
        grid=(B, E, F // tf),
        in_specs=[row_spec, row_spec,
                  pl.BlockSpec((1, S, D), lambda b, e, f: (b, 0, 0)),
                  pl.BlockSpec((1, D, tf), lambda b, e, f: (e, 0, f)),
                  pl.BlockSpec((1, D, tf), lambda b, e, f: (e, 0, f)),
                  pl.BlockSpec((1, tf, D), lambda b, e, f: (e, f, 0))],
        out_specs=pl.BlockSpec((1, S, D), lambda b, e, f: (b, 0, 0)),
        out_shape=jax.ShapeDtypeStruct((B, S, D), F32),
        scratch_shapes=[pltpu.VMEM((cap, S), BF16), pltpu.VMEM((cap, 1), F32),
                        pltpu.VMEM((cap, D), BF16), pltpu.VMEM((cap, D), F32)],
        compiler_params=_cparams(("parallel", "arbitrary", "arbitrary")),
        name="moe",
    )(pos4, aff4, h2, wg, wu, wd)


def _residual_kernel(x_ref, d_ref, o_ref):
    o_ref[0] = x_ref[0] + d_ref[0]


def _residual_norm_kernel(x_ref, d_ref, g_ref, o_ref):
    o_ref[0] = _rms(x_ref[0] + d_ref[0], g_ref[...])


def _residual(x1, delta, g, tm):
    B, S, D = x1.shape
    tok = pl.BlockSpec((1, tm, D), lambda b, i: (b, i, 0))
    if g is None:
        body, extra, extra_specs = _residual_kernel, (), []
    else:
        body, extra, extra_specs = _residual_norm_kernel, (g,), [_resident((1, D), lambda b, i: (0, 0))]
    return pl.pallas_call(
        body,
        grid=(B, S // tm),
        in_specs=[tok, tok] + extra_specs,
        out_specs=tok,
        out_shape=jax.ShapeDtypeStruct((B, S, D), F32),
        compiler_params=_cparams(("parallel", "parallel")),
        name="residual",
    )(x1, delta, *extra)


def _tile(n, pref):
    t = min(n, pref)
    assert n % t == 0, (n, t)
    return t


def _lambda_init(layer):
    return 0.8 - 0.6 * float(np.exp(-0.3 * layer))


def _lane_half_forms(w, n_heads, width):
    d = w.shape[0]
    w = w.reshape(d, n_heads, width)
    z = jnp.zeros_like(w)
    return jnp.concatenate([w, z, z, w], axis=-1).reshape(d, n_heads * 4 * width)


def kernel(x, mem, attn_norm_g, mem_norm_g, w_in, w_mem_kv, da_lambda_q1, da_lambda_k1, da_lambda_q2,
           da_lambda_k2, da_subln_g, wa_sink, w_da_o, w_wa_o, w_xa_o, w_out, ffn_norm_g, w_router,
           w_exp_gate, w_exp_up, w_exp_down, final_norm_g):
    B, S, D = x.shape
    depth = w_in.shape[0]
    cap = max(1, EC_FACTOR * S // N_EXPERTS)
    row = lambda v: v.reshape(1, -1).astype(F32)
    da_w = DA_HEADS * 2 * DA_HEAD_DIM
    wa_kv_w = WA_KV_HEADS * WA_HEAD_DIM
    o_waq = 3 * da_w
    o_wak = o_waq + WA_HEADS * WA_HEAD_DIM
    o_wav = o_wak + wa_kv_w
    o_xaq = o_wav + wa_kv_w
    o_gate = o_xaq + XA_HEADS * XA_HEAD_DIM
    for l in range(depth):
        lam_init = _lambda_init(l)
        wl = w_in[l]
        w1 = jnp.concatenate([wl[:, :o_wak],
                              _lane_half_forms(wl[:, o_wak:o_wav], WA_KV_HEADS, WA_HEAD_DIM),
                              wl[:, o_wav:o_gate]], axis=1).astype(BF16)
        wg = wl[:, o_gate:].astype(BF16)
        wr = jnp.pad(w_router[l], ((0, 0), (0, LANES - N_EXPERTS))).astype(BF16)

        daq, dak, dav, waq, wak, wavt, xaq = _in_proj(x, row(attn_norm_g[l]), w1, _tile(S, 1024))
        memkv = _mem_proj(mem, row(mem_norm_g[l]), w_mem_kv[l].astype(BF16))
        lams = [row(v[l]) for v in (da_lambda_q1, da_lambda_k1, da_lambda_q2, da_lambda_k2)]
        oda = _diff_attn(daq, dak, dav, lams, da_subln_g[l].astype(F32), lam_init, _tile(S, 1024))
        owa = _win_attn(waq, wak, wavt, wa_sink[l].astype(F32), _tile(S, 256))
        x1, h2, aff = _merge(x, oda, owa, xaq, memkv, row(attn_norm_g[l]), wg, w_da_o[l].astype(BF16),
                             w_wa_o[l].astype(BF16), w_xa_o[l].astype(BF16), w_out[l].astype(BF16),
                             row(ffn_norm_g[l]), wr, _tile(S, 512))
        pos = _route(aff, cap)
        delta = _moe(pos, aff, h2, w_exp_gate[l].astype(BF16), w_exp_up[l].astype(BF16),
                     w_exp_down[l].astype(BF16), cap, _tile(w_exp_gate.shape[-1], 2048))
        last = l == depth - 1
        x = _residual(x1, delta, row(final_norm_g) if last else None, _tile(S, 1024))
    return x
```

```python
import functools

import numpy as np
import jax
import jax.numpy as jnp
from jax import lax
from jax.experimental import pallas as pl
from jax.experimental.pallas import tpu as pltpu

EPS = 1e-6
WINDOW = 128
DA_HEADS = 8
DA_HEAD_DIM = 64
DA_V_DIM = 2 * DA_HEAD_DIM
WA_HEADS = 16
WA_KV_HEADS = 4
WA_GROUPS = WA_HEADS // WA_KV_HEADS
WA_HEAD_DIM = 64
XA_HEADS = 4
XA_HEAD_DIM = 256
N_EXPERTS = 16
EC_FACTOR = 2

LOG2E = 1.4426950408889634
DA_Q_SCALE = DA_HEAD_DIM ** -0.5 * LOG2E
XA_Q_SCALE = XA_HEAD_DIM ** -0.5 * LOG2E
WA_Q_SCALE = WA_HEAD_DIM ** -0.5 * LOG2E
WA_ONES_ROWS = 16
WA_KV_PER_STEP = 4
DA_KEY_CHUNK = 256
DA_ONES_ROWS = 16
DA_DRAIN_STEPS = 2
ROUTE_SEQS_PER_STEP = 4
MEM_SEQS_PER_STEP = 4

LANES = 128
VMEM_LIMIT = 56 * 1024 * 1024

F32 = jnp.float32
BF16 = jnp.bfloat16
NEG_INF = float("-inf")


def _cparams(sem):
    return pltpu.CompilerParams(dimension_semantics=sem, vmem_limit_bytes=VMEM_LIMIT)


def _rms(xf, g_row):
    ms = jnp.mean(xf * xf, axis=-1, keepdims=True)
    return xf * lax.rsqrt(ms + EPS) * g_row


def _dot(a, b):
    return jnp.dot(a, b, preferred_element_type=F32)


def _dot_nt(a, b):
    return lax.dot_general(a, b, (((1,), (1,)), ((), ())), preferred_element_type=F32)


def _dot_tn(a, b):
    return lax.dot_general(a, b, (((0,), (0,)), ((), ())), preferred_element_type=F32)


def _resident(shape, index_map):
    return pl.BlockSpec(shape, index_map, pipeline_mode=pl.Buffered(1))


def _in_proj_kernel(x_ref, g_ref, w_ref, daq_ref, dak_ref, dav_ref, waq_ref, wak_ref, wavt_ref, xaq_ref):
    d = x_ref.shape[-1]
    tm = x_ref.shape[1]
    h = _rms(x_ref[0], g_ref[...]).astype(BF16)
    col = [0]

    def sec(width):
        r = _dot(h, w_ref[:, col[0]:col[0] + width])
        col[0] += width
        return r

    for ref, scale in ((daq_ref, DA_Q_SCALE), (dak_ref, None), (dav_ref, None)):
        r = sec(d)
        r = (r if scale is None else r * scale).astype(BF16)
        for hh in range(DA_HEADS):
            ref[0, hh] = r[:, hh * LANES:(hh + 1) * LANES]
    waq_ref[0] = (sec(d) * WA_Q_SCALE).astype(BF16)
    wak_ref[0] = sec(d).astype(BF16)
    vt = sec(WA_KV_HEADS * WA_HEAD_DIM).T
    ones = jnp.ones((WA_ONES_ROWS, LANES), BF16)
    for hh in range(WA_KV_HEADS):
        for kb in range(tm // LANES):
            wavt_ref[0, hh, kb, 0:WA_HEAD_DIM, :] = vt[hh * WA_HEAD_DIM:(hh + 1) * WA_HEAD_DIM,
                                                       kb * LANES:(kb + 1) * LANES].astype(BF16)
            wavt_ref[0, hh, kb, WA_HEAD_DIM:, :] = ones
    xaq_ref[0] = (sec(d) * XA_Q_SCALE).astype(BF16)


def _in_proj(x, g, w1, tm):
    B, S, D = x.shape
    n_t = S // tm
    head_shape = jax.ShapeDtypeStruct((B, DA_HEADS, S, LANES), BF16)
    head_spec = pl.BlockSpec((1, DA_HEADS, tm, LANES), lambda b, i: (b, 0, i, 0))
    tok = pl.BlockSpec((1, tm, D), lambda b, i: (b, i, 0))
    tok_shape = jax.ShapeDtypeStruct((B, S, D), BF16)
    vt_rows = WA_HEAD_DIM + WA_ONES_ROWS
    return pl.pallas_call(
        _in_proj_kernel,
        grid=(B, n_t),
        in_specs=[tok, _resident((1, D), lambda b, i: (0, 0)), _resident(w1.shape, lambda b, i: (0, 0))],
        out_specs=[head_spec, head_spec, head_spec, tok, tok,
                   pl.BlockSpec((1, WA_KV_HEADS, tm // LANES, vt_rows, LANES), lambda b, i: (b, 0, i, 0, 0)),
                   tok],
        out_shape=[head_shape, head_shape, head_shape, tok_shape, tok_shape,
                   jax.ShapeDtypeStruct((B, WA_KV_HEADS, S // LANES, vt_rows, LANES), BF16),
                   tok_shape],
        compiler_params=_cparams(("parallel", "parallel")),
        name="in_proj",
    )(x, g, w1)


def _mem_proj_kernel(m_ref, g_ref, w_ref, o_ref):
    nb, rows, d = m_ref.shape
    h = _rms(m_ref[...].reshape(nb * rows, d), g_ref[...]).astype(BF16)
    o_ref[...] = _dot(h, w_ref[...]).astype(BF16).reshape(nb, rows, -1)


def _mem_proj(mem, g, w):
    B, M, D = mem.shape
    N = w.shape[1]
    nb = MEM_SEQS_PER_STEP if B % MEM_SEQS_PER_STEP == 0 else 1
    return pl.pallas_call(
        _mem_proj_kernel,
        grid=(B // nb,),
        in_specs=[pl.BlockSpec((nb, M, D), lambda b: (b, 0, 0)),
                  _resident((1, D), lambda b: (0, 0)),
                  _resident((D, N), lambda b: (0, 0))],
        out_specs=pl.BlockSpec((nb, M, N), lambda b: (b, 0, 0)),
        out_shape=jax.ShapeDtypeStruct((B, M, N), BF16),
        compiler_params=_cparams(("parallel",)),
        name="mem_proj",
    )(mem, g, w)


def _diff_attn_kernel(lq1_ref, lk1_ref, lq2_ref, lk2_ref, q_ref, k_ref, v_ref, bias_ref, g_ref, o_ref,
                      k1_scr, k2_scr, vt_scr, s_scr, m_scr, e_scr, *, lam_init, n_t):
    S = k_ref.shape[2]
    tq = q_ref.shape[2]
    j = pl.program_id(1)
    n_tiles = pl.num_programs(1) - DA_DRAIN_STEPS
    ck = DA_KEY_CHUNK
    seq, tile = _da_seq_tile(j, n_t, n_tiles)
    prev_seq, _ = _da_seq_tile(j - 1, n_t, n_tiles)

    @pl.when((tile == 0) & (j < n_tiles))
    def _():
        kk = k_ref[0, 0]
        lane = lax.broadcasted_iota(jnp.int32, kk.shape, 1)
        zero = jnp.zeros_like(kk)
        k1_scr[...] = jnp.where(lane < DA_HEAD_DIM, kk, zero)
        k2_scr[...] = jnp.where(lane >= DA_HEAD_DIM, kk, zero)
        vt_scr[seq % 2, 0:DA_V_DIM, :] = v_ref[0, 0].astype(F32).T.astype(BF16)
        vt_scr[seq % 2, DA_V_DIM:, :] = jnp.ones((DA_ONES_ROWS, S), BF16)

    @pl.when(j == 0)
    def _():
        s_scr[...] = jnp.zeros_like(s_scr)
        m_scr[...] = jnp.zeros_like(m_scr)
        e_scr[...] = jnp.ones_like(e_scr)

    lam = (jnp.exp(jnp.sum(lq1_ref[...] * lk1_ref[...], keepdims=True))
           - jnp.exp(jnp.sum(lq2_ref[...] * lk2_ref[...], keepdims=True)) + lam_init)

    q = q_ref[0, 0]
    r0 = (S - tq) - tile * tq
    vt_slot = prev_seq % 2

    def stage3():
        o1, l1 = e_scr[0, 0:DA_V_DIM, :], e_scr[0, DA_V_DIM:DA_V_DIM + 1, :]
        o2, l2 = e_scr[1, 0:DA_V_DIM, :], e_scr[1, DA_V_DIM:DA_V_DIM + 1, :]
        ot = o1 * (1.0 / l1) - o2 * (lam / l2)
        ms = jnp.mean(ot * ot, axis=0, keepdims=True)
        y = ot * lax.rsqrt(ms + EPS) * g_ref[...] * (1.0 - lam_init)
        o_ref[0] = y.T.astype(BF16)

    m_prev = (m_scr[0:1, :], m_scr[1:2, :])
    m_new = [None, None]
    acc = [None, None]
    n_chunks = S // ck
    for c in range(n_chunks):
        if c == n_chunks // 2:
            stage3()
        rows = slice(c * ck, (c + 1) * ck)
        bias = bias_ref[0, pl.ds(pl.multiple_of(r0 + c * ck, ck), ck), :]
        half = ck // 2
        for i, k_scr in enumerate((k1_scr, k2_scr)):
            p_lo = jnp.exp2((s_scr[i, c * ck:c * ck + half, :] - m_prev[i]).astype(BF16))
            s = _dot_nt(k_scr[rows, :], q) + bias
            p_hi = jnp.exp2((s_scr[i, c * ck + half:(c + 1) * ck, :] - m_prev[i]).astype(BF16))
            d = _dot(vt_scr[vt_slot, :, rows], jnp.concatenate([p_lo, p_hi], axis=0))
            acc[i] = d if acc[i] is None else acc[i] + d
            s_scr[i, rows, :] = s
            cm = jnp.max(s, axis=0, keepdims=True)
            m_new[i] = cm if m_new[i] is None else jnp.maximum(m_new[i], cm)
    m_scr[0:1, :] = m_new[0]
    m_scr[1:2, :] = m_new[1]
    e_scr[0] = acc[0]
    e_scr[1] = acc[1]


def _da_seq_tile(j, n_t, n_tiles):
    jc = jnp.clip(j, 0, n_tiles - 1)
    return jc // n_t, jc % n_t


def _diff_attn(daq, dak, dav, lams, subln_g, lam_init, tq):
    B, H, S, _ = daq.shape
    n_t = S // tq
    rows = 2 * S - tq
    slopes = jnp.exp2(-8.0 * jnp.arange(1, H + 1, dtype=F32) / H)
    r = jnp.arange(rows, dtype=jnp.int32)[:, None]
    c = jnp.arange(tq, dtype=jnp.int32)[None, :]
    dist = jnp.abs(r - (S - tq) - c).astype(F32)
    bias = -((slopes * LOG2E)[:, None, None] * dist[None])
    n_tiles = B * n_t
    lam_spec = _resident((1, DA_HEAD_DIM), lambda h, j: (0, 0))

    def kv_map(h, j):
        return (_da_seq_tile(j, n_t, n_tiles)[0], h, 0, 0)

    def q_map(h, j):
        seq, tile = _da_seq_tile(j, n_t, n_tiles)
        return (seq, h, tile, 0)

    def o_map(h, j):
        seq, tile = _da_seq_tile(j - DA_DRAIN_STEPS, n_t, n_tiles)
        return (seq, tile, h)

    kv_spec = pl.BlockSpec((1, 1, S, LANES), kv_map)
    acc_shape = pltpu.VMEM((2, DA_V_DIM + DA_ONES_ROWS, tq), F32)
    return pl.pallas_call(
        functools.partial(_diff_attn_kernel, lam_init=lam_init, n_t=n_t),
        grid=(H, n_tiles + DA_DRAIN_STEPS),
        in_specs=[lam_spec, lam_spec, lam_spec, lam_spec,
                  pl.BlockSpec((1, 1, tq, LANES), q_map),
                  kv_spec, kv_spec,
                  pl.BlockSpec((1, rows, tq), lambda h, j: (h, 0, 0), pipeline_mode=pl.Buffered(1)),
                  _resident((DA_V_DIM, 1), lambda h, j: (0, 0))],
        out_specs=pl.BlockSpec((1, tq, LANES), o_map),
        out_shape=jax.ShapeDtypeStruct((B, S, H * LANES), BF16),
        scratch_shapes=[pltpu.VMEM((S, LANES), BF16), pltpu.VMEM((S, LANES), BF16),
                        pltpu.VMEM((2, DA_V_DIM + DA_ONES_ROWS, S), BF16),
                        pltpu.VMEM((2, S, tq), F32), pltpu.VMEM((2, tq), F32), acc_shape],
        compiler_params=_cparams(("parallel", "arbitrary")),
        name="diff_attn",
    )(*lams, daq, dak, dav, bias, subln_g.reshape(DA_V_DIM, 1))


def _win_attn_kernel(sink_ref, q_ref, k_ref, vt_ref, tab_ref, o_ref, s_scr, m_scr, *, n_t):
    S = k_ref.shape[1]
    tq = q_ref.shape[1]
    span = tq + 2 * WINDOW
    g = pl.program_id(0)
    j = pl.program_id(1)
    n_tiles = pl.num_programs(1) - 1
    _, tile = _da_seq_tile(j, n_t, n_tiles)
    _, prev_tile = _da_seq_tile(j - 1, n_t, n_tiles)

    def window_start(t):
        return pl.multiple_of(jnp.clip(t * tq - WINDOW, 0, S - span), LANES)

    def variant_of(t):
        return jnp.where(t == 0, 0, jnp.where(t == n_t - 1, 2, 1))

    @pl.when(j == 0)
    def _():
        s_scr[...] = jnp.zeros_like(s_scr)
        m_scr[...] = jnp.zeros_like(m_scr)

    tab_row = pl.multiple_of(2 * WINDOW - variant_of(tile) * WINDOW, LANES)
    kwin = k_ref[0, pl.ds(window_start(tile), span), :]
    kb0 = window_start(prev_tile) // LANES
    heads = WA_KV_PER_STEP * WA_GROUPS
    n_kb = span // LANES
    n_qb = tq // LANES

    def step(prev_variant):
        prev_shift = prev_variant * WINDOW

        def probs(jh, kb, m):
            cols = []
            for qb in range(n_qb):
                if abs(prev_shift + (qb - kb) * LANES) < 2 * LANES:
                    x = s_scr[jh, kb * LANES:(kb + 1) * LANES, qb * LANES:(qb + 1) * LANES]
                    cols.append(jnp.exp2((x - m[:, qb * LANES:(qb + 1) * LANES]).astype(BF16)))
                else:
                    cols.append(jnp.zeros((LANES, LANES), BF16))
            return jnp.concatenate(cols, axis=1)

        outs = []
        for jh in range(heads):
            kv, pair_half = divmod(jh, WA_GROUPS)
            pair, half = divmod(pair_half, 2)
            sink = sink_ref[g * heads + jh] * LOG2E
            m = m_scr[jh:jh + 1, :]
            qcol = (kv * 2 + pair) * LANES
            kcol = (kv * 2 + half) * LANES
            qp = q_ref[0, :, qcol:qcol + LANES]
            ps = [probs(jh, kb, m) for kb in range(n_kb // 2)]
            s = _dot_nt(kwin[:, kcol:kcol + LANES], qp) + tab_ref[jh, pl.ds(tab_row, span), :]
            ps += [probs(jh, kb, m) for kb in range(n_kb // 2, n_kb)]
            acc = None
            for kb, p in enumerate(ps):
                d = _dot(vt_ref[0, kv, kb0 + kb], p)
                acc = d if acc is None else acc + d
            l = acc[WA_HEAD_DIM:WA_HEAD_DIM + 1, :] + jnp.exp2(sink - m)
            outs.append(acc[0:WA_HEAD_DIM, :] * (1.0 / l))
            s_scr[jh] = s
            m_scr[jh:jh + 1, :] = jnp.maximum(jnp.max(s, axis=0, keepdims=True), sink)
        o_ref[0] = jnp.concatenate(outs, axis=0).T.astype(BF16)

    prev_variant = variant_of(prev_tile)
    for v in range(3):
        pl.when(prev_variant == v)(functools.partial(step, v))


def _win_attn(waq, wak, wavt, sink, tq):
    B, S, D = waq.shape
    n_t = S // tq
    span = tq + 2 * WINDOW
    assert n_t >= 2 and S >= span, (S, tq)
    heads = WA_KV_PER_STEP * WA_GROUPS
    gw = heads * WA_HEAD_DIM
    slopes = jnp.exp2(-8.0 * jnp.arange(1, WA_HEADS + 1, dtype=F32) / WA_HEADS) * LOG2E
    r = jnp.arange(span + 2 * WINDOW, dtype=jnp.int32)[:, None]
    ir = jnp.arange(tq, dtype=jnp.int32)[None, :]
    dist = jnp.abs(2 * WINDOW + ir - r)
    tab = jnp.where((dist <= WINDOW)[None], -(slopes[:, None, None] * dist.astype(F32)[None]), NEG_INF)
    vt_rows = wavt.shape[3]
    n_tiles = B * n_t

    def q_map(g, j):
        seq, tile = _da_seq_tile(j, n_t, n_tiles)
        return (seq, tile, g)

    def o_map(g, j):
        seq, tile = _da_seq_tile(j - 1, n_t, n_tiles)
        return (seq, tile, g)

    return pl.pallas_call(
        functools.partial(_win_attn_kernel, n_t=n_t),
        grid=(WA_KV_HEADS // WA_KV_PER_STEP, n_tiles + 1),
        in_specs=[pl.BlockSpec(memory_space=pltpu.SMEM),
                  pl.BlockSpec((1, tq, gw), q_map),
                  pl.BlockSpec((1, S, gw), lambda g, j: (_da_seq_tile(j, n_t, n_tiles)[0], 0, g)),
                  pl.BlockSpec((1, WA_KV_PER_STEP, S // LANES, vt_rows, LANES),
                               lambda g, j: (_da_seq_tile(j - 1, n_t, n_tiles)[0], g, 0, 0, 0)),
                  pl.BlockSpec((heads, span + 2 * WINDOW, tq), lambda g, j: (g, 0, 0),
                               pipeline_mode=pl.Buffered(1))],
        out_specs=pl.BlockSpec((1, tq, gw), o_map),
        out_shape=jax.ShapeDtypeStruct((B, S, D), BF16),
        scratch_shapes=[pltpu.VMEM((heads, span, tq), F32), pltpu.VMEM((heads, tq), F32)],
        compiler_params=_cparams(("parallel", "arbitrary")),
        name="win_attn",
    )(sink, waq, wak, wavt, tab)


def _mem_attn_head(q_ref, kv_ref, hh):
    hd = XA_HEAD_DIM
    q = q_ref[0, :, hh * hd:(hh + 1) * hd]
    k = kv_ref[0, :, hh * hd:(hh + 1) * hd]
    v = kv_ref[0, :, (XA_HEADS + hh) * hd:(XA_HEADS + hh + 1) * hd]
    s = _dot_nt(q, k)
    p = jnp.exp2(s - jnp.max(s, axis=1, keepdims=True))
    pn = (p * (1.0 / jnp.sum(p, axis=1, keepdims=True))).astype(BF16)
    return _dot(pn, v).astype(BF16)


def _merge_kernel(x_ref, oda_ref, owa_ref, xaq_ref, memkv_ref, g1_ref, wg_ref, wda_ref, wwa_ref, wxa_ref,
                  wout_ref, g2_ref, wr_ref, x1_ref, h2_ref, aff_ref, x1_scr):
    d = x_ref.shape[-1]
    hd = XA_HEAD_DIM

    @pl.when(pl.program_id(0) == 0)
    def _():
        x1_scr[...] = jnp.zeros_like(x1_scr)

    def previous_tile_tail():
        h2 = _rms(x1_scr[...], g2_ref[...]).astype(BF16)
        h2_ref[0] = h2
        logits = _dot(h2, wr_ref[...])
        lt = logits.T[:N_EXPERTS, :]
        e = jnp.exp(lt - jnp.max(lt, axis=0, keepdims=True))
        aff_ref[0] = e / jnp.sum(e, axis=0, keepdims=True)

    x = x_ref[0]
    h = _rms(x, g1_ref[...]).astype(BF16)

    def gated(i, branch):
        return jax.nn.sigmoid(_dot(h, wg_ref[:, i * d:(i + 1) * d])) * branch

    xa = [_mem_attn_head(xaq_ref, memkv_ref, 0)]
    merged = gated(0, _dot(oda_ref[0], wda_ref[...]))
    xa.append(_mem_attn_head(xaq_ref, memkv_ref, 1))
    merged = merged + gated(1, _dot(owa_ref[0], wwa_ref[...]))
    xa.extend(_mem_attn_head(xaq_ref, memkv_ref, hh) for hh in range(2, XA_HEADS))
    xa_proj = None
    for hh, o in enumerate(xa):
        part = _dot(o, wxa_ref[hh * hd:(hh + 1) * hd, :])
        xa_proj = part if xa_proj is None else xa_proj + part
    merged = merged + gated(2, xa_proj)
    previous_tile_tail()
    x1 = x + _dot(merged.astype(BF16), wout_ref[...])
    x1_ref[0] = x1
    x1_scr[...] = x1


def _merge(x, oda, owa, xaq, memkv, g1, wg, wda, wwa, wxa, wout, g2, wr, tm):
    B, S, D = x.shape
    n_t = S // tm
    n_tiles = B * n_t

    def cur(j):
        return _da_seq_tile(j, n_t, n_tiles)

    def prev(j):
        return _da_seq_tile(j - 1, n_t, n_tiles)

    tok = pl.BlockSpec((1, tm, D), lambda j: cur(j) + (0,))
    tok_prev = pl.BlockSpec((1, tm, D), lambda j: prev(j) + (0,))
    full = lambda a: _resident(a.shape, lambda j: (0,) * a.ndim)
    return pl.pallas_call(
        _merge_kernel,
        grid=(n_tiles + 1,),
        in_specs=[tok, tok, tok, tok, pl.BlockSpec((1,) + memkv.shape[1:], lambda j: (cur(j)[0], 0, 0)),
                  full(g1), full(wg), full(wda), full(wwa), full(wxa), full(wout), full(g2), full(wr)],
        out_specs=[tok, tok_prev, pl.BlockSpec((1, N_EXPERTS, tm), lambda j: (prev(j)[0], 0, prev(j)[1]))],
        out_shape=[jax.ShapeDtypeStruct((B, S, D), F32),
                   jax.ShapeDtypeStruct((B, S, D), BF16),
                   jax.ShapeDtypeStruct((B, N_EXPERTS, S), F32)],
        scratch_shapes=[pltpu.VMEM((tm, D), F32)],
        compiler_params=_cparams(("arbitrary",)),
        name="merge",
    )(x, oda, owa, xaq, memkv, g1, wg, wda, wwa, wxa, wout, g2, wr)


def _route_kernel(aff_ref, pos_ref, *, cap):
    nb, ne, S = aff_ref.shape
    E = nb * ne
    aff = aff_ref[...].reshape(E, S)
    bits = pltpu.bitcast(aff, jnp.int32)
    prefix = jnp.zeros((E, 1), jnp.int32)
    for bit in range(30, -1, -1):
        cand = prefix | (1 << bit)
        cnt = jnp.sum(jnp.where(bits >= cand, 1.0, 0.0), axis=1, keepdims=True)
        prefix = jnp.where(cnt >= cap, cand, prefix)
    gt = bits > prefix
    eq = bits == prefix
    blk = 2 * LANES
    tri = (lax.broadcasted_iota(jnp.int32, (blk, blk), 0) <= lax.broadcasted_iota(jnp.int32, (blk, blk), 1))
    tri = jnp.where(tri, 1.0, 0.0).astype(BF16)

    def cumsum(mask):
        mb = jnp.where(mask, 1.0, 0.0).astype(BF16)
        carry = jnp.zeros((E, 1), F32)
        outs = []
        for i in range(S // blk):
            cs = _dot(mb[:, i * blk:(i + 1) * blk], tri) + carry
            outs.append(cs)
            carry = cs[:, blk - 1:blk]
        return jnp.concatenate(outs, axis=1), carry

    cs_gt, n_gt = cumsum(gt)
    cs_eq, _ = cumsum(eq)
    need = cap - n_gt
    sel = gt | (eq & (cs_eq <= need))
    slot = cs_gt + jnp.minimum(cs_eq, need) - 1.0
    pos_ref[...] = jnp.where(sel, slot, -1.0).astype(jnp.int32).reshape(nb, ne, S)


def _route(aff, cap):
    B, E, S = aff.shape
    nb = ROUTE_SEQS_PER_STEP if B % ROUTE_SEQS_PER_STEP == 0 else 1
    return pl.pallas_call(
        functools.partial(_route_kernel, cap=cap),
        grid=(B // nb,),
        in_specs=[pl.BlockSpec((nb, E, S), lambda b: (b, 0, 0))],
        out_specs=pl.BlockSpec((nb, E, S), lambda b: (b, 0, 0)),
        out_shape=jax.ShapeDtypeStruct((B, E, S), jnp.int32),
        compiler_params=_cparams(("parallel",)),
        name="route",
    )(aff)


def _moe_kernel(pos_ref, aff_ref, h2_ref, wg_ref, wu_ref, wd_ref, o_ref, p_scr, g_scr, xe_scr, y_scr, *, cap):
    e = pl.program_id(1)
    f = pl.program_id(2)
    nf = pl.num_programs(2)
    S = h2_ref.shape[1]

    @pl.when((e == 0) & (f == 0))
    def _():
        o_ref[...] = jnp.zeros_like(o_ref)

    @pl.when(f == 0)
    def _():
        slot = lax.broadcasted_iota(jnp.int32, (cap, S), 0)
        hit = pos_ref[0, 0] == slot
        onehot = jnp.where(hit, 1.0, 0.0).astype(BF16)
        p_scr[...] = onehot
        g_scr[...] = jnp.sum(jnp.where(hit, aff_ref[0, 0], 0.0), axis=1, keepdims=True)
        xe_scr[...] = _dot(onehot, h2_ref[0]).astype(BF16)
        y_scr[...] = jnp.zeros_like(y_scr)

    xe = xe_scr[...]
    a = _dot(xe, wg_ref[0])
    u = _dot(xe, wu_ref[0])
    act = (a * jax.nn.sigmoid(a) * u).astype(BF16)
    y_scr[...] += _dot(act, wd_ref[0])

    @pl.when(f == nf - 1)
    def _():
        ye = (y_scr[...] * g_scr[...]).astype(BF16)
        o_ref[0] += _dot_tn(p_scr[...], ye)


def _moe(pos, aff, h2, wg, wu, wd, cap, tf):
    B, S, D = h2.shape
    E, _, F = wg.shape
    pos4 = pos.reshape(B, E, 1, S)
    aff4 = aff.reshape(B, E, 1, S)
    row_spec = pl.BlockSpec((1, 1, 1, S), lambda b, e, f: (b, e, 0, 0))
    return pl.pallas_call(
        functools.partial(_moe_kernel, cap=cap),
        grid=(B, E, F // tf),
        in_specs=[row_spec, row_spec,
                  pl.BlockSpec((1, S, D), lambda b, e, f: (b, 0, 0)),
                  pl.BlockSpec((1, D, tf), lambda b, e, f: (e, 0, f)),
                  pl.BlockSpec((1, D, tf), lambda b, e, f: (e, 0, f)),
                  pl.BlockSpec((1, tf, D), lambda b, e, f: (e, f, 0))],
        out_specs=pl.BlockSpec((1, S, D), lambda b, e, f: (b, 0, 0)),
        out_shape=jax.ShapeDtypeStruct((B, S, D), F32),
        scratch_shapes=[pltpu.VMEM((cap, S), BF16), pltpu.VMEM((cap, 1), F32),
                        pltpu.VMEM((cap, D), BF16), pltpu.VMEM((cap, D), F32)],
        compiler_params=_cparams(("parallel", "arbitrary", "arbitrary")),
        name="moe",
    )(pos4, aff4, h2, wg, wu, wd)


def _residual_kernel(x_ref, d_ref, o_ref):
    o_ref[0] = x_ref[0] + d_ref[0]


def _residual_norm_kernel(x_ref, d_ref, g_ref, o_ref):
    o_ref[0] = _rms(x_ref[0] + d_ref[0], g_ref[...])


def _residual(x1, delta, g, tm):
    B, S, D = x1.shape
    tok = pl.BlockSpec((1, tm, D), lambda b, i: (b, i, 0))
    if g is None:
        body, extra, extra_specs = _residual_kernel, (), []
    else:
        body, extra, extra_specs = _residual_norm_kernel, (g,), [_resident((1, D), lambda b, i: (0, 0))]
    return pl.pallas_call(
        body,
        grid=(B, S // tm),
        in_specs=[tok, tok] + extra_specs,
        out_specs=tok,
        out_shape=jax.ShapeDtypeStruct((B, S, D), F32),
        compiler_params=_cparams(("parallel", "parallel")),
        name="residual",
    )(x1, delta, *extra)


def _tile(n, pref):
    t = min(n, pref)
    assert n % t == 0, (n, t)
    return t


def _lambda_init(layer):
    return 0.8 - 0.6 * float(np.exp(-0.3 * layer))


def _lane_half_forms(w, n_heads, width):
    d = w.shape[0]
    w = w.reshape(d, n_heads, width)
    z = jnp.zeros_like(w)
    return jnp.concatenate([w, z, z, w], axis=-1).reshape(d, n_heads * 4 * width)


def kernel(x, mem, attn_norm_g, mem_norm_g, w_in, w_mem_kv, da_lambda_q1, da_lambda_k1, da_lambda_q2,
           da_lambda_k2, da_subln_g, wa_sink, w_da_o, w_wa_o, w_xa_o, w_out, ffn_norm_g, w_router,
           w_exp_gate, w_exp_up, w_exp_down, final_norm_g):
    B, S, D = x.shape
    depth = w_in.shape[0]
    cap = max(1, EC_FACTOR * S // N_EXPERTS)
    row = lambda v: v.reshape(1, -1).astype(F32)
    da_w = DA_HEADS * 2 * DA_HEAD_DIM
    wa_kv_w = WA_KV_HEADS * WA_HEAD_DIM
    o_waq = 3 * da_w
    o_wak = o_waq + WA_HEADS * WA_HEAD_DIM
    o_wav = o_wak + wa_kv_w
    o_xaq = o_wav + wa_kv_w
    o_gate = o_xaq + XA_HEADS * XA_HEAD_DIM
    for l in range(depth):
        lam_init = _lambda_init(l)
        wl = w_in[l]
        w1 = jnp.concatenate([wl[:, :o_wak],
                              _lane_half_forms(wl[:, o_wak:o_wav], WA_KV_HEADS, WA_HEAD_DIM),
                              wl[:, o_wav:o_gate]], axis=1).astype(BF16)
        wg = wl[:, o_gate:].astype(BF16)
        wr = jnp.pad(w_router[l], ((0, 0), (0, LANES - N_EXPERTS))).astype(BF16)

        daq, dak, dav, waq, wak, wavt, xaq = _in_proj(x, row(attn_norm_g[l]), w1, _tile(S, 1024))
        memkv = _mem_proj(mem, row(mem_norm_g[l]), w_mem_kv[l].astype(BF16))
        lams = [row(v[l]) for v in (da_lambda_q1, da_lambda_k1, da_lambda_q2, da_lambda_k2)]
        oda = _diff_attn(daq, dak, dav, lams, da_subln_g[l].astype(F32), lam_init, _tile(S, 1024))
        owa = _win_attn(waq, wak, wavt, wa_sink[l].astype(F32), _tile(S, 256))
        x1, h2, aff = _merge(x, oda, owa, xaq, memkv, row(attn_norm_g[l]), wg, w_da_o[l].astype(BF16),
                             w_wa_o[l].astype(BF16), w_xa_o[l].astype(BF16), w_out[l].astype(BF16),
                             row(ffn_norm_g[l]), wr, _tile(S, 512))
        pos = _route(aff, cap)
        delta = _moe(pos, aff, h2, w_exp_gate[l].astype(BF16), w_exp_up[l].astype(BF16),
                     w_exp_down[l].astype(BF16), cap, _tile(w_exp_gate.shape[-1], 2048))
        last = l == depth - 1
        x = _residual(x1, delta, row(final_norm_g) if last else None, _tile(S, 1024))
    return x
```

```python
import functools

import numpy as np
import jax
import jax.numpy as jnp
from jax import lax
from jax.experimental import pallas as pl
from jax.experimental.pallas import tpu as pltpu

EPS = 1e-6
WINDOW = 128
DA_HEADS = 8
DA_HEAD_DIM = 64
DA_V_DIM = 2 * DA_HEAD_DIM
WA_HEADS = 16
WA_KV_HEADS = 4
WA_GROUPS = WA_HEADS // WA_KV_HEADS
WA_HEAD_DIM = 64
XA_HEADS = 4
XA_HEAD_DIM = 256
N_EXPERTS = 16
EC_FACTOR = 2

LOG2E = 1.4426950408889634
DA_Q_SCALE = DA_HEAD_DIM ** -0.5 * LOG2E
XA_Q_SCALE = XA_HEAD_DIM ** -0.5 * LOG2E
WA_Q_SCALE = WA_HEAD_DIM ** -0.5 * LOG2E
WA_ONES_ROWS = 16
WA_KV_PER_STEP = 4
DA_KEY_CHUNK = 256
DA_ONES_ROWS = 16
DA_DRAIN_STEPS = 2
ROUTE_SEQS_PER_STEP = 4
MEM_SEQS_PER_STEP = 4

LANES = 128
VMEM_LIMIT = 56 * 1024 * 1024

F32 = jnp.float32
BF16 = jnp.bfloat16
NEG_INF = float("-inf")


def _cparams(sem):
    return pltpu.CompilerParams(dimension_semantics=sem, vmem_limit_bytes=VMEM_LIMIT)


def _rms(xf, g_row):
    ms = jnp.mean(xf * xf, axis=-1, keepdims=True)
    return xf * lax.rsqrt(ms + EPS) * g_row


def _dot(a, b):
    return jnp.dot(a, b, preferred_element_type=F32)


def _dot_nt(a, b):
    return lax.dot_general(a, b, (((1,), (1,)), ((), ())), preferred_element_type=F32)


def _dot_tn(a, b):
    return lax.dot_general(a, b, (((0,), (0,)), ((), ())), preferred_element_type=F32)


def _resident(shape, index_map):
    return pl.BlockSpec(shape, index_map, pipeline_mode=pl.Buffered(1))


def _in_proj_kernel(x_ref, g_ref, w_ref, daq_ref, dak_ref, dav_ref, waq_ref, wak_ref, wavt_ref, xaq_ref):
    d = x_ref.shape[-1]
    tm = x_ref.shape[1]
    h = _rms(x_ref[0], g_ref[...]).astype(BF16)
    col = [0]

    def sec(width):
        r = _dot(h, w_ref[:, col[0]:col[0] + width])
        col[0] += width
        return r

    for ref, scale in ((daq_ref, DA_Q_SCALE), (dak_ref, None)):
        r = sec(d)
        r = (r if scale is None else r * scale).astype(BF16)
        for hh in range(DA_HEADS):
            ref[0, hh] = r[:, hh * LANES:(hh + 1) * LANES]
    r = sec(d)
    for hh in range(DA_HEADS):
        dav_ref[0, hh, 0:DA_V_DIM, :] = r[:, hh * LANES:(hh + 1) * LANES].T.astype(BF16)
        dav_ref[0, hh, DA_V_DIM:, :] = jnp.ones((DA_ONES_ROWS, tm), BF16)
    waq_ref[0] = (sec(d) * WA_Q_SCALE).astype(BF16)
    wak_ref[0] = sec(d).astype(BF16)
    vt = sec(WA_KV_HEADS * WA_HEAD_DIM).T
    ones = jnp.ones((WA_ONES_ROWS, LANES), BF16)
    for hh in range(WA_KV_HEADS):
        for kb in range(tm // LANES):
            wavt_ref[0, hh, kb, 0:WA_HEAD_DIM, :] = vt[hh * WA_HEAD_DIM:(hh + 1) * WA_HEAD_DIM,
                                                       kb * LANES:(kb + 1) * LANES].astype(BF16)
            wavt_ref[0, hh, kb, WA_HEAD_DIM:, :] = ones
    xaq_ref[0] = (sec(d) * XA_Q_SCALE).astype(BF16)


def _in_proj(x, g, w1, tm):
    B, S, D = x.shape
    n_t = S // tm
    head_shape = jax.ShapeDtypeStruct((B, DA_HEADS, S, LANES), BF16)
    head_spec = pl.BlockSpec((1, DA_HEADS, tm, LANES), lambda b, i: (b, 0, i, 0))
    tok = pl.BlockSpec((1, tm, D), lambda b, i: (b, i, 0))
    tok_shape = jax.ShapeDtypeStruct((B, S, D), BF16)
    vt_rows = WA_HEAD_DIM + WA_ONES_ROWS
    return pl.pallas_call(
        _in_proj_kernel,
        grid=(B, n_t),
        in_specs=[tok, _resident((1, D), lambda b, i: (0, 0)), _resident(w1.shape, lambda b, i: (0, 0))],
        out_specs=[head_spec, head_spec,
                   pl.BlockSpec((1, DA_HEADS, DA_V_DIM + DA_ONES_ROWS, tm), lambda b, i: (b, 0, 0, i)),
                   tok, tok,
                   pl.BlockSpec((1, WA_KV_HEADS, tm // LANES, vt_rows, LANES), lambda b, i: (b, 0, i, 0, 0)),
                   tok],
        out_shape=[head_shape, head_shape,
                   jax.ShapeDtypeStruct((B, DA_HEADS, DA_V_DIM + DA_ONES_ROWS, S), BF16),
                   tok_shape, tok_shape,
                   jax.ShapeDtypeStruct((B, WA_KV_HEADS, S // LANES, vt_rows, LANES), BF16),
                   tok_shape],
        compiler_params=_cparams(("parallel", "parallel")),
        name="in_proj",
    )(x, g, w1)


def _mem_proj_kernel(m_ref, g_ref, w_ref, o_ref):
    nb, rows, d = m_ref.shape
    h = _rms(m_ref[...].reshape(nb * rows, d), g_ref[...]).astype(BF16)
    o_ref[...] = _dot(h, w_ref[...]).astype(BF16).reshape(nb, rows, -1)


def _mem_proj(mem, g, w):
    B, M, D = mem.shape
    N = w.shape[1]
    nb = MEM_SEQS_PER_STEP if B % MEM_SEQS_PER_STEP == 0 else 1
    return pl.pallas_call(
        _mem_proj_kernel,
        grid=(B // nb,),
        in_specs=[pl.BlockSpec((nb, M, D), lambda b: (b, 0, 0)),
                  _resident((1, D), lambda b: (0, 0)),
                  _resident((D, N), lambda b: (0, 0))],
        out_specs=pl.BlockSpec((nb, M, N), lambda b: (b, 0, 0)),
        out_shape=jax.ShapeDtypeStruct((B, M, N), BF16),
        compiler_params=_cparams(("parallel",)),
        name="mem_proj",
    )(mem, g, w)


def _diff_attn_kernel(lq1_ref, lk1_ref, lq2_ref, lk2_ref, q_ref, k_ref, vt_ref, bias_ref, g_ref, o_ref,
                      k1_scr, k2_scr, s_scr, m_scr, e_scr, *, lam_init, n_t):
    S = k_ref.shape[2]
    tq = q_ref.shape[2]
    j = pl.program_id(1)
    n_tiles = pl.num_programs(1) - DA_DRAIN_STEPS
    ck = DA_KEY_CHUNK
    _, tile = _da_seq_tile(j, n_t, n_tiles)

    @pl.when((tile == 0) & (j < n_tiles))
    def _():
        kk = k_ref[0, 0]
        lane = lax.broadcasted_iota(jnp.int32, kk.shape, 1)
        zero = jnp.zeros_like(kk)
        k1_scr[...] = jnp.where(lane < DA_HEAD_DIM, kk, zero)
        k2_scr[...] = jnp.where(lane >= DA_HEAD_DIM, kk, zero)

    @pl.when(j == 0)
    def _():
        s_scr[...] = jnp.zeros_like(s_scr)
        m_scr[...] = jnp.zeros_like(m_scr)
        e_scr[...] = jnp.ones_like(e_scr)

    lam = (jnp.exp(jnp.sum(lq1_ref[...] * lk1_ref[...], keepdims=True))
           - jnp.exp(jnp.sum(lq2_ref[...] * lk2_ref[...], keepdims=True)) + lam_init)

    q = q_ref[0, 0]
    r0 = (S - tq) - tile * tq

    def stage3():
        o1, l1 = e_scr[0, 0:DA_V_DIM, :], e_scr[0, DA_V_DIM:DA_V_DIM + 1, :]
        o2, l2 = e_scr[1, 0:DA_V_DIM, :], e_scr[1, DA_V_DIM:DA_V_DIM + 1, :]
        ot = o1 * (1.0 / l1) - o2 * (lam / l2)
        ms = jnp.mean(ot * ot, axis=0, keepdims=True)
        y = ot * lax.rsqrt(ms + EPS) * g_ref[...] * (1.0 - lam_init)
        o_ref[0] = y.T.astype(BF16)

    m_prev = (m_scr[0:1, :], m_scr[1:2, :])
    m_new = [None, None]
    acc = [None, None]
    n_chunks = S // ck
    for c in range(n_chunks):
        if c == n_chunks // 2:
            stage3()
        rows = slice(c * ck, (c + 1) * ck)
        bias = bias_ref[0, pl.ds(pl.multiple_of(r0 + c * ck, ck), ck), :]
        half = ck // 2
        for i, k_scr in enumerate((k1_scr, k2_scr)):
            p_lo = jnp.exp2((s_scr[i, c * ck:c * ck + half, :] - m_prev[i]).astype(BF16))
            s = _dot_nt(k_scr[rows, :], q) + bias
            p_hi = jnp.exp2((s_scr[i, c * ck + half:(c + 1) * ck, :] - m_prev[i]).astype(BF16))
            d = _dot(vt_ref[0, 0, :, rows], jnp.concatenate([p_lo, p_hi], axis=0))
            acc[i] = d if acc[i] is None else acc[i] + d
            s_scr[i, rows, :] = s
            cm = jnp.max(s, axis=0, keepdims=True)
            m_new[i] = cm if m_new[i] is None else jnp.maximum(m_new[i], cm)
    m_scr[0:1, :] = m_new[0]
    m_scr[1:2, :] = m_new[1]
    e_scr[0] = acc[0]
    e_scr[1] = acc[1]


def _da_seq_tile(j, n_t, n_tiles):
    jc = jnp.clip(j, 0, n_tiles - 1)
    return jc // n_t, jc % n_t


def _diff_attn(daq, dak, dav, lams, subln_g, lam_init, tq):
    B, H, S, _ = daq.shape
    n_t = S // tq
    rows = 2 * S - tq
    slopes = jnp.exp2(-8.0 * jnp.arange(1, H + 1, dtype=F32) / H)
    r = jnp.arange(rows, dtype=jnp.int32)[:, None]
    c = jnp.arange(tq, dtype=jnp.int32)[None, :]
    dist = jnp.abs(r - (S - tq) - c).astype(F32)
    bias = -((slopes * LOG2E)[:, None, None] * dist[None])
    n_tiles = B * n_t
    lam_spec = _resident((1, DA_HEAD_DIM), lambda h, j: (0, 0))

    def kv_map(h, j):
        return (_da_seq_tile(j, n_t, n_tiles)[0], h, 0, 0)

    def q_map(h, j):
        seq, tile = _da_seq_tile(j, n_t, n_tiles)
        return (seq, h, tile, 0)

    def o_map(h, j):
        seq, tile = _da_seq_tile(j - DA_DRAIN_STEPS, n_t, n_tiles)
        return (seq, tile, h)

    def vt_map(h, j):
        return (_da_seq_tile(j - 1, n_t, n_tiles)[0], h, 0, 0)

    kv_spec = pl.BlockSpec((1, 1, S, LANES), kv_map)
    vt_spec = pl.BlockSpec((1, 1, DA_V_DIM + DA_ONES_ROWS, S), vt_map)
    acc_shape = pltpu.VMEM((2, DA_V_DIM + DA_ONES_ROWS, tq), F32)
    return pl.pallas_call(
        functools.partial(_diff_attn_kernel, lam_init=lam_init, n_t=n_t),
        grid=(H, n_tiles + DA_DRAIN_STEPS),
        in_specs=[lam_spec, lam_spec, lam_spec, lam_spec,
                  pl.BlockSpec((1, 1, tq, LANES), q_map),
                  kv_spec, vt_spec,
                  pl.BlockSpec((1, rows, tq), lambda h, j: (h, 0, 0), pipeline_mode=pl.Buffered(1)),
                  _resident((DA_V_DIM, 1), lambda h, j: (0, 0))],
        out_specs=pl.BlockSpec((1, tq, LANES), o_map),
        out_shape=jax.ShapeDtypeStruct((B, S, H * LANES), BF16),
        scratch_shapes=[pltpu.VMEM((S, LANES), BF16), pltpu.VMEM((S, LANES), BF16),
                        pltpu.VMEM((2, S, tq), F32), pltpu.VMEM((2, tq), F32), acc_shape],
        compiler_params=_cparams(("parallel", "arbitrary")),
        name="diff_attn",
    )(*lams, daq, dak, dav, bias, subln_g.reshape(DA_V_DIM, 1))


def _win_attn_kernel(sink_ref, q_ref, k_ref, vt_ref, tab_ref, o_ref, s_scr, m_scr, *, n_t):
    S = k_ref.shape[1]
    tq = q_ref.shape[1]
    span = tq + 2 * WINDOW
    g = pl.program_id(0)
    j = pl.program_id(1)
    n_tiles = pl.num_programs(1) - 1
    _, tile = _da_seq_tile(j, n_t, n_tiles)
    _, prev_tile = _da_seq_tile(j - 1, n_t, n_tiles)

    def window_start(t):
        return pl.multiple_of(jnp.clip(t * tq - WINDOW, 0, S - span), LANES)

    def variant_of(t):
        return jnp.where(t == 0, 0, jnp.where(t == n_t - 1, 2, 1))

    @pl.when(j == 0)
    def _():
        s_scr[...] = jnp.zeros_like(s_scr)
        m_scr[...] = jnp.zeros_like(m_scr)

    tab_row = pl.multiple_of(2 * WINDOW - variant_of(tile) * WINDOW, LANES)
    kwin = k_ref[0, pl.ds(window_start(tile), span), :]
    kb0 = window_start(prev_tile) // LANES
    heads = WA_KV_PER_STEP * WA_GROUPS
    n_kb = span // LANES
    n_qb = tq // LANES

    def step(prev_variant):
        prev_shift = prev_variant * WINDOW

        def probs(jh, kb, m):
            cols = []
            for qb in range(n_qb):
                if abs(prev_shift + (qb - kb) * LANES) < 2 * LANES:
                    x = s_scr[jh, kb * LANES:(kb + 1) * LANES, qb * LANES:(qb + 1) * LANES]
                    cols.append(jnp.exp2((x - m[:, qb * LANES:(qb + 1) * LANES]).astype(BF16)))
                else:
                    cols.append(jnp.zeros((LANES, LANES), BF16))
            return jnp.concatenate(cols, axis=1)

        outs = []
        for jh in range(heads):
            kv, pair_half = divmod(jh, WA_GROUPS)
            pair, half = divmod(pair_half, 2)
            sink = sink_ref[g * heads + jh] * LOG2E
            m = m_scr[jh:jh + 1, :]
            qcol = (kv * 2 + pair) * LANES
            kcol = (kv * 2 + half) * LANES
            qp = q_ref[0, :, qcol:qcol + LANES]
            ps = [probs(jh, kb, m) for kb in range(n_kb // 2)]
            s = _dot_nt(kwin[:, kcol:kcol + LANES], qp) + tab_ref[jh, pl.ds(tab_row, span), :]
            ps += [probs(jh, kb, m) for kb in range(n_kb // 2, n_kb)]
            acc = None
            for kb, p in enumerate(ps):
                d = _dot(vt_ref[0, kv, kb0 + kb], p)
                acc = d if acc is None else acc + d
            l = acc[WA_HEAD_DIM:WA_HEAD_DIM + 1, :] + jnp.exp2(sink - m)
            outs.append(acc[0:WA_HEAD_DIM, :] * (1.0 / l))
            s_scr[jh] = s
            m_scr[jh:jh + 1, :] = jnp.maximum(jnp.max(s, axis=0, keepdims=True), sink)
        o_ref[0] = jnp.concatenate(outs, axis=0).T.astype(BF16)

    prev_variant = variant_of(prev_tile)
    for v in range(3):
        pl.when(prev_variant == v)(functools.partial(step, v))


def _win_attn(waq, wak, wavt, sink, tq):
    B, S, D = waq.shape
    n_t = S // tq
    span = tq + 2 * WINDOW
    assert n_t >= 2 and S >= span, (S, tq)
    heads = WA_KV_PER_STEP * WA_GROUPS
    gw = heads * WA_HEAD_DIM
    slopes = jnp.exp2(-8.0 * jnp.arange(1, WA_HEADS + 1, dtype=F32) / WA_HEADS) * LOG2E
    r = jnp.arange(span + 2 * WINDOW, dtype=jnp.int32)[:, None]
    ir = jnp.arange(tq, dtype=jnp.int32)[None, :]
    dist = jnp.abs(2 * WINDOW + ir - r)
    tab = jnp.where((dist <= WINDOW)[None], -(slopes[:, None, None] * dist.astype(F32)[None]), NEG_INF)
    vt_rows = wavt.shape[3]
    n_tiles = B * n_t

    def q_map(g, j):
        seq, tile = _da_seq_tile(j, n_t, n_tiles)
        return (seq, tile, g)

    def o_map(g, j):
        seq, tile = _da_seq_tile(j - 1, n_t, n_tiles)
        return (seq, tile, g)

    return pl.pallas_call(
        functools.partial(_win_attn_kernel, n_t=n_t),
        grid=(WA_KV_HEADS // WA_KV_PER_STEP, n_tiles + 1),
        in_specs=[pl.BlockSpec(memory_space=pltpu.SMEM),
                  pl.BlockSpec((1, tq, gw), q_map),
                  pl.BlockSpec((1, S, gw), lambda g, j: (_da_seq_tile(j, n_t, n_tiles)[0], 0, g)),
                  pl.BlockSpec((1, WA_KV_PER_STEP, S // LANES, vt_rows, LANES),
                               lambda g, j: (_da_seq_tile(j - 1, n_t, n_tiles)[0], g, 0, 0, 0)),
                  pl.BlockSpec((heads, span + 2 * WINDOW, tq), lambda g, j: (g, 0, 0),
                               pipeline_mode=pl.Buffered(1))],
        out_specs=pl.BlockSpec((1, tq, gw), o_map),
        out_shape=jax.ShapeDtypeStruct((B, S, D), BF16),
        scratch_shapes=[pltpu.VMEM((heads, span, tq), F32), pltpu.VMEM((heads, tq), F32)],
        compiler_params=_cparams(("parallel", "arbitrary")),
        name="win_attn",
    )(sink, waq, wak, wavt, tab)


def _mem_attn_head(q_ref, kv_ref, hh):
    hd = XA_HEAD_DIM
    q = q_ref[0, :, hh * hd:(hh + 1) * hd]
    k = kv_ref[0, :, hh * hd:(hh + 1) * hd]
    v = kv_ref[0, :, (XA_HEADS + hh) * hd:(XA_HEADS + hh + 1) * hd]
    s = _dot_nt(q, k)
    p = jnp.exp2(s - jnp.max(s, axis=1, keepdims=True))
    pn = (p * (1.0 / jnp.sum(p, axis=1, keepdims=True))).astype(BF16)
    return _dot(pn, v).astype(BF16)


def _merge_kernel(x_ref, oda_ref, owa_ref, xaq_ref, memkv_ref, g1_ref, wg_ref, wda_ref, wwa_ref, wxa_ref,
                  wout_ref, g2_ref, wr_ref, x1_ref, h2_ref, aff_ref, x1_scr):
    d = x_ref.shape[-1]
    hd = XA_HEAD_DIM

    @pl.when(pl.program_id(0) == 0)
    def _():
        x1_scr[...] = jnp.zeros_like(x1_scr)

    def previous_tile_tail():
        h2 = _rms(x1_scr[...], g2_ref[...]).astype(BF16)
        h2_ref[0] = h2
        logits = _dot(h2, wr_ref[...])
        lt = logits.T[:N_EXPERTS, :]
        e = jnp.exp(lt - jnp.max(lt, axis=0, keepdims=True))
        aff_ref[0] = e / jnp.sum(e, axis=0, keepdims=True)

    x = x_ref[0]
    h = _rms(x, g1_ref[...]).astype(BF16)

    def gated(i, branch):
        return jax.nn.sigmoid(_dot(h, wg_ref[:, i * d:(i + 1) * d])) * branch

    xa = [_mem_attn_head(xaq_ref, memkv_ref, 0)]
    merged = gated(0, _dot(oda_ref[0], wda_ref[...]))
    xa.append(_mem_attn_head(xaq_ref, memkv_ref, 1))
    merged = merged + gated(1, _dot(owa_ref[0], wwa_ref[...]))
    xa.extend(_mem_attn_head(xaq_ref, memkv_ref, hh) for hh in range(2, XA_HEADS))
    xa_proj = None
    for hh, o in enumerate(xa):
        part = _dot(o, wxa_ref[hh * hd:(hh + 1) * hd, :])
        xa_proj = part if xa_proj is None else xa_proj + part
    merged = merged + gated(2, xa_proj)
    previous_tile_tail()
    x1 = x + _dot(merged.astype(BF16), wout_ref[...])
    x1_ref[0] = x1
    x1_scr[...] = x1


def _merge(x, oda, owa, xaq, memkv, g1, wg, wda, wwa, wxa, wout, g2, wr, tm):
    B, S, D = x.shape
    n_t = S // tm
    n_tiles = B * n_t

    def cur(j):
        return _da_seq_tile(j, n_t, n_tiles)

    def prev(j):
        return _da_seq_tile(j - 1, n_t, n_tiles)

    tok = pl.BlockSpec((1, tm, D), lambda j: cur(j) + (0,))
    tok_prev = pl.BlockSpec((1, tm, D), lambda j: prev(j) + (0,))
    full = lambda a: _resident(a.shape, lambda j: (0,) * a.ndim)
    return pl.pallas_call(
        _merge_kernel,
        grid=(n_tiles + 1,),
        in_specs=[tok, tok, tok, tok, pl.BlockSpec((1,) + memkv.shape[1:], lambda j: (cur(j)[0], 0, 0)),
                  full(g1), full(wg), full(wda), full(wwa), full(wxa), full(wout), full(g2), full(wr)],
        out_specs=[tok, tok_prev, pl.BlockSpec((1, N_EXPERTS, tm), lambda j: (prev(j)[0], 0, prev(j)[1]))],
        out_shape=[jax.ShapeDtypeStruct((B, S, D), F32),
                   jax.ShapeDtypeStruct((B, S, D), BF16),
                   jax.ShapeDtypeStruct((B, N_EXPERTS, S), F32)],
        scratch_shapes=[pltpu.VMEM((tm, D), F32)],
        compiler_params=_cparams(("arbitrary",)),
        name="merge",
    )(x, oda, owa, xaq, memkv, g1, wg, wda, wwa, wxa, wout, g2, wr)


def _route_kernel(aff_ref, pos_ref, *, cap):
    nb, ne, S = aff_ref.shape
    E = nb * ne
    aff = aff_ref[...].reshape(E, S)
    bits = pltpu.bitcast(aff, jnp.int32)
    prefix = jnp.zeros((E, 1), jnp.int32)
    for bit in range(30, -1, -1):
        cand = prefix | (1 << bit)
        cnt = jnp.sum(jnp.where(bits >= cand, 1.0, 0.0), axis=1, keepdims=True)
        prefix = jnp.where(cnt >= cap, cand, prefix)
    gt = bits > prefix
    eq = bits == prefix
    blk = 2 * LANES
    tri = (lax.broadcasted_iota(jnp.int32, (blk, blk), 0) <= lax.broadcasted_iota(jnp.int32, (blk, blk), 1))
    tri = jnp.where(tri, 1.0, 0.0).astype(BF16)

    def cumsum(mask):
        mb = jnp.where(mask, 1.0, 0.0).astype(BF16)
        carry = jnp.zeros((E, 1), F32)
        outs = []
        for i in range(S // blk):
            cs = _dot(mb[:, i * blk:(i + 1) * blk], tri) + carry
            outs.append(cs)
            carry = cs[:, blk - 1:blk]
        return jnp.concatenate(outs, axis=1), carry

    cs_gt, n_gt = cumsum(gt)
    cs_eq, _ = cumsum(eq)
    need = cap - n_gt
    sel = gt | (eq & (cs_eq <= need))
    slot = cs_gt + jnp.minimum(cs_eq, need) - 1.0
    pos_ref[...] = jnp.where(sel, slot, -1.0).astype(jnp.int32).reshape(nb, ne, S)


def _route(aff, cap):
    B, E, S = aff.shape
    nb = ROUTE_SEQS_PER_STEP if B % ROUTE_SEQS_PER_STEP == 0 else 1
    return pl.pallas_call(
        functools.partial(_route_kernel, cap=cap),
        grid=(B // nb,),
        in_specs=[pl.BlockSpec((nb, E, S), lambda b: (b, 0, 0))],
        out_specs=pl.BlockSpec((nb, E, S), lambda b: (b, 0, 0)),
        out_shape=jax.ShapeDtypeStruct((B, E, S), jnp.int32),
        compiler_params=_cparams(("parallel",)),
        name="route",
    )(aff)


def _moe_kernel(pos_ref, aff_ref, h2_ref, wg_ref, wu_ref, wd_ref, o_ref, p_scr, g_scr, xe_scr, y_scr, *, cap):
    e = pl.program_id(1)
    f = pl.program_id(2)
    nf = pl.num_programs(2)
    S = h2_ref.shape[1]

    @pl.when((e == 0) & (f == 0))
    def _():
        o_ref[...] = jnp.zeros_like(o_ref)

    @pl.when(f == 0)
    def _():
        slot = lax.broadcasted_iota(jnp.int32, (cap, S), 0)
        hit = pos_ref[0, 0] == slot
        onehot = jnp.where(hit, 1.0, 0.0).astype(BF16)
        p_scr[...] = onehot
        g_scr[...] = jnp.sum(jnp.where(hit, aff_ref[0, 0], 0.0), axis=1, keepdims=True)
        xe_scr[...] = _dot(onehot, h2_ref[0]).astype(BF16)
        y_scr[...] = jnp.zeros_like(y_scr)

    xe = xe_scr[...]
    a = _dot(xe, wg_ref[0])
    u = _dot(xe, wu_ref[0])
    act = (a * jax.nn.sigmoid(a) * u).astype(BF16)
    y_scr[...] += _dot(act, wd_ref[0])

    @pl.when(f == nf - 1)
    def _():
        ye = (y_scr[...] * g_scr[...]).astype(BF16)
        o_ref[0] += _dot_tn(p_scr[...], ye)


def _moe(pos, aff, h2, wg, wu, wd, cap, tf):
    B, S, D = h2.shape
    E, _, F = wg.shape
    pos4 = pos.reshape(B, E, 1, S)
    aff4 = aff.reshape(B, E, 1, S)
    row_spec = pl.BlockSpec((1, 1, 1, S), lambda b, e, f: (b, e, 0, 0))
    return pl.pallas_call(
        functools.partial(_moe_kernel, cap=cap),
        grid=(B, E, F // tf),
        in_specs=[row_spec, row_spec,
                  pl.BlockSpec((1, S, D), lambda b, e, f: (b, 0, 0)),
                  pl.BlockSpec((1, D, tf), lambda b, e, f: (e, 0, f)),
                  pl.BlockSpec((1, D, tf), lambda b, e, f: (e, 0, f)),
                  pl.BlockSpec((1, tf, D), lambda b, e, f: (e, f, 0))],
        out_specs=pl.BlockSpec((1, S, D), lambda b, e, f: (b, 0, 0)),
        out_shape=jax.ShapeDtypeStruct((B, S, D), F32),
        scratch_shapes=[pltpu.VMEM((cap, S), BF16), pltpu.VMEM((cap, 1), F32),
                        pltpu.VMEM((cap, D), BF16), pltpu.VMEM((cap, D), F32)],
        compiler_params=_cparams(("parallel", "arbitrary", "arbitrary")),
        name="moe",
    )(pos4, aff4, h2, wg, wu, wd)


def _residual_kernel(x_ref, d_ref, o_ref):
    o_ref[0] = x_ref[0] + d_ref[0]


def _residual_norm_kernel(x_ref, d_ref, g_ref, o_ref):
    o_ref[0] = _rms(x_ref[0] + d_ref[0], g_ref[...])


def _residual(x1, delta, g, tm):
    B, S, D = x1.shape
    tok = pl.BlockSpec((1, tm, D), lambda b, i: (b, i, 0))
    if g is None:
        body, extra, extra_specs = _residual_kernel, (), []
    else:
        body, extra, extra_specs = _residual_norm_kernel, (g,), [_resident((1, D), lambda b, i: (0, 0))]
    return pl.pallas_call(
        body,
        grid=(B, S // tm),
        in_specs=[tok, tok] + extra_specs,
        out_specs=tok,
        out_shape=jax.ShapeDtypeStruct((B, S, D), F32),
        compiler_params=_cparams(("parallel", "parallel")),
        name="residual",
    )(x1, delta, *extra)


def _tile(n, pref):
    t = min(n, pref)
    assert n % t == 0, (n, t)
    return t


def _lambda_init(layer):
    return 0.8 - 0.6 * float(np.exp(-0.3 * layer))


def _lane_half_forms(w, n_heads, width):
    d = w.shape[0]
    w = w.reshape(d, n_heads, width)
    z = jnp.zeros_like(w)
    return jnp.concatenate([w, z, z, w], axis=-1).reshape(d, n_heads * 4 * width)


def kernel(x, mem, attn_norm_g, mem_norm_g, w_in, w_mem_kv, da_lambda_q1, da_lambda_k1, da_lambda_q2,
           da_lambda_k2, da_subln_g, wa_sink, w_da_o, w_wa_o, w_xa_o, w_out, ffn_norm_g, w_router,
           w_exp_gate, w_exp_up, w_exp_down, final_norm_g):
    B, S, D = x.shape
    depth = w_in.shape[0]
    cap = max(1, EC_FACTOR * S // N_EXPERTS)
    row = lambda v: v.reshape(1, -1).astype(F32)
    da_w = DA_HEADS * 2 * DA_HEAD_DIM
    wa_kv_w = WA_KV_HEADS * WA_HEAD_DIM
    o_waq = 3 * da_w
    o_wak = o_waq + WA_HEADS * WA_HEAD_DIM
    o_wav = o_wak + wa_kv_w
    o_xaq = o_wav + wa_kv_w
    o_gate = o_xaq + XA_HEADS * XA_HEAD_DIM
    for l in range(depth):
        lam_init = _lambda_init(l)
        wl = w_in[l]
        w1 = jnp.concatenate([wl[:, :o_wak],
                              _lane_half_forms(wl[:, o_wak:o_wav], WA_KV_HEADS, WA_HEAD_DIM),
                              wl[:, o_wav:o_gate]], axis=1).astype(BF16)
        wg = wl[:, o_gate:].astype(BF16)
        wr = jnp.pad(w_router[l], ((0, 0), (0, LANES - N_EXPERTS))).astype(BF16)

        daq, dak, dav, waq, wak, wavt, xaq = _in_proj(x, row(attn_norm_g[l]), w1, _tile(S, 1024))
        memkv = _mem_proj(mem, row(mem_norm_g[l]), w_mem_kv[l].astype(BF16))
        lams = [row(v[l]) for v in (da_lambda_q1, da_lambda_k1, da_lambda_q2, da_lambda_k2)]
        oda = _diff_attn(daq, dak, dav, lams, da_subln_g[l].astype(F32), lam_init, _tile(S, 1024))
        owa = _win_attn(waq, wak, wavt, wa_sink[l].astype(F32), _tile(S, 256))
        x1, h2, aff = _merge(x, oda, owa, xaq, memkv, row(attn_norm_g[l]), wg, w_da_o[l].astype(BF16),
                             w_wa_o[l].astype(BF16), w_xa_o[l].astype(BF16), w_out[l].astype(BF16),
                             row(ffn_norm_g[l]), wr, _tile(S, 512))
        pos = _route(aff, cap)
        delta = _moe(pos, aff, h2, w_exp_gate[l].astype(BF16), w_exp_up[l].astype(BF16),
                     w_exp_down[l].astype(BF16), cap, _tile(w_exp_gate.shape[-1], 2048))
        last = l == depth - 1
        x = _residual(x1, delta, row(final_norm_g) if last else None, _tile(S, 1024))
    return x
```

```python
import functools

import numpy as np
import jax
import jax.numpy as jnp
from jax import lax
from jax.experimental import pallas as pl
from jax.experimental.pallas import tpu as pltpu

EPS = 1e-6
WINDOW = 128
DA_HEADS = 8
DA_HEAD_DIM = 64
DA_V_DIM = 2 * DA_HEAD_DIM
WA_HEADS = 16
WA_KV_HEADS = 4
WA_GROUPS = WA_HEADS // WA_KV_HEADS
WA_HEAD_DIM = 64
XA_HEADS = 4
XA_HEAD_DIM = 256
N_EXPERTS = 16
EC_FACTOR = 2

LOG2E = 1.4426950408889634
DA_Q_SCALE = DA_HEAD_DIM ** -0.5 * LOG2E
XA_Q_SCALE = XA_HEAD_DIM ** -0.5 * LOG2E
WA_Q_SCALE = WA_HEAD_DIM ** -0.5 * LOG2E
WA_ONES_ROWS = 16
WA_KV_PER_STEP = 4
DA_KEY_CHUNK = 256
DA_ONES_ROWS = 16
DA_DRAIN_STEPS = 2
ROUTE_SEQS_PER_STEP = 4
MEM_SEQS_PER_STEP = 4

LANES = 128
VMEM_LIMIT = 56 * 1024 * 1024

F32 = jnp.float32
BF16 = jnp.bfloat16
NEG_INF = float("-inf")


def _cparams(sem):
    return pltpu.CompilerParams(dimension_semantics=sem, vmem_limit_bytes=VMEM_LIMIT)


def _rms(xf, g_row):
    ms = jnp.mean(xf * xf, axis=-1, keepdims=True)
    return xf * lax.rsqrt(ms + EPS) * g_row


def _dot(a, b):
    return jnp.dot(a, b, preferred_element_type=F32)


def _dot_nt(a, b):
    return lax.dot_general(a, b, (((1,), (1,)), ((), ())), preferred_element_type=F32)


def _dot_tn(a, b):
    return lax.dot_general(a, b, (((0,), (0,)), ((), ())), preferred_element_type=F32)


def _resident(shape, index_map):
    return pl.BlockSpec(shape, index_map, pipeline_mode=pl.Buffered(1))


def _in_proj_kernel(x_ref, g_ref, w_ref, daq_ref, dak_ref, dav_ref, waq_ref, wak_ref, wavt_ref, xaq_ref):
    d = x_ref.shape[-1]
    tm = x_ref.shape[1]
    h = _rms(x_ref[0], g_ref[...]).astype(BF16)
    col = [0]

    def sec(width):
        r = _dot(h, w_ref[:, col[0]:col[0] + width])
        col[0] += width
        return r

    r = (sec(d) * DA_Q_SCALE).astype(BF16)
    for hh in range(DA_HEADS):
        daq_ref[0, hh] = r[:, hh * LANES:(hh + 1) * LANES]
    r = sec(d).astype(BF16)
    low_half = lax.broadcasted_iota(jnp.int32, (tm, LANES), 1) < DA_HEAD_DIM
    zero = jnp.zeros((tm, LANES), BF16)
    for hh in range(DA_HEADS):
        kk = r[:, hh * LANES:(hh + 1) * LANES]
        dak_ref[0, hh, 0] = jnp.where(low_half, kk, zero)
        dak_ref[0, hh, 1] = jnp.where(low_half, zero, kk)
    r = sec(d)
    for hh in range(DA_HEADS):
        dav_ref[0, hh, 0:DA_V_DIM, :] = r[:, hh * LANES:(hh + 1) * LANES].T.astype(BF16)
        dav_ref[0, hh, DA_V_DIM:, :] = jnp.ones((DA_ONES_ROWS, tm), BF16)
    waq_ref[0] = (sec(d) * WA_Q_SCALE).astype(BF16)
    wak_ref[0] = sec(d).astype(BF16)
    vt = sec(WA_KV_HEADS * WA_HEAD_DIM).T
    ones = jnp.ones((WA_ONES_ROWS, LANES), BF16)
    for hh in range(WA_KV_HEADS):
        for kb in range(tm // LANES):
            wavt_ref[0, hh, kb, 0:WA_HEAD_DIM, :] = vt[hh * WA_HEAD_DIM:(hh + 1) * WA_HEAD_DIM,
                                                       kb * LANES:(kb + 1) * LANES].astype(BF16)
            wavt_ref[0, hh, kb, WA_HEAD_DIM:, :] = ones
    xaq_ref[0] = (sec(d) * XA_Q_SCALE).astype(BF16)


def _in_proj(x, g, w1, tm):
    B, S, D = x.shape
    n_t = S // tm
    head_shape = jax.ShapeDtypeStruct((B, DA_HEADS, S, LANES), BF16)
    head_spec = pl.BlockSpec((1, DA_HEADS, tm, LANES), lambda b, i: (b, 0, i, 0))
    tok = pl.BlockSpec((1, tm, D), lambda b, i: (b, i, 0))
    tok_shape = jax.ShapeDtypeStruct((B, S, D), BF16)
    vt_rows = WA_HEAD_DIM + WA_ONES_ROWS
    return pl.pallas_call(
        _in_proj_kernel,
        grid=(B, n_t),
        in_specs=[tok, _resident((1, D), lambda b, i: (0, 0)), _resident(w1.shape, lambda b, i: (0, 0))],
        out_specs=[head_spec,
                   pl.BlockSpec((1, DA_HEADS, 2, tm, LANES), lambda b, i: (b, 0, 0, i, 0)),
                   pl.BlockSpec((1, DA_HEADS, DA_V_DIM + DA_ONES_ROWS, tm), lambda b, i: (b, 0, 0, i)),
                   tok, tok,
                   pl.BlockSpec((1, WA_KV_HEADS, tm // LANES, vt_rows, LANES), lambda b, i: (b, 0, i, 0, 0)),
                   tok],
        out_shape=[head_shape,
                   jax.ShapeDtypeStruct((B, DA_HEADS, 2, S, LANES), BF16),
                   jax.ShapeDtypeStruct((B, DA_HEADS, DA_V_DIM + DA_ONES_ROWS, S), BF16),
                   tok_shape, tok_shape,
                   jax.ShapeDtypeStruct((B, WA_KV_HEADS, S // LANES, vt_rows, LANES), BF16),
                   tok_shape],
        compiler_params=_cparams(("parallel", "parallel")),
        name="in_proj",
    )(x, g, w1)


def _mem_proj_kernel(m_ref, g_ref, w_ref, o_ref):
    nb, rows, d = m_ref.shape
    h = _rms(m_ref[...].reshape(nb * rows, d), g_ref[...]).astype(BF16)
    o_ref[...] = _dot(h, w_ref[...]).astype(BF16).reshape(nb, rows, -1)


def _mem_proj(mem, g, w):
    B, M, D = mem.shape
    N = w.shape[1]
    nb = MEM_SEQS_PER_STEP if B % MEM_SEQS_PER_STEP == 0 else 1
    return pl.pallas_call(
        _mem_proj_kernel,
        grid=(B // nb,),
        in_specs=[pl.BlockSpec((nb, M, D), lambda b: (b, 0, 0)),
                  _resident((1, D), lambda b: (0, 0)),
                  _resident((D, N), lambda b: (0, 0))],
        out_specs=pl.BlockSpec((nb, M, N), lambda b: (b, 0, 0)),
        out_shape=jax.ShapeDtypeStruct((B, M, N), BF16),
        compiler_params=_cparams(("parallel",)),
        name="mem_proj",
    )(mem, g, w)


def _diff_attn_kernel(lq1_ref, lk1_ref, lq2_ref, lk2_ref, q_ref, k_ref, vt_ref, bias_ref, g_ref, o_ref,
                      s_scr, m_scr, e_scr, *, lam_init, n_t):
    S = k_ref.shape[3]
    tq = q_ref.shape[2]
    j = pl.program_id(1)
    n_tiles = pl.num_programs(1) - DA_DRAIN_STEPS
    ck = DA_KEY_CHUNK
    _, tile = _da_seq_tile(j, n_t, n_tiles)
    k1_scr, k2_scr = k_ref.at[0, 0, 0], k_ref.at[0, 0, 1]

    @pl.when(j == 0)
    def _():
        s_scr[...] = jnp.zeros_like(s_scr)
        m_scr[...] = jnp.zeros_like(m_scr)
        e_scr[...] = jnp.ones_like(e_scr)

    lam = (jnp.exp(jnp.sum(lq1_ref[...] * lk1_ref[...], keepdims=True))
           - jnp.exp(jnp.sum(lq2_ref[...] * lk2_ref[...], keepdims=True)) + lam_init)

    q = q_ref[0, 0]
    r0 = (S - tq) - tile * tq

    def stage3():
        o1, l1 = e_scr[0, 0:DA_V_DIM, :], e_scr[0, DA_V_DIM:DA_V_DIM + 1, :]
        o2, l2 = e_scr[1, 0:DA_V_DIM, :], e_scr[1, DA_V_DIM:DA_V_DIM + 1, :]
        ot = o1 * (1.0 / l1) - o2 * (lam / l2)
        ms = jnp.mean(ot * ot, axis=0, keepdims=True)
        y = ot * lax.rsqrt(ms + EPS) * g_ref[...] * (1.0 - lam_init)
        o_ref[0] = y.T.astype(BF16)

    m_prev = (m_scr[0:1, :], m_scr[1:2, :])
    m_new = [None, None]
    acc = [None, None]
    n_chunks = S // ck
    for c in range(n_chunks):
        if c == n_chunks // 2:
            stage3()
        rows = slice(c * ck, (c + 1) * ck)
        bias = bias_ref[0, pl.ds(pl.multiple_of(r0 + c * ck, ck), ck), :]
        half = ck // 2
        for i, k_scr in enumerate((k1_scr, k2_scr)):
            p_lo = jnp.exp2((s_scr[i, c * ck:c * ck + half, :] - m_prev[i]).astype(BF16))
            s = _dot_nt(k_scr[rows, :], q) + bias
            p_hi = jnp.exp2((s_scr[i, c * ck + half:(c + 1) * ck, :] - m_prev[i]).astype(BF16))
            d = _dot(vt_ref[0, 0, :, rows], jnp.concatenate([p_lo, p_hi], axis=0))
            acc[i] = d if acc[i] is None else acc[i] + d
            s_scr[i, rows, :] = s
            cm = jnp.max(s, axis=0, keepdims=True)
            m_new[i] = cm if m_new[i] is None else jnp.maximum(m_new[i], cm)
    m_scr[0:1, :] = m_new[0]
    m_scr[1:2, :] = m_new[1]
    e_scr[0] = acc[0]
    e_scr[1] = acc[1]


def _da_seq_tile(j, n_t, n_tiles):
    jc = jnp.clip(j, 0, n_tiles - 1)
    return jc // n_t, jc % n_t


def _diff_attn(daq, dak, dav, lams, subln_g, lam_init, tq):
    B, H, S, _ = daq.shape
    n_t = S // tq
    rows = 2 * S - tq
    slopes = jnp.exp2(-8.0 * jnp.arange(1, H + 1, dtype=F32) / H)
    r = jnp.arange(rows, dtype=jnp.int32)[:, None]
    c = jnp.arange(tq, dtype=jnp.int32)[None, :]
    dist = jnp.abs(r - (S - tq) - c).astype(F32)
    bias = -((slopes * LOG2E)[:, None, None] * dist[None])
    n_tiles = B * n_t
    lam_spec = _resident((1, DA_HEAD_DIM), lambda h, j: (0, 0))

    def kv_map(h, j):
        return (_da_seq_tile(j, n_t, n_tiles)[0], h, 0, 0)

    def q_map(h, j):
        seq, tile = _da_seq_tile(j, n_t, n_tiles)
        return (seq, h, tile, 0)

    def o_map(h, j):
        seq, tile = _da_seq_tile(j - DA_DRAIN_STEPS, n_t, n_tiles)
        return (seq, tile, h)

    def vt_map(h, j):
        return (_da_seq_tile(j - 1, n_t, n_tiles)[0], h, 0, 0)

    kv_spec = pl.BlockSpec((1, 1, 2, S, LANES), lambda h, j: kv_map(h, j) + (0,))
    vt_spec = pl.BlockSpec((1, 1, DA_V_DIM + DA_ONES_ROWS, S), vt_map)
    acc_shape = pltpu.VMEM((2, DA_V_DIM + DA_ONES_ROWS, tq), F32)
    return pl.pallas_call(
        functools.partial(_diff_attn_kernel, lam_init=lam_init, n_t=n_t),
        grid=(H, n_tiles + DA_DRAIN_STEPS),
        in_specs=[lam_spec, lam_spec, lam_spec, lam_spec,
                  pl.BlockSpec((1, 1, tq, LANES), q_map),
                  kv_spec, vt_spec,
                  pl.BlockSpec((1, rows, tq), lambda h, j: (h, 0, 0), pipeline_mode=pl.Buffered(1)),
                  _resident((DA_V_DIM, 1), lambda h, j: (0, 0))],
        out_specs=pl.BlockSpec((1, tq, LANES), o_map),
        out_shape=jax.ShapeDtypeStruct((B, S, H * LANES), BF16),
        scratch_shapes=[pltpu.VMEM((2, S, tq), F32), pltpu.VMEM((2, tq), F32), acc_shape],
        compiler_params=_cparams(("parallel", "arbitrary")),
        name="diff_attn",
    )(*lams, daq, dak, dav, bias, subln_g.reshape(DA_V_DIM, 1))


def _win_attn_kernel(sink_ref, q_ref, k_ref, vt_ref, tab_ref, o_ref, s_scr, m_scr, *, n_t):
    S = k_ref.shape[1]
    tq = q_ref.shape[1]
    span = tq + 2 * WINDOW
    g = pl.program_id(0)
    j = pl.program_id(1)
    n_tiles = pl.num_programs(1) - 1
    _, tile = _da_seq_tile(j, n_t, n_tiles)
    _, prev_tile = _da_seq_tile(j - 1, n_t, n_tiles)

    def window_start(t):
        return pl.multiple_of(jnp.clip(t * tq - WINDOW, 0, S - span), LANES)

    def variant_of(t):
        return jnp.where(t == 0, 0, jnp.where(t == n_t - 1, 2, 1))

    @pl.when(j == 0)
    def _():
        s_scr[...] = jnp.zeros_like(s_scr)
        m_scr[...] = jnp.zeros_like(m_scr)

    tab_row = pl.multiple_of(2 * WINDOW - variant_of(tile) * WINDOW, LANES)
    kwin = k_ref[0, pl.ds(window_start(tile), span), :]
    kb0 = window_start(prev_tile) // LANES
    heads = WA_KV_PER_STEP * WA_GROUPS
    n_kb = span // LANES
    n_qb = tq // LANES

    def step(prev_variant):
        prev_shift = prev_variant * WINDOW

        def probs(jh, kb, m):
            cols = []
            for qb in range(n_qb):
                if abs(prev_shift + (qb - kb) * LANES) < 2 * LANES:
                    x = s_scr[jh, kb * LANES:(kb + 1) * LANES, qb * LANES:(qb + 1) * LANES]
                    cols.append(jnp.exp2((x - m[:, qb * LANES:(qb + 1) * LANES]).astype(BF16)))
                else:
                    cols.append(jnp.zeros((LANES, LANES), BF16))
            return jnp.concatenate(cols, axis=1)

        outs = []
        for jh in range(heads):
            kv, pair_half = divmod(jh, WA_GROUPS)
            pair, half = divmod(pair_half, 2)
            sink = sink_ref[g * heads + jh] * LOG2E
            m = m_scr[jh:jh + 1, :]
            qcol = (kv * 2 + pair) * LANES
            kcol = (kv * 2 + half) * LANES
            qp = q_ref[0, :, qcol:qcol + LANES]
            ps = [probs(jh, kb, m) for kb in range(n_kb // 2)]
            s = _dot_nt(kwin[:, kcol:kcol + LANES], qp) + tab_ref[jh, pl.ds(tab_row, span), :]
            ps += [probs(jh, kb, m) for kb in range(n_kb // 2, n_kb)]
            acc = None
            for kb, p in enumerate(ps):
                d = _dot(vt_ref[0, kv, kb0 + kb], p)
                acc = d if acc is None else acc + d
            l = acc[WA_HEAD_DIM:WA_HEAD_DIM + 1, :] + jnp.exp2(sink - m)
            outs.append(acc[0:WA_HEAD_DIM, :] * (1.0 / l))
            s_scr[jh] = s
            m_scr[jh:jh + 1, :] = jnp.maximum(jnp.max(s, axis=0, keepdims=True), sink)
        o_ref[0] = jnp.concatenate(outs, axis=0).T.astype(BF16)

    prev_variant = variant_of(prev_tile)
    for v in range(3):
        pl.when(prev_variant == v)(functools.partial(step, v))


def _win_attn(waq, wak, wavt, sink, tq):
    B, S, D = waq.shape
    n_t = S // tq
    span = tq + 2 * WINDOW
    assert n_t >= 2 and S >= span, (S, tq)
    heads = WA_KV_PER_STEP * WA_GROUPS
    gw = heads * WA_HEAD_DIM
    slopes = jnp.exp2(-8.0 * jnp.arange(1, WA_HEADS + 1, dtype=F32) / WA_HEADS) * LOG2E
    r = jnp.arange(span + 2 * WINDOW, dtype=jnp.int32)[:, None]
    ir = jnp.arange(tq, dtype=jnp.int32)[None, :]
    dist = jnp.abs(2 * WINDOW + ir - r)
    tab = jnp.where((dist <= WINDOW)[None], -(slopes[:, None, None] * dist.astype(F32)[None]), NEG_INF)
    vt_rows = wavt.shape[3]
    n_tiles = B * n_t

    def q_map(g, j):
        seq, tile = _da_seq_tile(j, n_t, n_tiles)
        return (seq, tile, g)

    def o_map(g, j):
        seq, tile = _da_seq_tile(j - 1, n_t, n_tiles)
        return (seq, tile, g)

    return pl.pallas_call(
        functools.partial(_win_attn_kernel, n_t=n_t),
        grid=(WA_KV_HEADS // WA_KV_PER_STEP, n_tiles + 1),
        in_specs=[pl.BlockSpec(memory_space=pltpu.SMEM),
                  pl.BlockSpec((1, tq, gw), q_map),
                  pl.BlockSpec((1, S, gw), lambda g, j: (_da_seq_tile(j, n_t, n_tiles)[0], 0, g)),
                  pl.BlockSpec((1, WA_KV_PER_STEP, S // LANES, vt_rows, LANES),
                               lambda g, j: (_da_seq_tile(j - 1, n_t, n_tiles)[0], g, 0, 0, 0)),
                  pl.BlockSpec((heads, span + 2 * WINDOW, tq), lambda g, j: (g, 0, 0),
                               pipeline_mode=pl.Buffered(1))],
        out_specs=pl.BlockSpec((1, tq, gw), o_map),
        out_shape=jax.ShapeDtypeStruct((B, S, D), BF16),
        scratch_shapes=[pltpu.VMEM((heads, span, tq), F32), pltpu.VMEM((heads, tq), F32)],
        compiler_params=_cparams(("parallel", "arbitrary")),
        name="win_attn",
    )(sink, waq, wak, wavt, tab)


def _mem_attn_head(q_ref, kv_ref, hh):
    hd = XA_HEAD_DIM
    q = q_ref[0, :, hh * hd:(hh + 1) * hd]
    k = kv_ref[0, :, hh * hd:(hh + 1) * hd]
    v = kv_ref[0, :, (XA_HEADS + hh) * hd:(XA_HEADS + hh + 1) * hd]
    s = _dot_nt(q, k)
    p = jnp.exp2(s - jnp.max(s, axis=1, keepdims=True))
    pn = (p * (1.0 / jnp.sum(p, axis=1, keepdims=True))).astype(BF16)
    return _dot(pn, v).astype(BF16)


def _merge_kernel(x_ref, oda_ref, owa_ref, xaq_ref, memkv_ref, g1_ref, wg_ref, wda_ref, wwa_ref, wxa_ref,
                  wout_ref, g2_ref, wr_ref, x1_ref, h2_ref, aff_ref, x1_scr):
    d = x_ref.shape[-1]
    hd = XA_HEAD_DIM

    @pl.when(pl.program_id(0) == 0)
    def _():
        x1_scr[...] = jnp.zeros_like(x1_scr)

    def previous_tile_tail():
        h2 = _rms(x1_scr[...], g2_ref[...]).astype(BF16)
        h2_ref[0] = h2
        logits = _dot(h2, wr_ref[...])
        lt = logits.T[:N_EXPERTS, :]
        e = jnp.exp(lt - jnp.max(lt, axis=0, keepdims=True))
        aff_ref[0] = e / jnp.sum(e, axis=0, keepdims=True)

    x = x_ref[0]
    h = _rms(x, g1_ref[...]).astype(BF16)

    def gated(i, branch):
        return jax.nn.sigmoid(_dot(h, wg_ref[:, i * d:(i + 1) * d])) * branch

    xa = [_mem_attn_head(xaq_ref, memkv_ref, 0)]
    merged = gated(0, _dot(oda_ref[0], wda_ref[...]))
    xa.append(_mem_attn_head(xaq_ref, memkv_ref, 1))
    merged = merged + gated(1, _dot(owa_ref[0], wwa_ref[...]))
    xa.extend(_mem_attn_head(xaq_ref, memkv_ref, hh) for hh in range(2, XA_HEADS))
    xa_proj = None
    for hh, o in enumerate(xa):
        part = _dot(o, wxa_ref[hh * hd:(hh + 1) * hd, :])
        xa_proj = part if xa_proj is None else xa_proj + part
    merged = merged + gated(2, xa_proj)
    previous_tile_tail()
    x1 = x + _dot(merged.astype(BF16), wout_ref[...])
    x1_ref[0] = x1
    x1_scr[...] = x1


def _merge(x, oda, owa, xaq, memkv, g1, wg, wda, wwa, wxa, wout, g2, wr, tm):
    B, S, D = x.shape
    n_t = S // tm
    n_tiles = B * n_t

    def cur(j):
        return _da_seq_tile(j, n_t, n_tiles)

    def prev(j):
        return _da_seq_tile(j - 1, n_t, n_tiles)

    tok = pl.BlockSpec((1, tm, D), lambda j: cur(j) + (0,))
    tok_prev = pl.BlockSpec((1, tm, D), lambda j: prev(j) + (0,))
    full = lambda a: _resident(a.shape, lambda j: (0,) * a.ndim)
    return pl.pallas_call(
        _merge_kernel,
        grid=(n_tiles + 1,),
        in_specs=[tok, tok, tok, tok, pl.BlockSpec((1,) + memkv.shape[1:], lambda j: (cur(j)[0], 0, 0)),
                  full(g1), full(wg), full(wda), full(wwa), full(wxa), full(wout), full(g2), full(wr)],
        out_specs=[tok, tok_prev, pl.BlockSpec((1, N_EXPERTS, tm), lambda j: (prev(j)[0], 0, prev(j)[1]))],
        out_shape=[jax.ShapeDtypeStruct((B, S, D), F32),
                   jax.ShapeDtypeStruct((B, S, D), BF16),
                   jax.ShapeDtypeStruct((B, N_EXPERTS, S), F32)],
        scratch_shapes=[pltpu.VMEM((tm, D), F32)],
        compiler_params=_cparams(("arbitrary",)),
        name="merge",
    )(x, oda, owa, xaq, memkv, g1, wg, wda, wwa, wxa, wout, g2, wr)


def _route_kernel(aff_ref, pos_ref, *, cap):
    nb, ne, S = aff_ref.shape
    E = nb * ne
    aff = aff_ref[...].reshape(E, S)
    bits = pltpu.bitcast(aff, jnp.int32)
    prefix = jnp.zeros((E, 1), jnp.int32)
    for bit in range(30, -1, -1):
        cand = prefix | (1 << bit)
        cnt = jnp.sum(jnp.where(bits >= cand, 1.0, 0.0), axis=1, keepdims=True)
        prefix = jnp.where(cnt >= cap, cand, prefix)
    gt = bits > prefix
    eq = bits == prefix
    blk = 2 * LANES
    tri = (lax.broadcasted_iota(jnp.int32, (blk, blk), 0) <= lax.broadcasted_iota(jnp.int32, (blk, blk), 1))
    tri = jnp.where(tri, 1.0, 0.0).astype(BF16)

    def cumsum(mask):
        mb = jnp.where(mask, 1.0, 0.0).astype(BF16)
        carry = jnp.zeros((E, 1), F32)
        outs = []
        for i in range(S // blk):
            cs = _dot(mb[:, i * blk:(i + 1) * blk], tri) + carry
            outs.append(cs)
            carry = cs[:, blk - 1:blk]
        return jnp.concatenate(outs, axis=1), carry

    cs_gt, n_gt = cumsum(gt)
    cs_eq, _ = cumsum(eq)
    need = cap - n_gt
    sel = gt | (eq & (cs_eq <= need))
    slot = cs_gt + jnp.minimum(cs_eq, need) - 1.0
    pos_ref[...] = jnp.where(sel, slot, -1.0).astype(jnp.int32).reshape(nb, ne, S)


def _route(aff, cap):
    B, E, S = aff.shape
    nb = ROUTE_SEQS_PER_STEP if B % ROUTE_SEQS_PER_STEP == 0 else 1
    return pl.pallas_call(
        functools.partial(_route_kernel, cap=cap),
        grid=(B // nb,),
        in_specs=[pl.BlockSpec((nb, E, S), lambda b: (b, 0, 0))],
        out_specs=pl.BlockSpec((nb, E, S), lambda b: (b, 0, 0)),
        out_shape=jax.ShapeDtypeStruct((B, E, S), jnp.int32),
        compiler_params=_cparams(("parallel",)),
        name="route",
    )(aff)


def _moe_kernel(pos_ref, aff_ref, h2_ref, wg_ref, wu_ref, wd_ref, o_ref, p_scr, g_scr, xe_scr, y_scr, *, cap):
    e = pl.program_id(1)
    f = pl.program_id(2)
    nf = pl.num_programs(2)
    S = h2_ref.shape[1]

    @pl.when((e == 0) & (f == 0))
    def _():
        o_ref[...] = jnp.zeros_like(o_ref)

    @pl.when(f == 0)
    def _():
        slot = lax.broadcasted_iota(jnp.int32, (cap, S), 0)
        hit = pos_ref[0, 0] == slot
        onehot = jnp.where(hit, 1.0, 0.0).astype(BF16)
        p_scr[...] = onehot
        g_scr[...] = jnp.sum(jnp.where(hit, aff_ref[0, 0], 0.0), axis=1, keepdims=True)
        xe_scr[...] = _dot(onehot, h2_ref[0]).astype(BF16)
        y_scr[...] = jnp.zeros_like(y_scr)

    xe = xe_scr[...]
    a = _dot(xe, wg_ref[0])
    u = _dot(xe, wu_ref[0])
    act = (a * jax.nn.sigmoid(a) * u).astype(BF16)
    y_scr[...] += _dot(act, wd_ref[0])

    @pl.when(f == nf - 1)
    def _():
        ye = (y_scr[...] * g_scr[...]).astype(BF16)
        o_ref[0] += _dot_tn(p_scr[...], ye)


def _moe(pos, aff, h2, wg, wu, wd, cap, tf):
    B, S, D = h2.shape
    E, _, F = wg.shape
    pos4 = pos.reshape(B, E, 1, S)
    aff4 = aff.reshape(B, E, 1, S)
    row_spec = pl.BlockSpec((1, 1, 1, S), lambda b, e, f: (b, e, 0, 0))
    return pl.pallas_call(
        functools.partial(_moe_kernel, cap=cap),
        grid=(B, E, F // tf),
        in_specs=[row_spec, row_spec,
                  pl.BlockSpec((1, S, D), lambda b, e, f: (b, 0, 0)),
                  pl.BlockSpec((1, D, tf), lambda b, e, f: (e, 0, f)),
                  pl.BlockSpec((1, D, tf), lambda b, e, f: (e, 0, f)),
                  pl.BlockSpec((1, tf, D), lambda b, e, f: (e, f, 0))],
        out_specs=pl.BlockSpec((1, S, D), lambda b, e, f: (b, 0, 0)),
        out_shape=jax.ShapeDtypeStruct((B, S, D), F32),
        scratch_shapes=[pltpu.VMEM((cap, S), BF16), pltpu.VMEM((cap, 1), F32),
                        pltpu.VMEM((cap, D), BF16), pltpu.VMEM((cap, D), F32)],
        compiler_params=_cparams(("parallel", "arbitrary", "arbitrary")),
        name="moe",
    )(pos4, aff4, h2, wg, wu, wd)


def _residual_kernel(x_ref, d_ref, o_ref):
    o_ref[0] = x_ref[0] + d_ref[0]


def _residual_norm_kernel(x_ref, d_ref, g_ref, o_ref):
    o_ref[0] = _rms(x_ref[0] + d_ref[0], g_ref[...])


def _residual(x1, delta, g, tm):
    B, S, D = x1.shape
    tok = pl.BlockSpec((1, tm, D), lambda b, i: (b, i, 0))
    if g is None:
        body, extra, extra_specs = _residual_kernel, (), []
    else:
        body, extra, extra_specs = _residual_norm_kernel, (g,), [_resident((1, D), lambda b, i: (0, 0))]
    return pl.pallas_call(
        body,
        grid=(B, S // tm),
        in_specs=[tok, tok] + extra_specs,
        out_specs=tok,
        out_shape=jax.ShapeDtypeStruct((B, S, D), F32),
        compiler_params=_cparams(("parallel", "parallel")),
        name="residual",
    )(x1, delta, *extra)


def _tile(n, pref):
    t = min(n, pref)
    assert n % t == 0, (n, t)
    return t


def _lambda_init(layer):
    return 0.8 - 0.6 * float(np.exp(-0.3 * layer))


def _lane_half_forms(w, n_heads, width):
    d = w.shape[0]
    w = w.reshape(d, n_heads, width)
    z = jnp.zeros_like(w)
    return jnp.concatenate([w, z, z, w], axis=-1).reshape(d, n_heads * 4 * width)


def kernel(x, mem, attn_norm_g, mem_norm_g, w_in, w_mem_kv, da_lambda_q1, da_lambda_k1, da_lambda_q2,
           da_lambda_k2, da_subln_g, wa_sink, w_da_o, w_wa_o, w_xa_o, w_out, ffn_norm_g, w_router,
           w_exp_gate, w_exp_up, w_exp_down, final_norm_g):
    B, S, D = x.shape
    depth = w_in.shape[0]
    cap = max(1, EC_FACTOR * S // N_EXPERTS)
    row = lambda v: v.reshape(1, -1).astype(F32)
    da_w = DA_HEADS * 2 * DA_HEAD_DIM
    wa_kv_w = WA_KV_HEADS * WA_HEAD_DIM
    o_waq = 3 * da_w
    o_wak = o_waq + WA_HEADS * WA_HEAD_DIM
    o_wav = o_wak + wa_kv_w
    o_xaq = o_wav + wa_kv_w
    o_gate = o_xaq + XA_HEADS * XA_HEAD_DIM
    for l in range(depth):
        lam_init = _lambda_init(l)
        wl = w_in[l]
        w1 = jnp.concatenate([wl[:, :o_wak],
                              _lane_half_forms(wl[:, o_wak:o_wav], WA_KV_HEADS, WA_HEAD_DIM),
                              wl[:, o_wav:o_gate]], axis=1).astype(BF16)
        wg = wl[:, o_gate:].astype(BF16)
        wr = jnp.pad(w_router[l], ((0, 0), (0, LANES - N_EXPERTS))).astype(BF16)

        daq, dak, dav, waq, wak, wavt, xaq = _in_proj(x, row(attn_norm_g[l]), w1, _tile(S, 1024))
        memkv = _mem_proj(mem, row(mem_norm_g[l]), w_mem_kv[l].astype(BF16))
        lams = [row(v[l]) for v in (da_lambda_q1, da_lambda_k1, da_lambda_q2, da_lambda_k2)]
        oda = _diff_attn(daq, dak, dav, lams, da_subln_g[l].astype(F32), lam_init, _tile(S, 1024))
        owa = _win_attn(waq, wak, wavt, wa_sink[l].astype(F32), _tile(S, 256))
        x1, h2, aff = _merge(x, oda, owa, xaq, memkv, row(attn_norm_g[l]), wg, w_da_o[l].astype(BF16),
                             w_wa_o[l].astype(BF16), w_xa_o[l].astype(BF16), w_out[l].astype(BF16),
                             row(ffn_norm_g[l]), wr, _tile(S, 512))
        pos = _route(aff, cap)
        delta = _moe(pos, aff, h2, w_exp_gate[l].astype(BF16), w_exp_up[l].astype(BF16),
                     w_exp_down[l].astype(BF16), cap, _tile(w_exp_gate.shape[-1], 2048))
        last = l == depth - 1
        x = _residual(x1, delta, row(final_norm_g) if last else None, _tile(S, 1024))
    return x
```
